```python
import jax, jax.numpy as jnp
from jax import lax
import numpy as np

D_MODEL = 1024
BATCH = 2
SEQ = 8192
DEPTH = 1

ATT_HEADS = 12
ATT_HEAD_DIM = 64
ATT_WIDTH = ATT_HEADS * ATT_HEAD_DIM
ROPE_DIM = ATT_HEAD_DIM // 4
ROPE_THETA = 500000.0
DILATED_PATTERNS = ((128, 1), (512, 4), (2048, 16))
ATT_BLOCK = 128
SSD_INNER = 2 * D_MODEL
SSD_HEAD_DIM = 64
SSD_HEADS = SSD_INNER // SSD_HEAD_DIM
SSD_GROUPS = 8
SSD_STATE = 128
SSD_CONV = 4
SSD_CHUNK = 128
SSD_CONV_CH = SSD_INNER + 2 * SSD_GROUPS * SSD_STATE
D_FF = 2816
N_BRANCH = 2
N_MOD = 9
NORM_EPS = 1e-6
IN_SIZES = (ATT_WIDTH, ATT_WIDTH, ATT_WIDTH, SSD_INNER, SSD_CONV_CH, SSD_HEADS, N_BRANCH * D_MODEL)
IN_WIDTH = sum(IN_SIZES)

kernel_name = "hybrid_gated_dilated_attn_ssd_macaron_block"


def rmsnorm(x, w):
    xf = x.astype(jnp.float32)
    y = xf * lax.rsqrt(jnp.mean(xf * xf, axis=-1, keepdims=True) + NORM_EPS)
    return (y * w.astype(jnp.float32)).astype(x.dtype)


def modulate(h, shift, scale):
    return h * (1.0 + scale[:, None, :]) + shift[:, None, :]


def swiglu(h, w_gu, w_down):
    g, u = jnp.split(h @ w_gu, 2, axis=-1)
    return (jax.nn.silu(g) * u) @ w_down


def partial_rope(t, positions):
    inv_freq = ROPE_THETA ** (-jnp.arange(0, ROPE_DIM, 2, dtype=jnp.float32) / ROPE_DIM)
    ang = positions.astype(jnp.float32)[..., None] * inv_freq
    cos, sin = jnp.cos(ang)[:, :, None, :], jnp.sin(ang)[:, :, None, :]
    tf = t.astype(jnp.float32)
    half = ROPE_DIM // 2
    t1, t2, rest = tf[..., :half], tf[..., half:ROPE_DIM], tf[..., ROPE_DIM:]
    out = jnp.concatenate([t1 * cos - t2 * sin, t2 * cos + t1 * sin, rest], axis=-1)
    return out.astype(t.dtype)


def dilated_window_attention(q, k, v, window, dilation):
    b, s, h, d = q.shape
    r = dilation
    reach = window // dilation
    n_sub = s // r
    blk = min(ATT_BLOCK, n_sub)
    n_blk = -(-n_sub // blk)
    pad = n_blk * blk - n_sub

    def to_sub(t):
        t = t.reshape(b, n_sub, r, h, d).transpose(0, 2, 1, 3, 4).reshape(b * r, n_sub, h, d)
        t = jnp.pad(t, ((0, 0), (0, pad), (0, 0), (0, 0)))
        return t.reshape(b * r, n_blk, blk, h, d)

    def with_prev(t):
        prev = jnp.pad(t, ((0, 0), (1, 0), (0, 0), (0, 0), (0, 0)))[:, :-1]
        return jnp.concatenate([prev, t], axis=2)

    qb = to_sub(q)
    kc, vc = with_prev(to_sub(k)), with_prev(to_sub(v))
    scores = jnp.einsum('nbqhd,nbkhd->nbhqk', qb, kc)
    qi = jnp.arange(blk)[:, None]
    ki = jnp.arange(2 * blk)[None, :]
    dist = qi + blk - ki
    band = (dist >= 0) & (dist <= reach)
    has_prev = (jnp.arange(n_blk)[:, None, None] > 0) | (ki >= blk)[None]
    valid = band[None] & has_prev
    scores = jnp.where(valid[None, :, None], scores, -jnp.inf)
    m = jnp.max(scores, axis=-1, keepdims=True)
    p = jnp.exp(scores - m)
    den = jnp.sum(p, axis=-1, keepdims=True)
    o = jnp.einsum('nbhqk,nbkhd->nbqhd', p / den, vc)
    lse = jnp.swapaxes((m + jnp.log(den))[..., 0], 2, 3)

    def from_sub(t):
        t = t.reshape((b, r, n_blk * blk) + t.shape[3:])[:, :, :n_sub]
        t = jnp.moveaxis(t, 1, 2)
        return t.reshape((b, s) + t.shape[3:])

    return from_sub(o), from_sub(lse)


def dilated_mixture_attention(q, k, v):
    outs, lses = [], []
    for window, dilation in DILATED_PATTERNS:
        o, l = dilated_window_attention(q, k, v, window, dilation)
        outs.append(o)
        lses.append(l)
    alpha = jax.nn.softmax(jnp.stack(lses), axis=0)
    return jnp.einsum('pbsh,pbshd->bshd', alpha, jnp.stack(outs))


def causal_depthwise_conv(u, w, bias):
    kw, ch = w.shape
    y = lax.conv_general_dilated(u, w[:, None, :].astype(u.dtype), window_strides=(1,),
                                 padding=[(kw - 1, 0)], dimension_numbers=('NWC', 'WIO', 'NWC'),
                                 feature_group_count=ch)
    return y + bias


def segsum(a):
    cs = jnp.cumsum(a, axis=-1)
    diff = cs[..., :, None] - cs[..., None, :]
    t = a.shape[-1]
    mask = jnp.tril(jnp.ones((t, t), dtype=bool))
    return jnp.where(mask, diff, -jnp.inf)


def ssd_chunked(xh, dt, a, bmat, cmat):
    b, s, h, p = xh.shape
    g, n = bmat.shape[-2:]
    e = h // g
    q = SSD_CHUNK
    c = s // q
    xdt = (xh * dt[..., None]).reshape(b, c, q, g, e, p)
    a_dt = (dt * a).reshape(b, c, q, h).transpose(0, 3, 1, 2)
    a_cs = jnp.cumsum(a_dt, axis=-1)
    bc = bmat.reshape(b, c, q, g, n)
    cc = cmat.reshape(b, c, q, g, n)
    decay_in = jnp.exp(segsum(a_dt)).reshape(b, g, e, c, q, q)
    cb = jnp.einsum('bclgn,bcsgn->bcgls', cc, bc)
    y_diag = jnp.einsum('bcgls,bgecls,bcsgep->bclgep', cb, decay_in, xdt)
    decay_to_end = jnp.exp(a_cs[..., -1:] - a_cs).reshape(b, g, e, c, q)
    states = jnp.einsum('bclgn,bgecl,bclgep->bcgepn', bc, decay_to_end, xdt)
    chunk_tot = jnp.pad(a_cs[..., -1], ((0, 0), (0, 0), (1, 0)))
    decay_chunk = jnp.exp(segsum(chunk_tot)).reshape(b, g, e, c + 1, c + 1)
    states = jnp.concatenate([jnp.zeros_like(states[:, :1]), states], axis=1)
    states = jnp.einsum('bgezk,bkgepn->bzgepn', decay_chunk, states)[:, :-1]
    decay_from_start = jnp.exp(a_cs).reshape(b, g, e, c, q)
    y_off = jnp.einsum('bclgn,bcgepn,bgecl->bclgep', cc, states, decay_from_start)
    return (y_diag + y_off).reshape(b, s, h, p)


def hybrid_mixer(h, positions, w_in, conv_w, conv_b, dt_bias, a_log, d_skip, ssd_norm_w,
                 w_att_out, w_ssd_out, w_mix_out):
    b, s, _ = h.shape
    offs = np.cumsum(IN_SIZES)[:-1].tolist()
    q, k, v, z, xbc, dt_raw, gates = jnp.split(h @ w_in, offs, axis=-1)

    q = partial_rope(q.reshape(b, s, ATT_HEADS, ATT_HEAD_DIM), positions).astype(jnp.float32)
    k = partial_rope(k.reshape(b, s, ATT_HEADS, ATT_HEAD_DIM), positions).astype(jnp.float32)
    v = v.reshape(b, s, ATT_HEADS, ATT_HEAD_DIM).astype(jnp.float32)
    o_att = dilated_mixture_attention(q * (ATT_HEAD_DIM ** -0.5), k, v)
    y_att = o_att.reshape(b, s, ATT_WIDTH).astype(h.dtype) @ w_att_out

    xbc = jax.nn.silu(causal_depthwise_conv(xbc, conv_w, conv_b)).astype(jnp.float32)
    xs, bm, cm = jnp.split(xbc, [SSD_INNER, SSD_INNER + SSD_GROUPS * SSD_STATE], axis=-1)
    xh = xs.reshape(b, s, SSD_HEADS, SSD_HEAD_DIM)
    dt = jax.nn.softplus(dt_raw.astype(jnp.float32) + dt_bias.astype(jnp.float32))
    a = -jnp.exp(a_log.astype(jnp.float32))
    y = ssd_chunked(xh, dt, a, bm.reshape(b, s, SSD_GROUPS, SSD_STATE),
                    cm.reshape(b, s, SSD_GROUPS, SSD_STATE))
    y = y + d_skip.astype(jnp.float32)[:, None] * xh
    y = y.reshape(b, s, SSD_INNER) * jax.nn.silu(z.astype(jnp.float32))
    y_ssd = rmsnorm(y, ssd_norm_w).astype(h.dtype) @ w_ssd_out

    g = jax.nn.sigmoid(gates.astype(jnp.float32)).reshape(b, s, N_BRANCH, D_MODEL)
    merged = g[:, :, 0] * y_att.astype(jnp.float32) + g[:, :, 1] * y_ssd.astype(jnp.float32)
    return merged.astype(h.dtype) @ w_mix_out


def setup_inputs(seed: int = 0) -> dict:
    key = jax.random.key(seed)
    ks = jax.random.split(key, 24)
    f32 = jnp.float32

    def nrm(k, shape, scale):
        return jax.random.normal(k, shape, f32) * scale

    def gain(k, shape):
        return 1.0 + 0.02 * jax.random.normal(k, shape, f32)

    L = DEPTH
    dt0 = jnp.exp(jax.random.uniform(ks[10], (L, SSD_HEADS), f32, np.log(1e-3), np.log(1e-1)))
    return {
        "x": nrm(ks[0], (BATCH, SEQ, D_MODEL), 1.0),
        "c": nrm(ks[1], (BATCH, D_MODEL), 1.0),
        "positions": jnp.broadcast_to(jnp.arange(SEQ, dtype=jnp.int32), (BATCH, SEQ)),
        "w_ada": nrm(ks[2], (L, D_MODEL, N_MOD * D_MODEL), 0.5 * D_MODEL ** -0.5),
        "b_ada": nrm(ks[3], (L, N_MOD * D_MODEL), 0.01),
        "norm1_w": gain(ks[4], (L, D_MODEL)),
        "ffn1_w_gu": nrm(ks[5], (L, D_MODEL, 2 * D_FF), D_MODEL ** -0.5),
        "ffn1_w_down": nrm(ks[6], (L, D_FF, D_MODEL), D_FF ** -0.5),
        "norm2_w": gain(ks[7], (L, D_MODEL)),
        "w_in": nrm(ks[8], (L, D_MODEL, IN_WIDTH), D_MODEL ** -0.5),
        "conv_w": nrm(ks[9], (L, SSD_CONV, SSD_CONV_CH), SSD_CONV ** -0.5),
        "conv_b": nrm(ks[11], (L, SSD_CONV_CH), 0.01),
        "dt_bias": dt0 + jnp.log(-jnp.expm1(-dt0)),
        "a_log": jnp.log(jax.random.uniform(ks[12], (L, SSD_HEADS), f32, 1.0, 16.0)),
        "d_skip": gain(ks[13], (L, SSD_HEADS)),
        "ssd_norm_w": gain(ks[14], (L, SSD_INNER)),
        "w_att_out": nrm(ks[15], (L, ATT_WIDTH, D_MODEL), ATT_WIDTH ** -0.5),
        "w_ssd_out": nrm(ks[16], (L, SSD_INNER, D_MODEL), SSD_INNER ** -0.5),
        "w_mix_out": nrm(ks[17], (L, D_MODEL, D_MODEL), D_MODEL ** -0.5),
        "norm3_w": gain(ks[18], (L, D_MODEL)),
        "ffn2_w_gu": nrm(ks[19], (L, D_MODEL, 2 * D_FF), D_MODEL ** -0.5),
        "ffn2_w_down": nrm(ks[20], (L, D_FF, D_MODEL), D_FF ** -0.5),
        "final_norm_w": gain(ks[21], (D_MODEL,)),
    }


def reference(x, c, positions, w_ada, b_ada, norm1_w, ffn1_w_gu, ffn1_w_down, norm2_w, w_in,
              conv_w, conv_b, dt_bias, a_log, d_skip, ssd_norm_w, w_att_out, w_ssd_out,
              w_mix_out, norm3_w, ffn2_w_gu, ffn2_w_down, final_norm_w):
    b = x.shape[0]
    c_act = jax.nn.silu(c)
    for l in range(DEPTH):
        mod = (c_act @ w_ada[l] + b_ada[l]).reshape(b, N_MOD, D_MODEL)
        sh1, sc1, g1, sh2, sc2, g2, sh3, sc3, g3 = [mod[:, i] for i in range(N_MOD)]
        h = modulate(rmsnorm(x, norm1_w[l]), sh1, sc1)
        x = x + 0.5 * g1[:, None, :] * swiglu(h, ffn1_w_gu[l], ffn1_w_down[l])
        h = modulate(rmsnorm(x, norm2_w[l]), sh2, sc2)
        y = hybrid_mixer(h, positions, w_in[l], conv_w[l], conv_b[l], dt_bias[l], a_log[l],
                         d_skip[l], ssd_norm_w[l], w_att_out[l], w_ssd_out[l], w_mix_out[l])
        x = x + g2[:, None, :] * y
        h = modulate(rmsnorm(x, norm3_w[l]), sh3, sc3)
        x = x + 0.5 * g3[:, None, :] * swiglu(h, ffn2_w_gu[l], ffn2_w_down[l])
    return rmsnorm(x, final_norm_w)
```

```python
import functools

import jax
import jax.numpy as jnp
import numpy as np
from jax import lax
from jax.experimental import pallas as pl
from jax.experimental.pallas import tpu as pltpu

F32 = jnp.float32
BF16 = jnp.bfloat16

D_MODEL = 1024
BATCH = 2
SEQ = 8192
TOKENS = BATCH * SEQ
ATT_HEADS = 12
ATT_HEAD_DIM = 64
ATT_WIDTH = ATT_HEADS * ATT_HEAD_DIM
ROPE_DIM = ATT_HEAD_DIM // 4
ROPE_THETA = 500000.0
DILATIONS = (1, 4, 16)
ATT_BLOCK = 128
SSD_INNER = 2 * D_MODEL
SSD_HEAD_DIM = 64
SSD_HEADS = SSD_INNER // SSD_HEAD_DIM
SSD_GROUPS = 8
SSD_STATE = 128
SSD_CONV = 4
SSD_CHUNK = 128
SSD_CONV_CH = SSD_INNER + 2 * SSD_GROUPS * SSD_STATE
D_FF = 2816
N_MOD = 9
NORM_EPS = 1e-6

LANES = 128
SUBLANES = 8
VMEM_LIMIT = 56 * 1024 * 1024
TOKEN_TILE = 256
ADA_COL_TILE = 1536
NEG_BIG = -1e30

_NT = (((1,), (1,)), ((), ()))


def _params(*sem):
    return pltpu.CompilerParams(dimension_semantics=sem, vmem_limit_bytes=VMEM_LIMIT)


def _resident(shape):
    zeros = (0,) * len(shape)
    return pl.BlockSpec(shape, lambda *_: zeros, pipeline_mode=pl.Buffered(1))


def _silu(x):
    return x * jax.nn.sigmoid(x)


def _rms(x, w):
    ms = jnp.sum(x * x, axis=-1, keepdims=True) * (1.0 / x.shape[-1])
    return x * lax.rsqrt(ms + NORM_EPS) * w


def _rms_mod(x, w, shift, scale):
    return _rms(x, w) * (1.0 + scale) + shift


def _swiglu(h, wgu_ref, wd_ref):
    gu = jnp.dot(h, wgu_ref[...], preferred_element_type=F32)
    a = (_silu(gu[:, :D_FF]) * gu[:, D_FF:]).astype(BF16)
    return jnp.dot(a, wd_ref[...], preferred_element_type=F32)


def _ada_kernel(ct_ref, w_ref, b_ref, o_ref):
    act = _silu(ct_ref[...])
    w = w_ref[...]
    rows = [jnp.sum(w * act[:, b:b + 1], axis=0, keepdims=True) for b in range(BATCH)]
    o_ref[...] = jnp.concatenate(rows, axis=0) + b_ref[...]


def _ada(c, w_ada, b_ada):
    n = N_MOD * D_MODEL
    return pl.pallas_call(
        _ada_kernel,
        grid=(n // ADA_COL_TILE,),
        in_specs=[
            pl.BlockSpec((D_MODEL, BATCH), lambda j: (0, 0)),
            pl.BlockSpec((D_MODEL, ADA_COL_TILE), lambda j: (0, j)),
            pl.BlockSpec((1, ADA_COL_TILE), lambda j: (0, j)),
        ],
        out_specs=pl.BlockSpec((BATCH, ADA_COL_TILE), lambda j: (0, j)),
        out_shape=jax.ShapeDtypeStruct((BATCH, n), F32),
        compiler_params=_params("arbitrary"),
        name="ada_mod",
    )(c.T, w_ada, b_ada.reshape(1, n))


def _ffn1_kernel(x_ref, mod_ref, nw_ref, wgu_ref, wd_ref, o_ref):
    x = x_ref[...]
    mod = mod_ref[0]
    h = _rms_mod(x, nw_ref[...], mod[0:1], mod[1:2]).astype(BF16)
    o_ref[...] = x + (0.5 * mod[2:3]) * _swiglu(h, wgu_ref, wd_ref)


def _tile_spec(width, tm=TOKEN_TILE):
    return pl.BlockSpec((tm, width), lambda i: (i, 0))


def _mod_spec(tm=TOKEN_TILE):
    per_batch = SEQ // tm
    return pl.BlockSpec((1, N_MOD, D_MODEL), lambda i: (i // per_batch, 0, 0))


def _ffn1(x, mod, nw, wgu, wd):
    return pl.pallas_call(
        _ffn1_kernel,
        grid=(TOKENS // TOKEN_TILE,),
        in_specs=[_tile_spec(D_MODEL), _mod_spec(), _resident((1, D_MODEL)),
                  _resident(wgu.shape), _resident(wd.shape)],
        out_specs=_tile_spec(D_MODEL),
        out_shape=jax.ShapeDtypeStruct((TOKENS, D_MODEL), F32),
        compiler_params=_params("arbitrary"),
        name="ffn1",
    )(x, mod, nw, wgu, wd)


def _softplus(x):
    return jnp.maximum(x, 0.0) + jnp.log1p(jnp.exp(-jnp.abs(x)))


def _inproj_kernel(x_ref, mod_ref, nw_ref, pos_ref, invf_ref, sgn_ref,
                   wqkv_ref, wz_ref, wxbc_ref, wdt_ref, wdtT_ref, wg_ref,
                   convw_ref, convb_ref, dtb_row_ref, dtb_col_ref,
                   q_ref, k_ref, v_ref, zs_ref, xs_ref, bm_ref, cm_ref,
                   dt_ref, dtT_ref, gates_ref, tail_ref):
    tm = x_ref.shape[0]
    mod = mod_ref[0]
    h = _rms_mod(x_ref[...], nw_ref[...], mod[3:4], mod[4:5]).astype(BF16)

    qkv = jnp.dot(h, wqkv_ref[...], preferred_element_type=F32)
    ang = pos_ref[...] * invf_ref[...]
    reps = ATT_WIDTH // LANES
    cos = jnp.concatenate([jnp.cos(ang)] * reps, axis=1)
    sin = jnp.concatenate([jnp.sin(ang) * sgn_ref[...]] * reps, axis=1)
    lane = lax.broadcasted_iota(jnp.int32, (tm, ATT_WIDTH), 1)
    first_half = (lane & (ATT_HEAD_DIM - 1)) < (ROPE_DIM // 2)

    def rope(t):
        partner = jnp.where(first_half,
                            pltpu.roll(t, ATT_WIDTH - ROPE_DIM // 2, 1),
                            pltpu.roll(t, ROPE_DIM // 2, 1))
        return t * cos + partner * sin

    q_ref[...] = (rope(qkv[:, :ATT_WIDTH]) * (ATT_HEAD_DIM ** -0.5)).astype(BF16)
    k_ref[...] = rope(qkv[:, ATT_WIDTH:2 * ATT_WIDTH]).astype(BF16)
    v_ref[...] = qkv[:, 2 * ATT_WIDTH:].astype(BF16)

    zs_ref[...] = _silu(jnp.dot(h, wz_ref[...], preferred_element_type=F32)).astype(BF16)
    gates_ref[...] = jax.nn.sigmoid(
        jnp.dot(h, wg_ref[...], preferred_element_type=F32)).astype(BF16)

    xbc = jnp.dot(h, wxbc_ref[...], preferred_element_type=F32)

    @pl.when(pl.program_id(0) % (SEQ // tm) == 0)
    def _():
        tail_ref[...] = jnp.zeros_like(tail_ref)

    tail = tail_ref[...]
    top = xbc[:SUBLANES]
    cw = convw_ref[...]
    acc = xbc * cw[SSD_CONV - 1:SSD_CONV] + convb_ref[...]
    acc_top = top * cw[SSD_CONV - 1:SSD_CONV] + convb_ref[...]
    row = lax.broadcasted_iota(jnp.int32, top.shape, 0)
    for s in range(1, SSD_CONV):
        tap = cw[SSD_CONV - 1 - s:SSD_CONV - s]
        acc = acc + pltpu.roll(xbc, s, 0) * tap
        top_prev = jnp.where(row < s, pltpu.roll(tail, s, 0), pltpu.roll(top, s, 0))
        acc_top = acc_top + top_prev * tap
    tail_ref[...] = xbc[tm - SUBLANES:]
    y = _silu(jnp.concatenate([acc_top, acc[SUBLANES:]], axis=0))
    xs_ref[...] = y[:, :SSD_INNER].astype(BF16)
    bm_ref[...] = y[:, SSD_INNER:SSD_INNER + SSD_GROUPS * SSD_STATE].astype(BF16)
    cm_ref[...] = y[:, SSD_INNER + SSD_GROUPS * SSD_STATE:].astype(BF16)

    dt_raw = jnp.dot(h, wdt_ref[...], preferred_element_type=F32)[:, :SSD_HEADS]
    dt_ref[...] = _softplus(dt_raw + dtb_row_ref[...])
    dtT_raw = lax.dot_general(wdtT_ref[...], h, _NT, preferred_element_type=F32)
    dtT_ref[...] = _softplus(dtT_raw + dtb_col_ref[...])


def _inproj(x1, mod, nw, posf, invf, sgn, wqkv, wz, wxbc, wdt, wdtT, wg,
            conv_w, conv_b, dtb_row, dtb_col):
    tm = TOKEN_TILE
    bc = SSD_GROUPS * SSD_STATE
    outs = [
        (ATT_WIDTH, BF16), (ATT_WIDTH, BF16), (ATT_WIDTH, BF16),
        (SSD_INNER, BF16), (SSD_INNER, BF16), (bc, BF16), (bc, BF16),
        (SSD_HEADS, F32),
    ]
    out_shape = [jax.ShapeDtypeStruct((TOKENS, w), d) for w, d in outs]
    out_specs = [_tile_spec(w) for w, _ in outs]
    out_shape.append(jax.ShapeDtypeStruct((SSD_HEADS, TOKENS), F32))
    out_specs.append(pl.BlockSpec((SSD_HEADS, tm), lambda i: (0, i)))
    out_shape.append(jax.ShapeDtypeStruct((TOKENS, 2 * D_MODEL), BF16))
    out_specs.append(_tile_spec(2 * D_MODEL))
    return pl.pallas_call(
        _inproj_kernel,
        grid=(TOKENS // tm,),
        in_specs=[_tile_spec(D_MODEL), _mod_spec(), _resident((1, D_MODEL)),
                  _tile_spec(LANES), _resident((1, LANES)), _resident((1, LANES)),
                  _resident(wqkv.shape), _resident(wz.shape), _resident(wxbc.shape),
                  _resident(wdt.shape), _resident(wdtT.shape), _resident(wg.shape),
                  _resident(conv_w.shape), _resident(conv_b.shape),
                  _resident(dtb_row.shape), _resident(dtb_col.shape)],
        out_specs=out_specs,
        out_shape=out_shape,
        scratch_shapes=[pltpu.VMEM((SUBLANES, SSD_CONV_CH), F32)],
        compiler_params=_params("arbitrary"),
        name="in_proj",
    )(x1, mod, nw, posf, invf, sgn, wqkv, wz, wxbc, wdt, wdtT, wg,
      conv_w, conv_b, dtb_row, dtb_col)


def _attn_kernel(q_ref, kp_ref, kc_ref, vp_ref, vc_ref, o_ref, lse_ref):
    blk = ATT_BLOCK
    has_prev = pl.program_id(2) > 0
    q = q_ref[0]
    k = jnp.concatenate([kp_ref[0], kc_ref[0]], axis=0)
    v = jnp.concatenate([vp_ref[0], vc_ref[0]], axis=0)
    qi = lax.broadcasted_iota(jnp.int32, (blk, 2 * blk), 0)
    ki = lax.broadcasted_iota(jnp.int32, (blk, 2 * blk), 1)
    dist = qi + blk - ki
    in_band = jnp.where(dist >= 0, jnp.where(dist <= blk, 1, 0), 0)
    in_seq = jnp.where(ki >= blk, 1, jnp.where(has_prev, 1, 0))
    bias = jnp.where(in_band * in_seq > 0, 0.0, NEG_BIG)
    lane_row = lax.broadcasted_iota(jnp.int32, (1, LANES), 1)
    lane = lax.broadcasted_iota(jnp.int32, (blk, LANES), 1)
    low = lane < ATT_HEAD_DIM
    head_mask = [(lane_row < ATT_HEAD_DIM).astype(BF16), (lane_row >= ATT_HEAD_DIM).astype(BF16)]
    lse_tile = jnp.zeros((blk, LANES), F32)
    for pair in range(ATT_HEADS // 2):
        cols = slice(pair * LANES, (pair + 1) * LANES)
        qp, kpair, vpair = q[:, cols], k[:, cols], v[:, cols]
        outs = []
        for half in range(2):
            s = lax.dot_general(qp * head_mask[half], kpair, _NT,
                                preferred_element_type=F32) + bias
            m = jnp.max(s, axis=1, keepdims=True)
            p = jnp.exp(s - m)
            den = jnp.sum(p, axis=1, keepdims=True)
            o = jnp.dot(p.astype(BF16), vpair, preferred_element_type=F32)
            outs.append(o * (1.0 / den))
            lse_tile = jnp.where(lane == 2 * pair + half, m + jnp.log(den), lse_tile)
        o_ref[0, :, cols] = jnp.where(low, outs[0], outs[1]).astype(BF16)
    lse_ref[0] = lse_tile


def _attention_pattern(q, k, v, r):
    n_sub = SEQ // r
    n_blk = n_sub // ATT_BLOCK
    view = lambda t: t.reshape(BATCH, n_sub, r * ATT_WIDTH)
    cur = pl.BlockSpec((1, ATT_BLOCK, ATT_WIDTH), lambda b, j, i: (b, i, j))
    prev = pl.BlockSpec((1, ATT_BLOCK, ATT_WIDTH), lambda b, j, i: (b, jnp.maximum(i - 1, 0), j))
    o, lse = pl.pallas_call(
        _attn_kernel,
        grid=(BATCH, r, n_blk),
        in_specs=[cur, prev, cur, prev, cur],
        out_specs=[cur, pl.BlockSpec((1, ATT_BLOCK, LANES), lambda b, j, i: (b, i, j))],
        out_shape=[jax.ShapeDtypeStruct((BATCH, n_sub, r * ATT_WIDTH), BF16),
                   jax.ShapeDtypeStruct((BATCH, n_sub, r * LANES), F32)],
        compiler_params=_params("arbitrary", "arbitrary", "arbitrary"),
        name=f"attn_dil{r}",
    )(view(q), view(k), view(k), view(v), view(v))
    return o.reshape(TOKENS, ATT_WIDTH), lse.reshape(TOKENS, LANES)


def _ssd_kernel(xs_ref, bm_ref, cm_ref, dt_ref, dtT_ref, zs_ref,
                alog_row_ref, alog_col_ref, dskip_ref, nw_ref,
                o_ref, state_ref, y_ref):
    q = SSD_CHUNK

    @pl.when(pl.program_id(1) == 0)
    def _():
        state_ref[...] = jnp.zeros_like(state_ref)

    dtT = dtT_ref[...]
    a_dt_col = dt_ref[...] * (-jnp.exp(alog_row_ref[...]))
    a_dt_row = dtT * (-jnp.exp(alog_col_ref[...]))
    r = lax.broadcasted_iota(jnp.int32, (q, q), 0)
    c = lax.broadcasted_iota(jnp.int32, (q, q), 1)
    causal = r >= c
    hi = lax.Precision.HIGHEST
    cs_col = jnp.dot(causal.astype(F32), a_dt_col, precision=hi, preferred_element_type=F32)
    cs_row = jnp.dot(a_dt_row, (r <= c).astype(F32), precision=hi, preferred_element_type=F32)
    total = cs_row[:, q - 1:q]
    to_end = dtT * jnp.exp(total - cs_row)
    chunk_decay = jnp.exp(total)
    eye = (r == c).astype(BF16)
    r2 = lax.broadcasted_iota(jnp.int32, (q, 2 * q), 0)
    c2 = lax.broadcasted_iota(jnp.int32, (q, 2 * q), 1)
    keep = (r2 >= c2) | (c2 >= q)
    top_rows = r < SSD_HEAD_DIM
    low = c < SSD_HEAD_DIM
    zeros_row = jnp.zeros((1, q), F32)
    ones_row = jnp.ones((1, q), F32)

    for g in range(SSD_GROUPS):
        gcols = slice(g * SSD_STATE, (g + 1) * SSD_STATE)
        b_g = bm_ref[:, gcols]
        c_g = cm_ref[:, gcols]
        cb = lax.dot_general(c_g, b_g, _NT, preferred_element_type=F32)
        cbc = jnp.concatenate([cb, c_g.astype(F32)], axis=1)
        for pr in range(2):
            pair = 2 * g + pr
            h0 = 2 * pair
            pcols = slice(pair * LANES, (pair + 1) * LANES)
            x_pair = xs_ref[:, pcols]
            state = state_ref[pair]
            x_t = lax.dot_general(eye, x_pair, _NT, preferred_element_type=F32)
            rhs_t = jnp.concatenate([x_t.astype(BF16), state.astype(BF16)], axis=1)
            ys = []
            for h in (h0, h0 + 1):
                diff = cs_col[:, h:h + 1] - jnp.concatenate([cs_row[h:h + 1], zeros_row], axis=1)
                scale = jnp.concatenate([dtT[h:h + 1], ones_row], axis=1)
                w = jnp.where(keep, cbc * jnp.exp(diff) * scale, 0.0).astype(BF16)
                ys.append(lax.dot_general(w, rhs_t, _NT, preferred_element_type=F32))
            y_ref[:, pcols] = jnp.where(low, ys[0], ys[1])
            w_end = jnp.where(top_rows, to_end[h0:h0 + 1], to_end[h0 + 1:h0 + 2])
            upd = jnp.dot((x_t * w_end).astype(BF16), b_g, preferred_element_type=F32)
            decay = jnp.where(top_rows, chunk_decay[h0:h0 + 1], chunk_decay[h0 + 1:h0 + 2])
            state_ref[pair] = state * decay + upd

    y = (y_ref[...] + dskip_ref[...] * xs_ref[...].astype(F32)) * zs_ref[...].astype(F32)
    o_ref[...] = _rms(y, nw_ref[...]).astype(BF16)


def _ssd(xs, bm, cm, dt, dtT, zs, alog_row, alog_col, dskip, nw):
    n_chunks = SEQ // SSD_CHUNK
    tok = lambda w: pl.BlockSpec((SSD_CHUNK, w), lambda b, c: (b * n_chunks + c, 0))
    bc = SSD_GROUPS * SSD_STATE
    return pl.pallas_call(
        _ssd_kernel,
        grid=(BATCH, n_chunks),
        in_specs=[tok(SSD_INNER), tok(bc), tok(bc), tok(SSD_HEADS),
                  pl.BlockSpec((SSD_HEADS, SSD_CHUNK), lambda b, c: (0, b * n_chunks + c)),
                  tok(SSD_INNER),
                  _resident((1, SSD_HEADS)), _resident((SSD_HEADS, 1)),
                  _resident((1, SSD_INNER)), _resident((1, SSD_INNER))],
        out_specs=tok(SSD_INNER),
        out_shape=jax.ShapeDtypeStruct((TOKENS, SSD_INNER), BF16),
        scratch_shapes=[pltpu.VMEM((SSD_HEADS // 2, LANES, SSD_STATE), F32),
                        pltpu.VMEM((SSD_CHUNK, SSD_INNER), F32)],
        compiler_params=_params("arbitrary", "arbitrary"),
        name="ssd_scan",
    )(xs, bm, cm, dt, dtT, zs, alog_row, alog_col, dskip, nw)


def _out_kernel(x_ref, mod_ref, o1_ref, o2_ref, o3_ref, l1_ref, l2_ref, l3_ref,
                yn_ref, gates_ref, expand_ref, watt_ref, wssd_ref, wmix_ref,
                n3w_ref, wgu_ref, wd_ref, fnw_ref, out_ref):
    mod = mod_ref[0]
    lses = [l1_ref[...], l2_ref[...], l3_ref[...]]
    m = jnp.maximum(jnp.maximum(lses[0], lses[1]), lses[2])
    es = [jnp.exp(l - m) for l in lses]
    inv = 1.0 / (es[0] + es[1] + es[2])
    o_att = None
    for e, o_ref in zip(es, (o1_ref, o2_ref, o3_ref)):
        alpha = jnp.dot((e * inv).astype(BF16), expand_ref[...], preferred_element_type=F32)
        term = alpha * o_ref[...].astype(F32)
        o_att = term if o_att is None else o_att + term
    y_att = jnp.dot(o_att.astype(BF16), watt_ref[...], preferred_element_type=F32)
    y_ssd = jnp.dot(yn_ref[...], wssd_ref[...], preferred_element_type=F32)
    gates = gates_ref[...].astype(F32)
    merged = gates[:, :D_MODEL] * y_att + gates[:, D_MODEL:] * y_ssd
    y = jnp.dot(merged.astype(BF16), wmix_ref[...], preferred_element_type=F32)
    x2 = x_ref[...] + mod[5:6] * y
    h = _rms_mod(x2, n3w_ref[...], mod[6:7], mod[7:8]).astype(BF16)
    x3 = x2 + (0.5 * mod[8:9]) * _swiglu(h, wgu_ref, wd_ref)
    out_ref[...] = _rms(x3, fnw_ref[...])


def _out(x1, mod, o_list, lse_list, yn, gates, expand, watt, wssd, wmix, n3w, wgu, wd, fnw):
    return pl.pallas_call(
        _out_kernel,
        grid=(TOKENS // TOKEN_TILE,),
        in_specs=[_tile_spec(D_MODEL), _mod_spec()]
                 + [_tile_spec(ATT_WIDTH)] * 3 + [_tile_spec(LANES)] * 3
                 + [_tile_spec(SSD_INNER), _tile_spec(2 * D_MODEL)]
                 + [_resident(a.shape) for a in (expand, watt, wssd, wmix, n3w, wgu, wd, fnw)],
        out_specs=_tile_spec(D_MODEL),
        out_shape=jax.ShapeDtypeStruct((TOKENS, D_MODEL), F32),
        compiler_params=_params("arbitrary"),
        name="mix_out_ffn2",
    )(x1, mod, *o_list, *lse_list, yn, gates, expand, watt, wssd, wmix, n3w, wgu, wd, fnw)


def _rope_tables():
    inv_freq = ROPE_THETA ** (-jnp.arange(0, ROPE_DIM, 2, dtype=F32) / ROPE_DIM)
    d = np.arange(LANES) % ATT_HEAD_DIM
    half = ROPE_DIM // 2
    invf = jnp.where(d < ROPE_DIM, inv_freq[d % half], 0.0).astype(F32).reshape(1, LANES)
    sgn = np.where(d < half, -1.0, np.where(d < ROPE_DIM, 1.0, 0.0)).astype(np.float32)
    return invf, jnp.asarray(sgn).reshape(1, LANES)


def kernel(x, c, positions, w_ada, b_ada, norm1_w, ffn1_w_gu, ffn1_w_down, norm2_w, w_in,
           conv_w, conv_b, dt_bias, a_log, d_skip, ssd_norm_w, w_att_out, w_ssd_out,
           w_mix_out, norm3_w, ffn2_w_gu, ffn2_w_down, final_norm_w):
    l = 0
    row = lambda t: t.reshape(1, -1).astype(F32)
    xf = x.reshape(TOKENS, D_MODEL)
    mod = _ada(c, w_ada[l], b_ada[l]).reshape(BATCH, N_MOD, D_MODEL)

    x1 = _ffn1(xf, mod, row(norm1_w[l]), ffn1_w_gu[l].astype(BF16), ffn1_w_down[l].astype(BF16))

    o_z = 3 * ATT_WIDTH
    o_xbc = o_z + SSD_INNER
    o_dt = o_xbc + SSD_CONV_CH
    o_g = o_dt + SSD_HEADS
    w = w_in[l]
    w_dt = w[:, o_dt:o_g]
    invf, sgn = _rope_tables()
    posf = jnp.broadcast_to(positions.reshape(TOKENS, 1).astype(F32), (TOKENS, LANES))
    q, k, v, zs, xs, bm, cm, dt, dtT, gates = _inproj(
        x1, mod, row(norm2_w[l]), posf, invf, sgn,
        w[:, :o_z].astype(BF16), w[:, o_z:o_xbc].astype(BF16), w[:, o_xbc:o_dt].astype(BF16),
        jnp.pad(w_dt, ((0, 0), (0, LANES - SSD_HEADS))).astype(BF16), w_dt.T.astype(BF16),
        w[:, o_g:].astype(BF16),
        conv_w[l], row(conv_b[l]), row(dt_bias[l]), dt_bias[l].reshape(SSD_HEADS, 1))

    o_list, lse_list = zip(*[_attention_pattern(q, k, v, r) for r in DILATIONS])

    yn = _ssd(xs, bm, cm, dt, dtT, zs, row(a_log[l]), a_log[l].reshape(SSD_HEADS, 1),
              row(jnp.repeat(d_skip[l], SSD_HEAD_DIM)), row(ssd_norm_w[l]))

    head_of_col = np.arange(ATT_WIDTH) // ATT_HEAD_DIM
    expand = jnp.asarray(np.arange(LANES)[:, None] == head_of_col[None, :], BF16)
    out = _out(x1, mod, o_list, lse_list, yn, gates, expand,
               w_att_out[l].astype(BF16), w_ssd_out[l].astype(BF16), w_mix_out[l].astype(BF16),
               row(norm3_w[l]), ffn2_w_gu[l].astype(BF16), ffn2_w_down[l].astype(BF16),
               row(final_norm_w))
    return out.reshape(BATCH, SEQ, D_MODEL)
```

```python
import jax
import jax.numpy as jnp
import numpy as np
from jax import lax
from jax.experimental import pallas as pl
from jax.experimental.pallas import tpu as pltpu

F32 = jnp.float32
BF16 = jnp.bfloat16

D_MODEL = 1024
BATCH = 2
SEQ = 8192
TOKENS = BATCH * SEQ
ATT_HEADS = 12
ATT_HEAD_DIM = 64
ATT_WIDTH = ATT_HEADS * ATT_HEAD_DIM
ROPE_DIM = ATT_HEAD_DIM // 4
ROPE_THETA = 500000.0
ATT_BLOCK = 128
SSD_INNER = 2 * D_MODEL
SSD_HEAD_DIM = 64
SSD_HEADS = SSD_INNER // SSD_HEAD_DIM
SSD_GROUPS = 8
SSD_STATE = 128
SSD_CONV = 4
SSD_CHUNK = 128
SSD_CONV_CH = SSD_INNER + 2 * SSD_GROUPS * SSD_STATE
D_FF = 2816
N_MOD = 9
NORM_EPS = 1e-6

LANES = 128
SUBLANES = 8
VMEM_LIMIT = 56 * 1024 * 1024
TOKEN_TILE = 256
COL_CHUNK = 512
REGROUP = 16
ADA_COL_TILE = 1536
NEG_BIG = -1e30

_NT = (((1,), (1,)), ((), ()))


def _params(*sem):
    return pltpu.CompilerParams(dimension_semantics=sem, vmem_limit_bytes=VMEM_LIMIT)


def _resident(shape):
    zeros = (0,) * len(shape)
    return pl.BlockSpec(shape, lambda *_: zeros, pipeline_mode=pl.Buffered(1))


def _sigmoid(x):
    return 0.5 * jnp.tanh(0.5 * x) + 0.5


def _silu(x):
    return x * _sigmoid(x)


def _rms(x, w):
    ms = jnp.sum(x * x, axis=-1, keepdims=True) * (1.0 / x.shape[-1])
    return x * lax.rsqrt(ms + NORM_EPS) * w


def _rms_mod(x, w, shift, scale):
    return _rms(x, w) * (1.0 + scale) + shift


def _swiglu(h, wgu_ref, wd_ref):
    gu = jnp.dot(h, wgu_ref[...], preferred_element_type=F32)
    a = (_silu(gu[:, :D_FF]) * gu[:, D_FF:]).astype(BF16)
    return jnp.dot(a, wd_ref[...], preferred_element_type=F32)


def _ada_kernel(ct_ref, w_ref, b_ref, o_ref):
    act = _silu(ct_ref[...])
    w = w_ref[...]
    rows = [jnp.sum(w * act[:, b:b + 1], axis=0, keepdims=True) for b in range(BATCH)]
    o_ref[...] = jnp.concatenate(rows, axis=0) + b_ref[...]


def _ada(c, w_ada, b_ada):
    n = N_MOD * D_MODEL
    return pl.pallas_call(
        _ada_kernel,
        grid=(n // ADA_COL_TILE,),
        in_specs=[
            pl.BlockSpec((D_MODEL, BATCH), lambda j: (0, 0)),
            pl.BlockSpec((D_MODEL, ADA_COL_TILE), lambda j: (0, j)),
            pl.BlockSpec((1, ADA_COL_TILE), lambda j: (0, j)),
        ],
        out_specs=pl.BlockSpec((BATCH, ADA_COL_TILE), lambda j: (0, j)),
        out_shape=jax.ShapeDtypeStruct((BATCH, n), F32),
        compiler_params=_params("arbitrary"),
        name="ada_mod",
    )(c.T, w_ada, b_ada.reshape(1, n))


def _ffn1_kernel(x_ref, mod_ref, nw_ref, wgu_ref, wd_ref, o_ref):
    x = x_ref[...]
    mod = mod_ref[0]
    h = _rms_mod(x, nw_ref[...], mod[0:1], mod[1:2]).astype(BF16)
    o_ref[...] = x + (0.5 * mod[2:3]) * _swiglu(h, wgu_ref, wd_ref)


def _tile_spec(width, tm=TOKEN_TILE):
    return pl.BlockSpec((tm, width), lambda i: (i, 0))


def _mod_spec(tm=TOKEN_TILE):
    per_batch = SEQ // tm
    return pl.BlockSpec((1, N_MOD, D_MODEL), lambda i: (i // per_batch, 0, 0))


def _ffn1(x, mod, nw, wgu, wd):
    return pl.pallas_call(
        _ffn1_kernel,
        grid=(TOKENS // TOKEN_TILE,),
        in_specs=[_tile_spec(D_MODEL), _mod_spec(), _resident((1, D_MODEL)),
                  _resident(wgu.shape), _resident(wd.shape)],
        out_specs=_tile_spec(D_MODEL),
        out_shape=jax.ShapeDtypeStruct((TOKENS, D_MODEL), F32),
        compiler_params=_params("arbitrary"),
        name="ffn1",
    )(x, mod, nw, wgu, wd)


def _softplus(x):
    return jnp.maximum(x, 0.0) + jnp.log1p(jnp.exp(-jnp.abs(x)))


def _col_loop(n_cols, body):
    for c in range(n_cols // COL_CHUNK):
        body(pl.ds(c * COL_CHUNK, COL_CHUNK))


def _inproj_kernel(x_ref, mod_ref, nw_ref, pos_ref, invf_ref, sgn_ref, perm_ref,
                   wqkv_ref, wz_ref, wxbc_ref, wdt_ref, wdtT_ref, wg_ref,
                   convw_ref, convb_ref, dtb_row_ref, dtb_col_ref,
                   q_ref, k_ref, v_ref, qg_ref, kg_ref, vg_ref,
                   zs_ref, xs_ref, bm_ref, cm_ref, dt_ref, dtT_ref, gates_ref,
                   h_ref, hist_ref):
    tm = x_ref.shape[0]
    mod = mod_ref[0]
    h_ref[...] = _rms_mod(x_ref[...], nw_ref[...], mod[3:4], mod[4:5]).astype(BF16)

    ang = pos_ref[...] * invf_ref[...]
    reps = ATT_WIDTH // LANES
    cos = jnp.concatenate([jnp.cos(ang)] * reps, axis=1)
    sin = jnp.concatenate([jnp.sin(ang) * sgn_ref[...]] * reps, axis=1)
    lane = lax.broadcasted_iota(jnp.int32, (tm, ATT_WIDTH), 1)
    first_half = (lane & (ATT_HEAD_DIM - 1)) < (ROPE_DIM // 2)

    def rope(t):
        partner = jnp.where(first_half,
                            pltpu.roll(t, ATT_WIDTH - ROPE_DIM // 2, 1),
                            pltpu.roll(t, ROPE_DIM // 2, 1))
        return t * cos + partner * sin

    def qkv_part(idx, tok_ref, grouped_ref):
        t = jnp.dot(h_ref[...], wqkv_ref[:, idx * ATT_WIDTH:(idx + 1) * ATT_WIDTH],
                    preferred_element_type=F32)
        if idx == 0:
            t = rope(t) * (ATT_HEAD_DIM ** -0.5)
        elif idx == 1:
            t = rope(t)
        t = t.astype(BF16)
        tok_ref[...] = t
        regrouped = jnp.dot(perm_ref[...], t, preferred_element_type=F32).astype(BF16)
        grouped_ref[...] = regrouped.reshape(grouped_ref.shape)

    def z_chunk(cols):
        z = jnp.dot(h_ref[...], wz_ref[:, cols], preferred_element_type=F32)
        zs_ref[:, cols] = _silu(z).astype(BF16)

    def gate_chunk(cols):
        g = jnp.dot(h_ref[...], wg_ref[:, cols], preferred_element_type=F32)
        gates_ref[:, cols] = _sigmoid(g).astype(BF16)

    @pl.when(pl.program_id(0) % (SEQ // tm) == 0)
    def _():
        hist_ref[:SUBLANES, :] = jnp.zeros((SUBLANES, SSD_CONV_CH), F32)

    def conv_chunk(out_ref, base, cols):
        src = pl.ds(cols.start + base, COL_CHUNK)
        raw = jnp.dot(h_ref[...], wxbc_ref[:, src], preferred_element_type=F32)
        hist_ref[SUBLANES:, src] = raw
        acc = raw * convw_ref[SSD_CONV - 1:SSD_CONV, src] + convb_ref[:, src]
        for s in range(1, SSD_CONV):
            acc = acc + (hist_ref[pl.ds(SUBLANES - s, tm), src]
                         * convw_ref[SSD_CONV - 1 - s:SSD_CONV - s, src])
        hist_ref[:SUBLANES, src] = raw[tm - SUBLANES:]
        out_ref[:, cols] = _silu(acc).astype(BF16)

    bc = SSD_GROUPS * SSD_STATE
    chunks = lambda width: [pl.ds(c * COL_CHUNK, COL_CHUNK) for c in range(width // COL_CHUNK)]
    heavy = ([(conv_chunk, (xs_ref, 0, c)) for c in chunks(SSD_INNER)]
             + [(conv_chunk, (bm_ref, SSD_INNER, c)) for c in chunks(bc)]
             + [(conv_chunk, (cm_ref, SSD_INNER + bc, c)) for c in chunks(bc)])
    light = ([(z_chunk, (c,)) for c in chunks(SSD_INNER)]
             + [(gate_chunk, (c,)) for c in chunks(2 * D_MODEL)]
             + [(qkv_part, (0, q_ref, qg_ref)), (qkv_part, (1, k_ref, kg_ref)),
                (qkv_part, (2, v_ref, vg_ref))])
    for i in range(max(len(heavy), len(light))):
        for work in (heavy, light):
            if i < len(work):
                fn, args = work[i]
                fn(*args)

    h = h_ref[...]
    dt_raw = jnp.dot(h, wdt_ref[...], preferred_element_type=F32)[:, :SSD_HEADS]
    dt_ref[...] = _softplus(dt_raw + dtb_row_ref[...])
    dtT_raw = lax.dot_general(wdtT_ref[...], h, _NT, preferred_element_type=F32)
    dtT_ref[...] = _softplus(dtT_raw + dtb_col_ref[...])


def _inproj(x1, mod, nw, posf, invf, sgn, perm, wqkv, wz, wxbc, wdt, wdtT, wg,
            conv_w, conv_b, dtb_row, dtb_col):
    tm = TOKEN_TILE
    bc = SSD_GROUPS * SSD_STATE
    tiles = SEQ // tm
    tok = lambda w, d: (jax.ShapeDtypeStruct((TOKENS, w), d), _tile_spec(w))
    grouped = (jax.ShapeDtypeStruct((BATCH, tiles, REGROUP, tm // REGROUP, ATT_WIDTH), BF16),
               pl.BlockSpec((None, None, REGROUP, tm // REGROUP, ATT_WIDTH),
                            lambda i: (i // tiles, i % tiles, 0, 0, 0)))
    outs = [tok(ATT_WIDTH, BF16)] * 3 + [grouped] * 3 + [
        tok(SSD_INNER, BF16), tok(SSD_INNER, BF16), tok(bc, BF16), tok(bc, BF16),
        tok(SSD_HEADS, F32),
        (jax.ShapeDtypeStruct((SSD_HEADS, TOKENS), F32),
         pl.BlockSpec((SSD_HEADS, tm), lambda i: (0, i))),
        tok(2 * D_MODEL, BF16),
    ]
    return pl.pallas_call(
        _inproj_kernel,
        grid=(TOKENS // tm,),
        in_specs=[_tile_spec(D_MODEL), _mod_spec(), _resident((1, D_MODEL)),
                  _tile_spec(LANES), _resident((1, LANES)), _resident((1, LANES)),
                  _resident(perm.shape),
                  _resident(wqkv.shape), _resident(wz.shape), _resident(wxbc.shape),
                  _resident(wdt.shape), _resident(wdtT.shape), _resident(wg.shape),
                  _resident(conv_w.shape), _resident(conv_b.shape),
                  _resident(dtb_row.shape), _resident(dtb_col.shape)],
        out_specs=[s for _, s in outs],
        out_shape=[s for s, _ in outs],
        scratch_shapes=[pltpu.VMEM((tm, D_MODEL), BF16),
                        pltpu.VMEM((SUBLANES + tm, SSD_CONV_CH), F32)],
        compiler_params=_params("arbitrary"),
        name="in_proj",
    )(x1, mod, nw, posf, invf, sgn, perm, wqkv, wz, wxbc, wdt, wdtT, wg,
      conv_w, conv_b, dtb_row, dtb_col)


def _band_bias(row_pos):
    blk = ATT_BLOCK
    pos = row_pos(np.arange(blk))
    k_pos = np.concatenate([pos, pos + blk])
    dist = pos[:, None] + blk - k_pos[None, :]
    return jnp.asarray(np.where((dist >= 0) & (dist <= blk), 0.0, NEG_BIG), F32)


def _attn_kernel(band_ref, q_ref, kp_ref, kc_ref, vp_ref, vc_ref, o_ref, lse_ref):
    blk = ATT_BLOCK
    has_prev = pl.program_id(2) > 0
    flat = lambda ref: ref[...].reshape(blk, ref.shape[-1])
    q = flat(q_ref)
    k = jnp.concatenate([flat(kp_ref), flat(kc_ref)], axis=0)
    v = jnp.concatenate([flat(vp_ref), flat(vc_ref)], axis=0)
    ki = lax.broadcasted_iota(jnp.int32, (blk, 2 * blk), 1)
    bias = jnp.where((ki >= blk) | has_prev, band_ref[...], NEG_BIG)
    lane_row = lax.broadcasted_iota(jnp.int32, (1, LANES), 1)
    lane = lax.broadcasted_iota(jnp.int32, (blk, LANES), 1)
    low = lane < ATT_HEAD_DIM
    head_mask = [(lane_row < ATT_HEAD_DIM).astype(BF16), (lane_row >= ATT_HEAD_DIM).astype(BF16)]
    lse_tile = jnp.zeros((blk, LANES), F32)
    o_pairs = []
    for pair in range(ATT_HEADS // 2):
        cols = slice(pair * LANES, (pair + 1) * LANES)
        qp, kpair, vpair = q[:, cols], k[:, cols], v[:, cols]
        outs = []
        for half in range(2):
            s = lax.dot_general(qp * head_mask[half], kpair, _NT,
                                preferred_element_type=F32) + bias
            m = jnp.max(s, axis=1, keepdims=True)
            p = jnp.exp(s - m)
            den = jnp.sum(p, axis=1, keepdims=True)
            o = jnp.dot(p.astype(BF16), vpair, preferred_element_type=F32)
            outs.append(o * (1.0 / den))
            lse_tile = jnp.where(lane == 2 * pair + half, m + jnp.log(den), lse_tile)
        o_pairs.append(jnp.where(low, outs[0], outs[1]).astype(BF16))
    o_ref[...] = jnp.concatenate(o_pairs, axis=1).reshape(o_ref.shape)
    lse_ref[...] = lse_tile.reshape(lse_ref.shape)


def _attention_natural(q, k, v):
    n_blk = SEQ // ATT_BLOCK
    view = lambda t: t.reshape(BATCH, SEQ, ATT_WIDTH)
    cur = pl.BlockSpec((None, ATT_BLOCK, ATT_WIDTH), lambda b, j, i: (b, i, 0))
    prev = pl.BlockSpec((None, ATT_BLOCK, ATT_WIDTH), lambda b, j, i: (b, jnp.maximum(i - 1, 0), 0))
    o, lse = pl.pallas_call(
        _attn_kernel,
        grid=(BATCH, 1, n_blk),
        in_specs=[_resident((ATT_BLOCK, 2 * ATT_BLOCK)), cur, prev, cur, prev, cur],
        out_specs=[cur, pl.BlockSpec((None, ATT_BLOCK, LANES), lambda b, j, i: (b, i, 0))],
        out_shape=[jax.ShapeDtypeStruct((BATCH, SEQ, ATT_WIDTH), BF16),
                   jax.ShapeDtypeStruct((BATCH, SEQ, LANES), F32)],
        compiler_params=_params("arbitrary", "arbitrary", "arbitrary"),
        name="attn_dil1",
    )(_band_bias(lambda r: r), view(q), view(k), view(k), view(v), view(v))
    return o.reshape(TOKENS, ATT_WIDTH), lse.reshape(TOKENS, LANES)


def _attention_regrouped(qg, kg, vg, r):
    tiles, rows = qg.shape[1], qg.shape[3]
    fold = REGROUP // r
    tiles_per_blk = ATT_BLOCK // (fold * rows)
    n_blk = tiles // tiles_per_blk
    shape6 = lambda w: (BATCH, tiles, fold, r, rows, w)
    view = lambda t: t.reshape(shape6(t.shape[-1]))
    blk6 = lambda w: (None, tiles_per_blk, fold, None, rows, w)
    cur = lambda w: pl.BlockSpec(blk6(w), lambda b, j, i: (b, i, 0, j, 0, 0))
    prev = pl.BlockSpec(blk6(ATT_WIDTH), lambda b, j, i: (b, jnp.maximum(i - 1, 0), 0, j, 0, 0))
    per_tile = fold * rows

    def row_pos(rho):
        t = rho // per_tile
        a = (rho // rows) % fold
        return t * per_tile + (rho % rows) * fold + a

    o, lse = pl.pallas_call(
        _attn_kernel,
        grid=(BATCH, r, n_blk),
        in_specs=[_resident((ATT_BLOCK, 2 * ATT_BLOCK)),
                  cur(ATT_WIDTH), prev, cur(ATT_WIDTH), prev, cur(ATT_WIDTH)],
        out_specs=[cur(ATT_WIDTH), cur(LANES)],
        out_shape=[jax.ShapeDtypeStruct(shape6(ATT_WIDTH), BF16),
                   jax.ShapeDtypeStruct(shape6(LANES), F32)],
        compiler_params=_params("arbitrary", "arbitrary", "arbitrary"),
        name=f"attn_dil{r}",
    )(_band_bias(row_pos), view(qg), view(kg), view(kg), view(vg), view(vg))
    return o.reshape(qg.shape), lse.reshape(qg.shape[:-1] + (LANES,))


def _ssd_kernel(xs_ref, bm_ref, cm_ref, dt_ref, dtT_ref, zs_ref,
                alog_row_ref, alog_col_ref, dskip_ref, nw_ref,
                o_ref, state_ref, y_ref):
    q = SSD_CHUNK

    @pl.when(pl.program_id(1) == 0)
    def _():
        state_ref[...] = jnp.zeros_like(state_ref)

    dtT = dtT_ref[...]
    a_dt_col = dt_ref[...] * (-jnp.exp(alog_row_ref[...]))
    a_dt_row = dtT * (-jnp.exp(alog_col_ref[...]))
    r = lax.broadcasted_iota(jnp.int32, (q, q), 0)
    c = lax.broadcasted_iota(jnp.int32, (q, q), 1)
    causal = r >= c
    hi = lax.Precision.HIGHEST
    cs_col = jnp.dot(causal.astype(F32), a_dt_col, precision=hi, preferred_element_type=F32)
    cs_row = jnp.dot(a_dt_row, (r <= c).astype(F32), precision=hi, preferred_element_type=F32)
    total = cs_row[:, q - 1:q]
    to_end = dtT * jnp.exp(total - cs_row)
    chunk_decay = jnp.exp(total)
    eye = (r == c).astype(BF16)
    r2 = lax.broadcasted_iota(jnp.int32, (q, 2 * q), 0)
    c2 = lax.broadcasted_iota(jnp.int32, (q, 2 * q), 1)
    keep = (r2 >= c2) | (c2 >= q)
    top_rows = r < SSD_HEAD_DIM
    low = c < SSD_HEAD_DIM
    zeros_row = jnp.zeros((1, q), F32)
    ones_row = jnp.ones((1, q), F32)

    for g in range(SSD_GROUPS):
        gcols = slice(g * SSD_STATE, (g + 1) * SSD_STATE)
        b_g = bm_ref[:, gcols]
        c_g = cm_ref[:, gcols]
        cb = lax.dot_general(c_g, b_g, _NT, preferred_element_type=F32)
        cbc = jnp.concatenate([cb, c_g.astype(F32)], axis=1)
        for pr in range(2):
            pair = 2 * g + pr
            h0 = 2 * pair
            pcols = slice(pair * LANES, (pair + 1) * LANES)
            x_pair = xs_ref[:, pcols]
            state = state_ref[pair]
            x_t = lax.dot_general(eye, x_pair, _NT, preferred_element_type=F32)
            rhs_t = jnp.concatenate([x_t.astype(BF16), state.astype(BF16)], axis=1)
            ys = []
            for h in (h0, h0 + 1):
                diff = cs_col[:, h:h + 1] - jnp.concatenate([cs_row[h:h + 1], zeros_row], axis=1)
                scale = jnp.concatenate([dtT[h:h + 1], ones_row], axis=1)
                w = jnp.where(keep, cbc * jnp.exp(diff) * scale, 0.0).astype(BF16)
                ys.append(lax.dot_general(w, rhs_t, _NT, preferred_element_type=F32))
            y_ref[:, pcols] = jnp.where(low, ys[0], ys[1])
            w_end = jnp.where(top_rows, to_end[h0:h0 + 1], to_end[h0 + 1:h0 + 2])
            upd = jnp.dot((x_t * w_end).astype(BF16), b_g, preferred_element_type=F32)
            decay = jnp.where(top_rows, chunk_decay[h0:h0 + 1], chunk_decay[h0 + 1:h0 + 2])
            state_ref[pair] = state * decay + upd

    y = (y_ref[...] + dskip_ref[...] * xs_ref[...].astype(F32)) * zs_ref[...].astype(F32)
    o_ref[...] = _rms(y, nw_ref[...]).astype(BF16)


def _ssd(xs, bm, cm, dt, dtT, zs, alog_row, alog_col, dskip, nw):
    n_chunks = SEQ // SSD_CHUNK
    tok = lambda w: pl.BlockSpec((SSD_CHUNK, w), lambda b, c: (b * n_chunks + c, 0))
    bc = SSD_GROUPS * SSD_STATE
    return pl.pallas_call(
        _ssd_kernel,
        grid=(BATCH, n_chunks),
        in_specs=[tok(SSD_INNER), tok(bc), tok(bc), tok(SSD_HEADS),
                  pl.BlockSpec((SSD_HEADS, SSD_CHUNK), lambda b, c: (0, b * n_chunks + c)),
                  tok(SSD_INNER),
                  _resident((1, SSD_HEADS)), _resident((SSD_HEADS, 1)),
                  _resident((1, SSD_INNER)), _resident((1, SSD_INNER))],
        out_specs=tok(SSD_INNER),
        out_shape=jax.ShapeDtypeStruct((TOKENS, SSD_INNER), BF16),
        scratch_shapes=[pltpu.VMEM((SSD_HEADS // 2, LANES, SSD_STATE), F32),
                        pltpu.VMEM((SSD_CHUNK, SSD_INNER), F32)],
        compiler_params=_params("arbitrary", "arbitrary"),
        name="ssd_scan",
    )(xs, bm, cm, dt, dtT, zs, alog_row, alog_col, dskip, nw)


def _out_kernel(x_ref, mod_ref, o1_ref, l1_ref, o4_ref, l4_ref, o16_ref, l16_ref,
                yn_ref, gates_ref, expand_ref, restore_ref, watt_ref, wssd_ref, wmix_ref,
                n3w_ref, wgu_ref, wd_ref, fnw_ref, out_ref):
    tm = x_ref.shape[0]
    mod = mod_ref[0]
    expand = lambda a: jnp.dot(a.astype(BF16), expand_ref[...], preferred_element_type=F32)
    rows = lambda ref: ref[...].reshape(tm, ref.shape[-1])

    def mix(lse_a, o_a, lse_b, o_b):
        m = jnp.maximum(lse_a, lse_b)
        e_a, e_b = jnp.exp(lse_a - m), jnp.exp(lse_b - m)
        tot = e_a + e_b
        inv = 1.0 / tot
        return m + jnp.log(tot), expand(e_a * inv) * o_a + expand(e_b * inv) * o_b

    lse_g, o_g = mix(rows(l4_ref), rows(o4_ref).astype(F32), rows(l16_ref), rows(o16_ref).astype(F32))
    restore = restore_ref[...]
    o_g = jnp.dot(restore, o_g.astype(BF16), preferred_element_type=F32)
    lse_rest, lse_g_tok = lse_g, None
    for _ in range(3):
        piece = lse_rest.astype(BF16)
        moved = jnp.dot(restore, piece, preferred_element_type=F32)
        lse_g_tok = moved if lse_g_tok is None else lse_g_tok + moved
        lse_rest = lse_rest - piece.astype(F32)
    _, o_att = mix(l1_ref[...], o1_ref[...].astype(F32), lse_g_tok, o_g)

    y_att = jnp.dot(o_att.astype(BF16), watt_ref[...], preferred_element_type=F32)
    y_ssd = jnp.dot(yn_ref[...], wssd_ref[...], preferred_element_type=F32)
    gates = gates_ref[...].astype(F32)
    merged = gates[:, :D_MODEL] * y_att + gates[:, D_MODEL:] * y_ssd
    y = jnp.dot(merged.astype(BF16), wmix_ref[...], preferred_element_type=F32)
    x2 = x_ref[...] + mod[5:6] * y
    h = _rms_mod(x2, n3w_ref[...], mod[6:7], mod[7:8]).astype(BF16)
    x3 = x2 + (0.5 * mod[8:9]) * _swiglu(h, wgu_ref, wd_ref)
    out_ref[...] = _rms(x3, fnw_ref[...])


def _out(x1, mod, o1, l1, o4, l4, o16, l16, yn, gates, expand, restore,
         watt, wssd, wmix, n3w, wgu, wd, fnw):
    tm = TOKEN_TILE
    tiles = SEQ // tm
    grouped = lambda w: pl.BlockSpec((None, None, REGROUP, tm // REGROUP, w),
                                     lambda i: (i // tiles, i % tiles, 0, 0, 0))
    return pl.pallas_call(
        _out_kernel,
        grid=(TOKENS // tm,),
        in_specs=[_tile_spec(D_MODEL), _mod_spec(), _tile_spec(ATT_WIDTH), _tile_spec(LANES),
                  grouped(ATT_WIDTH), grouped(LANES), grouped(ATT_WIDTH), grouped(LANES),
                  _tile_spec(SSD_INNER), _tile_spec(2 * D_MODEL)]
                 + [_resident(a.shape) for a in (expand, restore, watt, wssd, wmix, n3w, wgu, wd, fnw)],
        out_specs=_tile_spec(D_MODEL),
        out_shape=jax.ShapeDtypeStruct((TOKENS, D_MODEL), F32),
        compiler_params=_params("arbitrary"),
        name="mix_out_ffn2",
    )(x1, mod, o1, l1, o4, l4, o16, l16, yn, gates, expand, restore,
      watt, wssd, wmix, n3w, wgu, wd, fnw)


def _rope_tables():
    inv_freq = ROPE_THETA ** (-jnp.arange(0, ROPE_DIM, 2, dtype=F32) / ROPE_DIM)
    d = np.arange(LANES) % ATT_HEAD_DIM
    half = ROPE_DIM // 2
    invf = jnp.where(d < ROPE_DIM, inv_freq[d % half], 0.0).astype(F32).reshape(1, LANES)
    sgn = np.where(d < half, -1.0, np.where(d < ROPE_DIM, 1.0, 0.0)).astype(np.float32)
    return invf, jnp.asarray(sgn).reshape(1, LANES)


def _regroup_matrix(tm):
    rho = np.arange(tm)
    src = REGROUP * (rho % (tm // REGROUP)) + rho // (tm // REGROUP)
    return (np.arange(tm)[None, :] == src[:, None]).astype(np.float32)


def kernel(x, c, positions, w_ada, b_ada, norm1_w, ffn1_w_gu, ffn1_w_down, norm2_w, w_in,
           conv_w, conv_b, dt_bias, a_log, d_skip, ssd_norm_w, w_att_out, w_ssd_out,
           w_mix_out, norm3_w, ffn2_w_gu, ffn2_w_down, final_norm_w):
    l = 0
    row = lambda t: t.reshape(1, -1).astype(F32)
    xf = x.reshape(TOKENS, D_MODEL)
    mod = _ada(c, w_ada[l], b_ada[l]).reshape(BATCH, N_MOD, D_MODEL)

    x1 = _ffn1(xf, mod, row(norm1_w[l]), ffn1_w_gu[l].astype(BF16), ffn1_w_down[l].astype(BF16))

    o_z = 3 * ATT_WIDTH
    o_xbc = o_z + SSD_INNER
    o_dt = o_xbc + SSD_CONV_CH
    o_g = o_dt + SSD_HEADS
    w = w_in[l]
    w_dt = w[:, o_dt:o_g]
    invf, sgn = _rope_tables()
    perm = _regroup_matrix(TOKEN_TILE)
    posf = jnp.broadcast_to(positions.reshape(TOKENS, 1).astype(F32), (TOKENS, LANES))
    q, k, v, qg, kg, vg, zs, xs, bm, cm, dt, dtT, gates = _inproj(
        x1, mod, row(norm2_w[l]), posf, invf, sgn, jnp.asarray(perm, BF16),
        w[:, :o_z].astype(BF16), w[:, o_z:o_xbc].astype(BF16), w[:, o_xbc:o_dt].astype(BF16),
        jnp.pad(w_dt, ((0, 0), (0, LANES - SSD_HEADS))).astype(BF16), w_dt.T.astype(BF16),
        w[:, o_g:].astype(BF16),
        conv_w[l], row(conv_b[l]), row(dt_bias[l]), dt_bias[l].reshape(SSD_HEADS, 1))

    o1, l1 = _attention_natural(q, k, v)
    o4, l4 = _attention_regrouped(qg, kg, vg, 4)
    o16, l16 = _attention_regrouped(qg, kg, vg, 16)

    yn = _ssd(xs, bm, cm, dt, dtT, zs, row(a_log[l]), a_log[l].reshape(SSD_HEADS, 1),
              row(jnp.repeat(d_skip[l], SSD_HEAD_DIM)), row(ssd_norm_w[l]))

    head_of_col = np.arange(ATT_WIDTH) // ATT_HEAD_DIM
    expand = jnp.asarray(np.arange(LANES)[:, None] == head_of_col[None, :], BF16)
    out = _out(x1, mod, o1, l1, o4, l4, o16, l16, yn, gates, expand, jnp.asarray(perm.T, BF16),
               w_att_out[l].astype(BF16), w_ssd_out[l].astype(BF16), w_mix_out[l].astype(BF16),
               row(norm3_w[l]), ffn2_w_gu[l].astype(BF16), ffn2_w_down[l].astype(BF16),
               row(final_norm_w))
    return out.reshape(BATCH, SEQ, D_MODEL)
```

```python
import jax
import jax.numpy as jnp
import numpy as np
from jax import lax
from jax.experimental import pallas as pl
from jax.experimental.pallas import tpu as pltpu

F32 = jnp.float32
BF16 = jnp.bfloat16

D_MODEL = 1024
BATCH = 2
SEQ = 8192
TOKENS = BATCH * SEQ
ATT_HEADS = 12
ATT_HEAD_DIM = 64
ATT_WIDTH = ATT_HEADS * ATT_HEAD_DIM
ROPE_DIM = ATT_HEAD_DIM // 4
ROPE_THETA = 500000.0
ATT_BLOCK = 128
SSD_INNER = 2 * D_MODEL
SSD_HEAD_DIM = 64
SSD_HEADS = SSD_INNER // SSD_HEAD_DIM
SSD_GROUPS = 8
SSD_STATE = 128
SSD_CONV = 4
SSD_CHUNK = 128
SSD_CONV_CH = SSD_INNER + 2 * SSD_GROUPS * SSD_STATE
D_FF = 2816
N_MOD = 9
NORM_EPS = 1e-6

LANES = 128
SUBLANES = 8
VMEM_LIMIT = 56 * 1024 * 1024
TOKEN_TILE = 256
COL_CHUNK = 256
REGROUP = 16
ATT_STEP_BLOCKS = 4
ADA_COL_TILE = 1536
NEG_BIG = -1e30

_NT = (((1,), (1,)), ((), ()))


def _params(*sem):
    return pltpu.CompilerParams(dimension_semantics=sem, vmem_limit_bytes=VMEM_LIMIT)


def _resident(shape):
    zeros = (0,) * len(shape)
    return pl.BlockSpec(shape, lambda *_: zeros, pipeline_mode=pl.Buffered(1))


def _sigmoid(x):
    return 0.5 * jnp.tanh(0.5 * x) + 0.5


def _silu(x):
    return x * _sigmoid(x)


def _rms(x, w):
    ms = jnp.sum(x * x, axis=-1, keepdims=True) * (1.0 / x.shape[-1])
    return x * lax.rsqrt(ms + NORM_EPS) * w


def _rms_mod(x, w, shift, scale):
    return _rms(x, w) * (1.0 + scale) + shift


def _swiglu(h, wgu_ref, wd_ref):
    gu = jnp.dot(h, wgu_ref[...], preferred_element_type=F32)
    a = (_silu(gu[:, :D_FF]) * gu[:, D_FF:]).astype(BF16)
    return jnp.dot(a, wd_ref[...], preferred_element_type=F32)


def _ada_kernel(ct_ref, w_ref, b_ref, o_ref):
    act = _silu(ct_ref[...])
    w = w_ref[...]
    rows = [jnp.sum(w * act[:, b:b + 1], axis=0, keepdims=True) for b in range(BATCH)]
    o_ref[...] = jnp.concatenate(rows, axis=0) + b_ref[...]


def _ada(c, w_ada, b_ada, layer):
    n = N_MOD * D_MODEL
    return pl.pallas_call(
        _ada_kernel,
        grid=(n // ADA_COL_TILE,),
        in_specs=[
            pl.BlockSpec((D_MODEL, BATCH), lambda j: (0, 0)),
            pl.BlockSpec((None, D_MODEL, ADA_COL_TILE), lambda j: (layer, 0, j)),
            pl.BlockSpec((None, 1, ADA_COL_TILE), lambda j: (layer, 0, j)),
        ],
        out_specs=pl.BlockSpec((BATCH, ADA_COL_TILE), lambda j: (0, j)),
        out_shape=jax.ShapeDtypeStruct((BATCH, n), F32),
        compiler_params=_params("arbitrary"),
        name="ada_mod",
    )(c.T, w_ada, b_ada.reshape(-1, 1, n))


def _ffn1_kernel(x_ref, mod_ref, nw_ref, wgu_ref, wd_ref, o_ref):
    x = x_ref[...]
    mod = mod_ref[0]
    h = _rms_mod(x, nw_ref[...], mod[0:1], mod[1:2]).astype(BF16)
    o_ref[...] = x + (0.5 * mod[2:3]) * _swiglu(h, wgu_ref, wd_ref)


def _tile_spec(width, tm=TOKEN_TILE):
    return pl.BlockSpec((tm, width), lambda i: (i, 0))


def _mod_spec(tm=TOKEN_TILE):
    per_batch = SEQ // tm
    return pl.BlockSpec((1, N_MOD, D_MODEL), lambda i: (i // per_batch, 0, 0))


def _ffn1(x, mod, nw, wgu, wd):
    return pl.pallas_call(
        _ffn1_kernel,
        grid=(TOKENS // TOKEN_TILE,),
        in_specs=[_tile_spec(D_MODEL), _mod_spec(), _resident((1, D_MODEL)),
                  _resident(wgu.shape), _resident(wd.shape)],
        out_specs=_tile_spec(D_MODEL),
        out_shape=jax.ShapeDtypeStruct((TOKENS, D_MODEL), F32),
        compiler_params=_params("arbitrary"),
        name="ffn1",
    )(x, mod, nw, wgu, wd)


def _softplus(x):
    return jnp.maximum(x, 0.0) + jnp.log1p(jnp.exp(-jnp.abs(x)))


def _col_loop(n_cols, body):
    for c in range(n_cols // COL_CHUNK):
        body(pl.ds(c * COL_CHUNK, COL_CHUNK))


def _inproj_kernel(x_ref, mod_ref, nw_ref, pos_ref, invf_ref, sgn_ref, perm_ref, weave_ref,
                   wqkv_ref, wz_ref, wxbc_ref, wdt_ref, wdtT_ref, wg_ref,
                   convw_ref, convb_ref, dtb_row_ref, dtb_col_ref,
                   q_ref, k_ref, v_ref, qg_ref, kg_ref, vg_ref,
                   zs_ref, xs_ref, bm_ref, cm_ref, dt_ref, dtT_ref, gates_ref,
                   h_ref, hw_ref, hist_ref, cos_ref, sin_ref):
    tm = x_ref.shape[0]
    mod = mod_ref[0]
    h_ref[...] = _rms_mod(x_ref[...], nw_ref[...], mod[3:4], mod[4:5]).astype(BF16)
    hw_ref[...] = jnp.dot(weave_ref[...], h_ref[...], preferred_element_type=F32).astype(BF16)

    def rope_tables():
        ang = pos_ref[...] * invf_ref[...]
        cos_ref[...] = jnp.cos(ang)
        sin_ref[...] = jnp.sin(ang) * sgn_ref[...]

    def rope(t):
        reps = ATT_WIDTH // LANES
        cos = jnp.concatenate([cos_ref[...]] * reps, axis=1)
        sin = jnp.concatenate([sin_ref[...]] * reps, axis=1)
        lane = lax.broadcasted_iota(jnp.int32, (tm, ATT_WIDTH), 1)
        first_half = (lane & (ATT_HEAD_DIM - 1)) < (ROPE_DIM // 2)
        partner = jnp.where(first_half,
                            pltpu.roll(t, ATT_WIDTH - ROPE_DIM // 2, 1),
                            pltpu.roll(t, ROPE_DIM // 2, 1))
        return t * cos + partner * sin

    def qkv_part(idx, tok_ref, grouped_ref):
        t = jnp.dot(h_ref[...], wqkv_ref[:, idx * ATT_WIDTH:(idx + 1) * ATT_WIDTH],
                    preferred_element_type=F32)
        if idx == 0:
            t = rope(t) * (ATT_HEAD_DIM ** -0.5)
        elif idx == 1:
            t = rope(t)
        t = t.astype(BF16)
        tok_ref[...] = t
        regrouped = jnp.dot(perm_ref[...], t, preferred_element_type=F32).astype(BF16)
        grouped_ref[...] = regrouped.reshape(grouped_ref.shape)

    def z_chunk(cols):
        z = jnp.dot(hw_ref[...], wz_ref[:, cols], preferred_element_type=F32)
        zs_ref[:, cols] = _silu(z).astype(BF16)

    def gate_chunk(cols):
        g = jnp.dot(h_ref[...], wg_ref[:, cols], preferred_element_type=F32)
        gates_ref[:, cols] = _sigmoid(g).astype(BF16)

    tail_rows = (SSD_CONV - 1) * SUBLANES
    tail_start = SSD_CHUNK - tail_rows

    @pl.when(pl.program_id(0) % (SEQ // tm) == 0)
    def _():
        hist_ref[...] = jnp.zeros_like(hist_ref)

    first_sublane = lax.broadcasted_iota(jnp.int32, (SUBLANES, COL_CHUNK), 0) == 0

    def conv_chunk(out_ref, base, cols):
        src = pl.ds(cols.start + base, COL_CHUNK)
        raw = jnp.concatenate(
            [jnp.dot(hw_ref[ck * SSD_CHUNK:(ck + 1) * SSD_CHUNK, :], wxbc_ref[:, src],
                     preferred_element_type=F32) for ck in range(tm // SSD_CHUNK)], axis=0)
        taps = [convw_ref[k:k + 1, src] for k in range(SSD_CONV)]
        prev_tail = hist_ref[:, src]
        pieces = []
        for ck in range(tm // SSD_CHUNK):
            cur = raw[ck * SSD_CHUNK:(ck + 1) * SSD_CHUNK]
            cur_tail = cur[tail_start:]
            wrapped = []
            for k in range(SSD_CONV - 1):
                rows = slice(k * SUBLANES, (k + 1) * SUBLANES)
                wrapped.append(jnp.where(first_sublane, pltpu.roll(prev_tail[rows], 1, 0),
                                         pltpu.roll(cur_tail[rows], 1, 0)))
            ext = jnp.concatenate(wrapped + [cur], axis=0)
            acc = cur * taps[SSD_CONV - 1] + convb_ref[:, src]
            for s in range(1, SSD_CONV):
                lo = (SSD_CONV - 1 - s) * SUBLANES
                acc = acc + ext[lo:lo + SSD_CHUNK] * taps[SSD_CONV - 1 - s]
            pieces.append(acc)
            prev_tail = cur_tail
        hist_ref[:, src] = prev_tail
        out_ref[:, cols] = _silu(jnp.concatenate(pieces, axis=0)).astype(BF16)

    bc = SSD_GROUPS * SSD_STATE
    chunks = lambda width: [pl.ds(c * COL_CHUNK, COL_CHUNK) for c in range(width // COL_CHUNK)]
    heavy = ([(conv_chunk, (xs_ref, 0, c)) for c in chunks(SSD_INNER)]
             + [(conv_chunk, (bm_ref, SSD_INNER, c)) for c in chunks(bc)]
             + [(conv_chunk, (cm_ref, SSD_INNER + bc, c)) for c in chunks(bc)])
    light = ([(z_chunk, (c,)) for c in chunks(SSD_INNER)]
             + [(gate_chunk, (c,)) for c in chunks(2 * D_MODEL)]
             + [(qkv_part, (2, v_ref, vg_ref)), (rope_tables, ()),
                (qkv_part, (0, q_ref, qg_ref)), (qkv_part, (1, k_ref, kg_ref))])
    for i in range(max(len(heavy), len(light))):
        for work in (heavy, light):
            if i < len(work):
                fn, args = work[i]
                fn(*args)

    h = hw_ref[...]
    dt_raw = jnp.dot(h, wdt_ref[...], preferred_element_type=F32)[:, :SSD_HEADS]
    dt_ref[...] = _softplus(dt_raw + dtb_row_ref[...])
    dtT_raw = lax.dot_general(wdtT_ref[...], h, _NT, preferred_element_type=F32)
    dtT_ref[...] = _softplus(dtT_raw + dtb_col_ref[...])


def _inproj(x1, mod, nw, posf, invf, sgn, perm, weave, wqkv, wz, wxbc, wdt, wdtT, wg,
            conv_w, conv_b, dtb_row, dtb_col):
    tm = TOKEN_TILE
    bc = SSD_GROUPS * SSD_STATE
    tiles = SEQ // tm
    tok = lambda w, d: (jax.ShapeDtypeStruct((TOKENS, w), d), _tile_spec(w))
    grouped = (jax.ShapeDtypeStruct((BATCH, tiles, REGROUP, tm // REGROUP, ATT_WIDTH), BF16),
               pl.BlockSpec((None, None, REGROUP, tm // REGROUP, ATT_WIDTH),
                            lambda i: (i // tiles, i % tiles, 0, 0, 0)))
    outs = [tok(ATT_WIDTH, BF16)] * 3 + [grouped] * 3 + [
        tok(SSD_INNER, BF16), tok(SSD_INNER, BF16), tok(bc, BF16), tok(bc, BF16),
        tok(SSD_HEADS, F32),
        (jax.ShapeDtypeStruct((SSD_HEADS, TOKENS), F32),
         pl.BlockSpec((SSD_HEADS, tm), lambda i: (0, i))),
        tok(2 * D_MODEL, BF16),
    ]
    return pl.pallas_call(
        _inproj_kernel,
        grid=(TOKENS // tm,),
        in_specs=[_tile_spec(D_MODEL), _mod_spec(), _resident((1, D_MODEL)),
                  _tile_spec(LANES), _resident((1, LANES)), _resident((1, LANES)),
                  _resident(perm.shape), _resident(weave.shape),
                  _resident(wqkv.shape), _resident(wz.shape), _resident(wxbc.shape),
                  _resident(wdt.shape), _resident(wdtT.shape), _resident(wg.shape),
                  _resident(conv_w.shape), _resident(conv_b.shape),
                  _resident(dtb_row.shape), _resident(dtb_col.shape)],
        out_specs=[s for _, s in outs],
        out_shape=[s for s, _ in outs],
        scratch_shapes=[pltpu.VMEM((tm, D_MODEL), BF16), pltpu.VMEM((tm, D_MODEL), BF16),
                        pltpu.VMEM(((SSD_CONV - 1) * SUBLANES, SSD_CONV_CH), F32),
                        pltpu.VMEM((tm, LANES), F32), pltpu.VMEM((tm, LANES), F32)],
        compiler_params=_params("arbitrary"),
        name="in_proj",
    )(x1, mod, nw, posf, invf, sgn, perm, weave, wqkv, wz, wxbc, wdt, wdtT, wg,
      conv_w, conv_b, dtb_row, dtb_col)


def _band_bias(row_pos):
    blk = ATT_BLOCK
    pos = row_pos(np.arange(blk))
    k_pos = np.concatenate([pos, pos + blk])
    dist = pos[:, None] + blk - k_pos[None, :]
    return jnp.asarray(np.where((dist >= 0) & (dist <= blk), 0.0, NEG_BIG), F32)


def _attn_kernel(band_ref, q_ref, kp_ref, kc_ref, vp_ref, vc_ref, o_ref, lse_ref):
    blk = ATT_BLOCK
    has_prev = pl.program_id(2) > 0
    flat = lambda ref: ref[...].reshape(-1, ref.shape[-1])
    q = flat(q_ref)
    k = jnp.concatenate([flat(kp_ref), flat(kc_ref)], axis=0)
    v = jnp.concatenate([flat(vp_ref), flat(vc_ref)], axis=0)
    band = band_ref[...]
    ki = lax.broadcasted_iota(jnp.int32, (blk, 2 * blk), 1)
    first_bias = jnp.where((ki >= blk) | has_prev, band, NEG_BIG)
    lane_row = lax.broadcasted_iota(jnp.int32, (1, LANES), 1)
    lane = lax.broadcasted_iota(jnp.int32, (blk, LANES), 1)
    low = lane < ATT_HEAD_DIM
    head_mask = [(lane_row < ATT_HEAD_DIM).astype(BF16), (lane_row >= ATT_HEAD_DIM).astype(BF16)]
    o_blocks, lse_blocks = [], []
    for j in range(ATT_STEP_BLOCKS):
        bias = first_bias if j == 0 else band
        q_rows = slice(j * blk, (j + 1) * blk)
        k_rows = slice(j * blk, (j + 2) * blk)
        lse_tile = jnp.zeros((blk, LANES), F32)
        o_pairs = []
        for pair in range(ATT_HEADS // 2):
            cols = slice(pair * LANES, (pair + 1) * LANES)
            qp, kpair, vpair = q[q_rows, cols], k[k_rows, cols], v[k_rows, cols]
            outs = []
            for half in range(2):
                s = lax.dot_general(qp * head_mask[half], kpair, _NT,
                                    preferred_element_type=F32) + bias
                m = jnp.max(s, axis=1, keepdims=True)
                p = jnp.exp(s - m)
                den = jnp.sum(p, axis=1, keepdims=True)
                o = jnp.dot(p.astype(BF16), vpair, preferred_element_type=F32)
                outs.append(o * (1.0 / den))
                lse_tile = jnp.where(lane == 2 * pair + half, m + jnp.log(den), lse_tile)
            o_pairs.append(jnp.where(low, outs[0], outs[1]).astype(BF16))
        o_blocks.append(jnp.concatenate(o_pairs, axis=1))
        lse_blocks.append(lse_tile)
    o_ref[...] = jnp.concatenate(o_blocks, axis=0).reshape(o_ref.shape)
    lse_ref[...] = jnp.concatenate(lse_blocks, axis=0).reshape(lse_ref.shape)


def _attention_natural(q, k, v):
    step = ATT_STEP_BLOCKS * ATT_BLOCK
    n_blk = SEQ // step
    view = lambda t: t.reshape(BATCH, SEQ, ATT_WIDTH)
    cur = pl.BlockSpec((None, step, ATT_WIDTH), lambda b, j, i: (b, i, 0))
    prev = pl.BlockSpec((None, ATT_BLOCK, ATT_WIDTH),
                        lambda b, j, i: (b, jnp.maximum(ATT_STEP_BLOCKS * i - 1, 0), 0))
    o, lse = pl.pallas_call(
        _attn_kernel,
        grid=(BATCH, 1, n_blk),
        in_specs=[_resident((ATT_BLOCK, 2 * ATT_BLOCK)), cur, prev, cur, prev, cur],
        out_specs=[cur, pl.BlockSpec((None, step, LANES), lambda b, j, i: (b, i, 0))],
        out_shape=[jax.ShapeDtypeStruct((BATCH, SEQ, ATT_WIDTH), BF16),
                   jax.ShapeDtypeStruct((BATCH, SEQ, LANES), F32)],
        compiler_params=_params("arbitrary", "arbitrary", "arbitrary"),
        name="attn_dil1",
    )(_band_bias(lambda r: r), view(q), view(k), view(k), view(v), view(v))
    return o.reshape(TOKENS, ATT_WIDTH), lse.reshape(TOKENS, LANES)


def _attention_regrouped(qg, kg, vg, r):
    tiles, rows = qg.shape[1], qg.shape[3]
    fold = REGROUP // r
    tiles_per_blk = ATT_BLOCK // (fold * rows)
    n_blk = tiles // (ATT_STEP_BLOCKS * tiles_per_blk)
    shape6 = lambda w: (BATCH, tiles, fold, r, rows, w)
    view = lambda t: t.reshape(shape6(t.shape[-1]))
    blk6 = lambda n_tiles, w: (None, n_tiles, fold, None, rows, w)
    cur = lambda w: pl.BlockSpec(blk6(ATT_STEP_BLOCKS * tiles_per_blk, w),
                                 lambda b, j, i: (b, i, 0, j, 0, 0))
    prev = pl.BlockSpec(blk6(tiles_per_blk, ATT_WIDTH),
                        lambda b, j, i: (b, jnp.maximum(ATT_STEP_BLOCKS * i - 1, 0), 0, j, 0, 0))
    per_tile = fold * rows

    def row_pos(rho):
        t = rho // per_tile
        a = (rho // rows) % fold
        return t * per_tile + (rho % rows) * fold + a

    o, lse = pl.pallas_call(
        _attn_kernel,
        grid=(BATCH, r, n_blk),
        in_specs=[_resident((ATT_BLOCK, 2 * ATT_BLOCK)),
                  cur(ATT_WIDTH), prev, cur(ATT_WIDTH), prev, cur(ATT_WIDTH)],
        out_specs=[cur(ATT_WIDTH), cur(LANES)],
        out_shape=[jax.ShapeDtypeStruct(shape6(ATT_WIDTH), BF16),
                   jax.ShapeDtypeStruct(shape6(LANES), F32)],
        compiler_params=_params("arbitrary", "arbitrary", "arbitrary"),
        name=f"attn_dil{r}",
    )(_band_bias(row_pos), view(qg), view(kg), view(kg), view(vg), view(vg))
    return o.reshape(qg.shape), lse.reshape(qg.shape[:-1] + (LANES,))


def _woven_time(i):
    return ((i & (SUBLANES - 1)) << 4) | (i >> 3)


def _ssd_kernel(xs_ref, bm_ref, cm_ref, dt_ref, dtT_ref, zs_ref,
                alog_row_ref, alog_col_ref, dskip_ref, nw_ref, unweave_ref,
                o_ref, state_ref, y_ref):
    q = SSD_CHUNK
    heads = SSD_HEADS // SSD_GROUPS
    gw = heads * SSD_HEAD_DIM

    @pl.when(pl.program_id(1) == 0)
    def _():
        state_ref[...] = jnp.zeros_like(state_ref)

    dtT = dtT_ref[...]
    a_dt_col = dt_ref[...] * (-jnp.exp(alog_row_ref[...]))
    a_dt_row = dtT * (-jnp.exp(alog_col_ref[...]))
    t_row = _woven_time(lax.broadcasted_iota(jnp.int32, (q, q), 0))
    t_col = _woven_time(lax.broadcasted_iota(jnp.int32, (q, q), 1))
    causal = t_row >= t_col
    hi = lax.Precision.HIGHEST
    cs_col = jnp.dot(causal.astype(F32), a_dt_col, precision=hi, preferred_element_type=F32)
    cs_row = jnp.dot(a_dt_row, (t_row <= t_col).astype(F32), precision=hi,
                     preferred_element_type=F32)
    total = cs_row[:, q - 1:q]
    dt_decay_row = cs_row - jnp.log(dtT)
    to_end = dtT * jnp.exp(total - cs_row)
    chunk_decay = jnp.exp(total)
    r2 = lax.broadcasted_iota(jnp.int32, (gw, gw), 0)
    c2 = lax.broadcasted_iota(jnp.int32, (gw, gw), 1)
    eye = (r2 == c2).astype(BF16)
    no_rows = jnp.zeros((SSD_HEAD_DIM, q + SSD_STATE), BF16)
    sq_sum = jnp.zeros((q, LANES), F32)

    for g in range(SSD_GROUPS):
        gcols = slice(g * SSD_STATE, (g + 1) * SSD_STATE)
        b_g = bm_ref[:, gcols]
        c_g = cm_ref[:, gcols]
        c_f = c_g.astype(F32)
        cb = lax.dot_general(c_g, b_g, _NT, preferred_element_type=F32)
        x_g = xs_ref[:, g * gw:(g + 1) * gw]
        state = state_ref[g]
        x_t = lax.dot_general(eye, x_g, _NT, preferred_element_type=F32)
        rhs = jnp.concatenate([x_t.astype(BF16), state.astype(BF16)], axis=1)
        w_parts, rhs_rows, end_rows, decay_rows = [], [], [], []
        for j in range(heads):
            h = g * heads + j
            cs_l = jnp.broadcast_to(cs_col[:, h:h + 1], (q, q))
            w_parts.append(jnp.where(causal, cb * jnp.exp(cs_l - dt_decay_row[h:h + 1]), 0.0)
                           .astype(BF16))
            w_parts.append((c_f * jnp.exp(cs_l)).astype(BF16))
            mine = rhs[j * SSD_HEAD_DIM:(j + 1) * SSD_HEAD_DIM]
            rhs_rows.append(jnp.concatenate([mine if jj == j else no_rows for jj in range(heads)],
                                            axis=1))
            end_rows.append(jnp.broadcast_to(to_end[h:h + 1], (SSD_HEAD_DIM, q)))
            decay_rows.append(jnp.broadcast_to(chunk_decay[h:h + 1], (SSD_HEAD_DIM, SSD_STATE)))
        y = lax.dot_general(jnp.concatenate(w_parts, axis=1), jnp.concatenate(rhs_rows, axis=0),
                            _NT, preferred_element_type=F32)
        xcols = slice(g * gw, (g + 1) * gw)
        y = (y + dskip_ref[:, xcols] * x_g.astype(F32)) * zs_ref[:, xcols].astype(F32)
        y_ref[:, xcols] = y
        sq = y * y
        sq_sum = sq_sum + sq[:, :LANES] + sq[:, LANES:]
        upd = jnp.dot((x_t * jnp.concatenate(end_rows, axis=0)).astype(BF16), b_g,
                      preferred_element_type=F32)
        state_ref[g] = state * jnp.concatenate(decay_rows, axis=0) + upd

    ms = jnp.sum(sq_sum, axis=-1, keepdims=True) * (1.0 / SSD_INNER)
    y = (y_ref[...] * lax.rsqrt(ms + NORM_EPS) * nw_ref[...]).astype(BF16)
    o_ref[...] = jnp.dot(unweave_ref[...], y, preferred_element_type=F32).astype(BF16)


def _ssd(xs, bm, cm, dt, dtT, zs, alog_row, alog_col, dskip, nw, unweave):
    n_chunks = SEQ // SSD_CHUNK
    tok = lambda w: pl.BlockSpec((SSD_CHUNK, w), lambda b, c: (b * n_chunks + c, 0))
    bc = SSD_GROUPS * SSD_STATE
    gw = SSD_INNER // SSD_GROUPS
    return pl.pallas_call(
        _ssd_kernel,
        grid=(BATCH, n_chunks),
        in_specs=[tok(SSD_INNER), tok(bc), tok(bc), tok(SSD_HEADS),
                  pl.BlockSpec((SSD_HEADS, SSD_CHUNK), lambda b, c: (0, b * n_chunks + c)),
                  tok(SSD_INNER),
                  _resident((1, SSD_HEADS)), _resident((SSD_HEADS, 1)),
                  _resident((1, SSD_INNER)), _resident((1, SSD_INNER)),
                  _resident(unweave.shape)],
        out_specs=tok(SSD_INNER),
        out_shape=jax.ShapeDtypeStruct((TOKENS, SSD_INNER), BF16),
        scratch_shapes=[pltpu.VMEM((SSD_GROUPS, gw, SSD_STATE), F32),
                        pltpu.VMEM((SSD_CHUNK, SSD_INNER), F32)],
        compiler_params=_params("arbitrary", "arbitrary"),
        name="ssd_scan",
    )(xs, bm, cm, dt, dtT, zs, alog_row, alog_col, dskip, nw, unweave)


def _out_kernel(x_ref, mod_ref, o1_ref, l1_ref, o4_ref, l4_ref, o16_ref, l16_ref,
                yn_ref, gates_ref, expand_ref, restore_ref, watt_ref, wssd_ref, wmix_ref,
                n3w_ref, wgu_ref, wd_ref, fnw_ref, out_ref):
    tm = x_ref.shape[0]
    mod = mod_ref[0]
    expand = lambda a: jnp.dot(a.astype(BF16), expand_ref[...], preferred_element_type=F32)
    rows = lambda ref: ref[...].reshape(tm, ref.shape[-1])

    def mix(lse_a, o_a, lse_b, o_b):
        m = jnp.maximum(lse_a, lse_b)
        e_a, e_b = jnp.exp(lse_a - m), jnp.exp(lse_b - m)
        tot = e_a + e_b
        inv = 1.0 / tot
        return m + jnp.log(tot), expand(e_a * inv) * o_a + expand(e_b * inv) * o_b

    lse_g, o_g = mix(rows(l4_ref), rows(o4_ref).astype(F32), rows(l16_ref), rows(o16_ref).astype(F32))
    restore = restore_ref[...]
    o_g = jnp.dot(restore, o_g.astype(BF16), preferred_element_type=F32)
    lse_rest, lse_g_tok = lse_g, None
    for _ in range(3):
        piece = lse_rest.astype(BF16)
        moved = jnp.dot(restore, piece, preferred_element_type=F32)
        lse_g_tok = moved if lse_g_tok is None else lse_g_tok + moved
        lse_rest = lse_rest - piece.astype(F32)
    _, o_att = mix(l1_ref[...], o1_ref[...].astype(F32), lse_g_tok, o_g)

    y_att = jnp.dot(o_att.astype(BF16), watt_ref[...], preferred_element_type=F32)
    y_ssd = jnp.dot(yn_ref[...], wssd_ref[...], preferred_element_type=F32)
    gates = gates_ref[...].astype(F32)
    merged = gates[:, :D_MODEL] * y_att + gates[:, D_MODEL:] * y_ssd
    y = jnp.dot(merged.astype(BF16), wmix_ref[...], preferred_element_type=F32)
    x2 = x_ref[...] + mod[5:6] * y
    h = _rms_mod(x2, n3w_ref[...], mod[6:7], mod[7:8]).astype(BF16)
    x3 = x2 + (0.5 * mod[8:9]) * _swiglu(h, wgu_ref, wd_ref)
    out_ref[...] = _rms(x3, fnw_ref[...])


def _out(x1, mod, o1, l1, o4, l4, o16, l16, yn, gates, expand, restore,
         watt, wssd, wmix, n3w, wgu, wd, fnw):
    tm = TOKEN_TILE
    tiles = SEQ // tm
    grouped = lambda w: pl.BlockSpec((None, None, REGROUP, tm // REGROUP, w),
                                     lambda i: (i // tiles, i % tiles, 0, 0, 0))
    return pl.pallas_call(
        _out_kernel,
        grid=(TOKENS // tm,),
        in_specs=[_tile_spec(D_MODEL), _mod_spec(), _tile_spec(ATT_WIDTH), _tile_spec(LANES),
                  grouped(ATT_WIDTH), grouped(LANES), grouped(ATT_WIDTH), grouped(LANES),
                  _tile_spec(SSD_INNER), _tile_spec(2 * D_MODEL)]
                 + [_resident(a.shape) for a in (expand, restore, watt, wssd, wmix, n3w, wgu, wd, fnw)],
        out_specs=_tile_spec(D_MODEL),
        out_shape=jax.ShapeDtypeStruct((TOKENS, D_MODEL), F32),
        compiler_params=_params("arbitrary"),
        name="mix_out_ffn2",
    )(x1, mod, o1, l1, o4, l4, o16, l16, yn, gates, expand, restore,
      watt, wssd, wmix, n3w, wgu, wd, fnw)


def _rope_tables():
    inv_freq = ROPE_THETA ** (-jnp.arange(0, ROPE_DIM, 2, dtype=F32) / ROPE_DIM)
    d = np.arange(LANES) % ATT_HEAD_DIM
    half = ROPE_DIM // 2
    invf = jnp.where(d < ROPE_DIM, inv_freq[d % half], 0.0).astype(F32).reshape(1, LANES)
    sgn = np.where(d < half, -1.0, np.where(d < ROPE_DIM, 1.0, 0.0)).astype(np.float32)
    return invf, jnp.asarray(sgn).reshape(1, LANES)


def _regroup_matrix(tm):
    rho = np.arange(tm)
    src = REGROUP * (rho % (tm // REGROUP)) + rho // (tm // REGROUP)
    return (np.arange(tm)[None, :] == src[:, None]).astype(np.float32)


def _weave_matrix(tm):
    rho = np.arange(tm)
    src = (rho // SSD_CHUNK) * SSD_CHUNK + _woven_time(rho % SSD_CHUNK)
    return (np.arange(tm)[None, :] == src[:, None]).astype(np.float32)


def kernel(x, c, positions, w_ada, b_ada, norm1_w, ffn1_w_gu, ffn1_w_down, norm2_w, w_in,
           conv_w, conv_b, dt_bias, a_log, d_skip, ssd_norm_w, w_att_out, w_ssd_out,
           w_mix_out, norm3_w, ffn2_w_gu, ffn2_w_down, final_norm_w):
    l = 0
    row = lambda t: t.reshape(1, -1).astype(F32)
    xf = x.reshape(TOKENS, D_MODEL)
    mod = _ada(c, w_ada, b_ada, l).reshape(BATCH, N_MOD, D_MODEL)

    x1 = _ffn1(xf, mod, row(norm1_w[l]), ffn1_w_gu[l].astype(BF16), ffn1_w_down[l].astype(BF16))

    o_z = 3 * ATT_WIDTH
    o_xbc = o_z + SSD_INNER
    o_dt = o_xbc + SSD_CONV_CH
    o_g = o_dt + SSD_HEADS
    w = w_in[l]
    w_dt = w[:, o_dt:o_g]
    invf, sgn = _rope_tables()
    perm = _regroup_matrix(TOKEN_TILE)
    posf = jnp.broadcast_to(positions.reshape(TOKENS, 1).astype(F32), (TOKENS, LANES))
    q, k, v, qg, kg, vg, zs, xs, bm, cm, dt, dtT, gates = _inproj(
        x1, mod, row(norm2_w[l]), posf, invf, sgn, jnp.asarray(perm, BF16),
        jnp.asarray(_weave_matrix(TOKEN_TILE), BF16),
        w[:, :o_z].astype(BF16), w[:, o_z:o_xbc].astype(BF16), w[:, o_xbc:o_dt].astype(BF16),
        jnp.pad(w_dt, ((0, 0), (0, LANES - SSD_HEADS))).astype(BF16), w_dt.T.astype(BF16),
        w[:, o_g:].astype(BF16),
        conv_w[l], row(conv_b[l]), row(dt_bias[l]), dt_bias[l].reshape(SSD_HEADS, 1))

    o1, l1 = _attention_natural(q, k, v)
    o4, l4 = _attention_regrouped(qg, kg, vg, 4)
    o16, l16 = _attention_regrouped(qg, kg, vg, 16)

    yn = _ssd(xs, bm, cm, dt, dtT, zs, row(a_log[l]), a_log[l].reshape(SSD_HEADS, 1),
              row(jnp.repeat(d_skip[l], SSD_HEAD_DIM)), row(ssd_norm_w[l]),
              jnp.asarray(_weave_matrix(SSD_CHUNK).T, BF16))

    head_of_col = np.arange(ATT_WIDTH) // ATT_HEAD_DIM
    expand = jnp.asarray(np.arange(LANES)[:, None] == head_of_col[None, :], BF16)
    out = _out(x1, mod, o1, l1, o4, l4, o16, l16, yn, gates, expand, jnp.asarray(perm.T, BF16),
               w_att_out[l].astype(BF16), w_ssd_out[l].astype(BF16), w_mix_out[l].astype(BF16),
               row(norm3_w[l]), ffn2_w_gu[l].astype(BF16), ffn2_w_down[l].astype(BF16),
               row(final_norm_w))
    return out.reshape(BATCH, SEQ, D_MODEL)
```

```python
import jax
import jax.numpy as jnp
import numpy as np
from jax import lax
from jax.experimental import pallas as pl
from jax.experimental.pallas import tpu as pltpu

F32 = jnp.float32
BF16 = jnp.bfloat16

D_MODEL = 1024
BATCH = 2
SEQ = 8192
TOKENS = BATCH * SEQ
ATT_HEADS = 12
ATT_HEAD_DIM = 64
ATT_WIDTH = ATT_HEADS * ATT_HEAD_DIM
ROPE_DIM = ATT_HEAD_DIM // 4
ROPE_THETA = 500000.0
ATT_BLOCK = 128
SSD_INNER = 2 * D_MODEL
SSD_HEAD_DIM = 64
SSD_HEADS = SSD_INNER // SSD_HEAD_DIM
SSD_GROUPS = 8
SSD_STATE = 128
SSD_CONV = 4
SSD_CHUNK = 128
SSD_CONV_CH = SSD_INNER + 2 * SSD_GROUPS * SSD_STATE
D_FF = 2816
N_MOD = 9
NORM_EPS = 1e-6

LANES = 128
SUBLANES = 8
VMEM_LIMIT = 56 * 1024 * 1024
TOKEN_TILE = 256
COL_CHUNK = 256
PROJ_CHUNK = 512
REGROUP = 16
ATT_STEP_BLOCKS = 4
ADA_COL_TILE = 1536
NEG_BIG = -1e30

_NT = (((1,), (1,)), ((), ()))


def _params(*sem):
    return pltpu.CompilerParams(dimension_semantics=sem, vmem_limit_bytes=VMEM_LIMIT)


def _resident(shape):
    zeros = (0,) * len(shape)
    return pl.BlockSpec(shape, lambda *_: zeros, pipeline_mode=pl.Buffered(1))


def _sigmoid(x):
    return 0.5 * jnp.tanh(0.5 * x) + 0.5


def _silu(x):
    return x * _sigmoid(x)


def _rms(x, w):
    ms = jnp.sum(x * x, axis=-1, keepdims=True) * (1.0 / x.shape[-1])
    return x * lax.rsqrt(ms + NORM_EPS) * w


def _rms_mod(x, w, shift, scale):
    return _rms(x, w) * (1.0 + scale) + shift


def _swiglu(h, wgu_ref, wd_ref):
    gu = jnp.dot(h, wgu_ref[...], preferred_element_type=F32)
    a = (_silu(gu[:, :D_FF]) * gu[:, D_FF:]).astype(BF16)
    return jnp.dot(a, wd_ref[...], preferred_element_type=F32)


def _ada_kernel(ct_ref, w_ref, b_ref, o_ref):
    act = _silu(ct_ref[...])
    w = w_ref[...]
    rows = [jnp.sum(w * act[:, b:b + 1], axis=0, keepdims=True) for b in range(BATCH)]
    o_ref[...] = jnp.concatenate(rows, axis=0) + b_ref[...]


def _ada(c, w_ada, b_ada, layer):
    n = N_MOD * D_MODEL
    return pl.pallas_call(
        _ada_kernel,
        grid=(n // ADA_COL_TILE,),
        in_specs=[
            pl.BlockSpec((D_MODEL, BATCH), lambda j: (0, 0)),
            pl.BlockSpec((None, D_MODEL, ADA_COL_TILE), lambda j: (layer, 0, j)),
            pl.BlockSpec((None, 1, ADA_COL_TILE), lambda j: (layer, 0, j)),
        ],
        out_specs=pl.BlockSpec((BATCH, ADA_COL_TILE), lambda j: (0, j)),
        out_shape=jax.ShapeDtypeStruct((BATCH, n), F32),
        compiler_params=_params("arbitrary"),
        name="ada_mod",
    )(c.T, w_ada, b_ada.reshape(-1, 1, n))


def _ffn1_kernel(x_ref, mod_ref, nw_ref, wgu_ref, wd_ref, o_ref):
    x = x_ref[...]
    mod = mod_ref[0]
    h = _rms_mod(x, nw_ref[...], mod[0:1], mod[1:2]).astype(BF16)
    o_ref[...] = x + (0.5 * mod[2:3]) * _swiglu(h, wgu_ref, wd_ref)


def _tile_spec(width, tm=TOKEN_TILE):
    return pl.BlockSpec((tm, width), lambda i: (i, 0))


def _mod_spec(tm=TOKEN_TILE):
    per_batch = SEQ // tm
    return pl.BlockSpec((1, N_MOD, D_MODEL), lambda i: (i // per_batch, 0, 0))


def _ffn1(x, mod, nw, wgu, wd):
    return pl.pallas_call(
        _ffn1_kernel,
        grid=(TOKENS // TOKEN_TILE,),
        in_specs=[_tile_spec(D_MODEL), _mod_spec(), _resident((1, D_MODEL)),
                  _resident(wgu.shape), _resident(wd.shape)],
        out_specs=_tile_spec(D_MODEL),
        out_shape=jax.ShapeDtypeStruct((TOKENS, D_MODEL), F32),
        compiler_params=_params("arbitrary"),
        name="ffn1",
    )(x, mod, nw, wgu, wd)


def _softplus(x):
    return jnp.maximum(x, 0.0) + jnp.log1p(jnp.exp(-jnp.abs(x)))


def _inproj_kernel(*refs):
    *io_refs, raw_a_ref, raw_b_ref = refs
    hist_ref = io_refs[-3]
    step = pl.program_id(0)
    tiles_per_seq = SEQ // io_refs[0].shape[0]

    @pl.when(step == 0)
    def _():
        raw_b_ref[...] = jnp.zeros_like(raw_b_ref)

    @pl.when((step == 0) | (step % tiles_per_seq == 1))
    def _():
        hist_ref[...] = jnp.zeros_like(hist_ref)

    x_ref, mod_ref, nw_ref = io_refs[:3]
    weave_ref = io_refs[7]
    h_ref, hw_ref = io_refs[-5:-3]
    mod = mod_ref[0]
    h_ref[...] = _rms_mod(x_ref[...], nw_ref[...], mod[3:4], mod[4:5]).astype(BF16)
    hw_ref[...] = jnp.dot(weave_ref[...], h_ref[...], preferred_element_type=F32).astype(BF16)

    @pl.when(step % 2 == 0)
    def _():
        _inproj_step(*io_refs, raw_a_ref, raw_b_ref)

    @pl.when(step % 2 == 1)
    def _():
        _inproj_step(*io_refs, raw_b_ref, raw_a_ref)


def _inproj_step(x_ref, mod_ref, nw_ref, pos_ref, invf_ref, sgn_ref, perm_ref, weave_ref,
                 wqkv_ref, wz_ref, wxbc_ref, wdt_ref, wdtT_ref, wg_ref,
                 convw_ref, convb_ref, dtb_row_ref, dtb_col_ref,
                 qkv_ref, qkvg_ref, zs_ref, xbc_ref, dt_ref, dtT_ref, gates_ref,
                 h_ref, hw_ref, hist_ref, cos_ref, sin_ref, raw_out_ref, raw_in_ref):
    tm = x_ref.shape[0]

    def xbc_part(cols):
        raw_out_ref[:, cols] = jnp.dot(hw_ref[...], wxbc_ref[:, cols], preferred_element_type=F32)

    def rope_tables():
        ang = pos_ref[...] * invf_ref[...]
        cos_ref[...] = jnp.cos(ang)
        sin_ref[...] = jnp.sin(ang) * sgn_ref[...]

    def rope(t):
        reps = ATT_WIDTH // LANES
        cos = jnp.concatenate([cos_ref[...]] * reps, axis=1)
        sin = jnp.concatenate([sin_ref[...]] * reps, axis=1)
        lane = lax.broadcasted_iota(jnp.int32, (tm, ATT_WIDTH), 1)
        first_half = (lane & (ATT_HEAD_DIM - 1)) < (ROPE_DIM // 2)
        partner = jnp.where(first_half,
                            pltpu.roll(t, ATT_WIDTH - ROPE_DIM // 2, 1),
                            pltpu.roll(t, ROPE_DIM // 2, 1))
        return t * cos + partner * sin

    def qkv_part(idx):
        cols = slice(idx * ATT_WIDTH, (idx + 1) * ATT_WIDTH)
        t = jnp.dot(h_ref[...], wqkv_ref[:, cols], preferred_element_type=F32)
        if idx == 0:
            t = rope(t) * (ATT_HEAD_DIM ** -0.5)
        elif idx == 1:
            t = rope(t)
        t = t.astype(BF16)
        qkv_ref[:, cols] = t
        regrouped = jnp.dot(perm_ref[...], t, preferred_element_type=F32).astype(BF16)
        qkvg_ref[:, :, cols] = regrouped.reshape(qkvg_ref.shape[:2] + (ATT_WIDTH,))

    def z_part(cols):
        z = jnp.dot(hw_ref[...], wz_ref[:, cols], preferred_element_type=F32)
        zs_ref[:, cols] = _silu(z).astype(BF16)

    def gate_part(cols):
        g = jnp.dot(h_ref[...], wg_ref[:, cols], preferred_element_type=F32)
        gates_ref[:, cols] = _sigmoid(g).astype(BF16)

    def dt_part():
        hw = hw_ref[...]
        dt_raw = jnp.dot(hw, wdt_ref[...], preferred_element_type=F32)[:, :SSD_HEADS]
        dt_ref[...] = _softplus(dt_raw + dtb_row_ref[...])
        dtT_raw = lax.dot_general(wdtT_ref[...], hw, _NT, preferred_element_type=F32)
        dtT_ref[...] = _softplus(dtT_raw + dtb_col_ref[...])

    tail_rows = (SSD_CONV - 1) * SUBLANES
    tail_start = SSD_CHUNK - tail_rows
    first_sublane = lax.broadcasted_iota(jnp.int32, (SUBLANES, COL_CHUNK), 0) == 0

    def conv_chunk(src):
        raw = raw_in_ref[:, src]
        taps = [convw_ref[k:k + 1, src] for k in range(SSD_CONV)]
        prev_tail = hist_ref[:, src]
        pieces = []
        for ck in range(tm // SSD_CHUNK):
            cur = raw[ck * SSD_CHUNK:(ck + 1) * SSD_CHUNK]
            cur_tail = cur[tail_start:]
            wrapped = []
            for k in range(SSD_CONV - 1):
                rows = slice(k * SUBLANES, (k + 1) * SUBLANES)
                wrapped.append(jnp.where(first_sublane, pltpu.roll(prev_tail[rows], 1, 0),
                                         pltpu.roll(cur_tail[rows], 1, 0)))
            ext = jnp.concatenate(wrapped + [cur], axis=0)
            acc = cur * taps[SSD_CONV - 1] + convb_ref[:, src]
            for s in range(1, SSD_CONV):
                lo = (SSD_CONV - 1 - s) * SUBLANES
                acc = acc + ext[lo:lo + SSD_CHUNK] * taps[SSD_CONV - 1 - s]
            pieces.append(acc)
            prev_tail = cur_tail
        hist_ref[:, src] = prev_tail
        xbc_ref[:, src] = _silu(jnp.concatenate(pieces, axis=0)).astype(BF16)

    chunks = lambda width, w: [pl.ds(c * w, w) for c in range(width // w)]
    conv = chunks(SSD_CONV_CH, COL_CHUNK)
    parts = ([(xbc_part, (c,)) for c in chunks(SSD_CONV_CH, PROJ_CHUNK)]
             + [(z_part, (c,)) for c in chunks(SSD_INNER, PROJ_CHUNK)]
             + [(gate_part, (c,)) for c in chunks(2 * D_MODEL, PROJ_CHUNK)]
             + [(qkv_part, (2,)), (rope_tables, ()), (qkv_part, (0,)), (qkv_part, (1,)),
                (dt_part, ())])
    for i, (fn, args) in enumerate(parts):
        fn(*args)
        if i < len(conv):
            conv_chunk(conv[i])


def _inproj(x1, mod, nw, posf, invf, sgn, perm, weave, wqkv, wz, wxbc, wdt, wdtT, wg,
            conv_w, conv_b, dtb_row, dtb_col):
    tm = TOKEN_TILE
    tiles = SEQ // tm
    n_tiles = TOKENS // tm
    cur = lambda i: jnp.minimum(i, n_tiles - 1)
    done = lambda i: jnp.maximum(i - 1, 0)
    tok = lambda w, d, at=cur: (jax.ShapeDtypeStruct((TOKENS, w), d),
                                pl.BlockSpec((tm, w), lambda i: (at(i), 0)))
    grouped = (jax.ShapeDtypeStruct((BATCH, tiles, REGROUP, tm // REGROUP, 3 * ATT_WIDTH), BF16),
               pl.BlockSpec((None, None, REGROUP, tm // REGROUP, 3 * ATT_WIDTH),
                            lambda i: (cur(i) // tiles, cur(i) % tiles, 0, 0, 0)))
    outs = [tok(3 * ATT_WIDTH, BF16), grouped,
        tok(SSD_INNER, BF16),
        tok(SSD_CONV_CH, BF16, done),
        tok(SSD_HEADS, F32),
        (jax.ShapeDtypeStruct((SSD_HEADS, TOKENS), F32),
         pl.BlockSpec((SSD_HEADS, tm), lambda i: (0, cur(i)))),
        tok(2 * D_MODEL, BF16),
    ]
    return pl.pallas_call(
        _inproj_kernel,
        grid=(n_tiles + 1,),
        in_specs=[tok(D_MODEL, F32)[1],
                  pl.BlockSpec((1, N_MOD, D_MODEL), lambda i: (cur(i) // tiles, 0, 0)),
                  _resident((1, D_MODEL)),
                  tok(LANES, F32)[1], _resident((1, LANES)), _resident((1, LANES)),
                  _resident(perm.shape), _resident(weave.shape),
                  _resident(wqkv.shape), _resident(wz.shape), _resident(wxbc.shape),
                  _resident(wdt.shape), _resident(wdtT.shape), _resident(wg.shape),
                  _resident(conv_w.shape), _resident(conv_b.shape),
                  _resident(dtb_row.shape), _resident(dtb_col.shape)],
        out_specs=[s for _, s in outs],
        out_shape=[s for s, _ in outs],
        scratch_shapes=[pltpu.VMEM((tm, D_MODEL), BF16), pltpu.VMEM((tm, D_MODEL), BF16),
                        pltpu.VMEM(((SSD_CONV - 1) * SUBLANES, SSD_CONV_CH), F32),
                        pltpu.VMEM((tm, LANES), F32), pltpu.VMEM((tm, LANES), F32),
                        pltpu.VMEM((tm, SSD_CONV_CH), F32), pltpu.VMEM((tm, SSD_CONV_CH), F32)],
        compiler_params=_params("arbitrary"),
        name="in_proj",
    )(x1, mod, nw, posf, invf, sgn, perm, weave, wqkv, wz, wxbc, wdt, wdtT, wg,
      conv_w, conv_b, dtb_row, dtb_col)


def _band_bias(row_pos):
    blk = ATT_BLOCK
    pos = row_pos(np.arange(blk))
    k_pos = np.concatenate([pos, pos + blk])
    dist = pos[:, None] + blk - k_pos[None, :]
    return jnp.asarray(np.where((dist >= 0) & (dist <= blk), 0.0, NEG_BIG), F32)


def _attn_kernel(band_ref, q_ref, kp_ref, kc_ref, vp_ref, vc_ref, o_ref, lse_ref):
    blk = ATT_BLOCK
    has_prev = pl.program_id(2) > 0
    flat = lambda ref: ref[...].reshape(-1, ref.shape[-1])
    q = flat(q_ref)
    k = jnp.concatenate([flat(kp_ref), flat(kc_ref)], axis=0)
    v = jnp.concatenate([flat(vp_ref), flat(vc_ref)], axis=0)
    band = band_ref[...]
    ki = lax.broadcasted_iota(jnp.int32, (blk, 2 * blk), 1)
    first_bias = jnp.where((ki >= blk) | has_prev, band, NEG_BIG)
    lane_row = lax.broadcasted_iota(jnp.int32, (1, LANES), 1)
    lane = lax.broadcasted_iota(jnp.int32, (blk, LANES), 1)
    low = lane < ATT_HEAD_DIM
    head_mask = [(lane_row < ATT_HEAD_DIM).astype(BF16), (lane_row >= ATT_HEAD_DIM).astype(BF16)]
    o_blocks, lse_blocks = [], []
    for j in range(ATT_STEP_BLOCKS):
        bias = first_bias if j == 0 else band
        q_rows = slice(j * blk, (j + 1) * blk)
        k_rows = slice(j * blk, (j + 2) * blk)
        lse_tile = jnp.zeros((blk, LANES), F32)
        o_pairs = []
        for pair in range(ATT_HEADS // 2):
            cols = slice(pair * LANES, (pair + 1) * LANES)
            qp, kpair, vpair = q[q_rows, cols], k[k_rows, cols], v[k_rows, cols]
            outs = []
            for half in range(2):
                s = lax.dot_general(qp * head_mask[half], kpair, _NT,
                                    preferred_element_type=F32) + bias
                m = jnp.max(s, axis=1, keepdims=True)
                p = jnp.exp(s - m)
                den = jnp.sum(p, axis=1, keepdims=True)
                o = jnp.dot(p.astype(BF16), vpair, preferred_element_type=F32)
                outs.append(o * (1.0 / den))
                lse_tile = jnp.where(lane == 2 * pair + half, m + jnp.log(den), lse_tile)
            o_pairs.append(jnp.where(low, outs[0], outs[1]).astype(BF16))
        o_blocks.append(jnp.concatenate(o_pairs, axis=1))
        lse_blocks.append(lse_tile)
    o_ref[...] = jnp.concatenate(o_blocks, axis=0).reshape(o_ref.shape)
    lse_ref[...] = jnp.concatenate(lse_blocks, axis=0).reshape(lse_ref.shape)


def _attention_natural(qkv):
    step = ATT_STEP_BLOCKS * ATT_BLOCK
    n_blk = SEQ // step
    qkv = qkv.reshape(BATCH, SEQ, 3 * ATT_WIDTH)
    cur = lambda part: pl.BlockSpec((None, step, ATT_WIDTH), lambda b, j, i: (b, i, part))
    prev = lambda part: pl.BlockSpec(
        (None, ATT_BLOCK, ATT_WIDTH),
        lambda b, j, i: (b, jnp.maximum(ATT_STEP_BLOCKS * i - 1, 0), part))
    o, lse = pl.pallas_call(
        _attn_kernel,
        grid=(BATCH, 1, n_blk),
        in_specs=[_resident((ATT_BLOCK, 2 * ATT_BLOCK)), cur(0), prev(1), cur(1), prev(2), cur(2)],
        out_specs=[cur(0), pl.BlockSpec((None, step, LANES), lambda b, j, i: (b, i, 0))],
        out_shape=[jax.ShapeDtypeStruct((BATCH, SEQ, ATT_WIDTH), BF16),
                   jax.ShapeDtypeStruct((BATCH, SEQ, LANES), F32)],
        compiler_params=_params("arbitrary", "arbitrary", "arbitrary"),
        name="attn_dil1",
    )(_band_bias(lambda r: r), qkv, qkv, qkv, qkv, qkv)
    return o.reshape(TOKENS, ATT_WIDTH), lse.reshape(TOKENS, LANES)


def _attention_regrouped(qkvg, r):
    tiles, rows = qkvg.shape[1], qkvg.shape[3]
    fold = REGROUP // r
    tiles_per_blk = ATT_BLOCK // (fold * rows)
    n_blk = tiles // (ATT_STEP_BLOCKS * tiles_per_blk)
    shape6 = lambda w: (BATCH, tiles, fold, r, rows, w)
    qkvg = qkvg.reshape(shape6(3 * ATT_WIDTH))
    blk6 = lambda n_tiles, w: (None, n_tiles, fold, None, rows, w)
    cur = lambda w, part=0: pl.BlockSpec(blk6(ATT_STEP_BLOCKS * tiles_per_blk, w),
                                         lambda b, j, i: (b, i, 0, j, 0, part))
    prev = lambda part: pl.BlockSpec(
        blk6(tiles_per_blk, ATT_WIDTH),
        lambda b, j, i: (b, jnp.maximum(ATT_STEP_BLOCKS * i - 1, 0), 0, j, 0, part))
    per_tile = fold * rows

    def row_pos(rho):
        t = rho // per_tile
        a = (rho // rows) % fold
        return t * per_tile + (rho % rows) * fold + a

    o, lse = pl.pallas_call(
        _attn_kernel,
        grid=(BATCH, r, n_blk),
        in_specs=[_resident((ATT_BLOCK, 2 * ATT_BLOCK)), cur(ATT_WIDTH, 0),
                  prev(1), cur(ATT_WIDTH, 1), prev(2), cur(ATT_WIDTH, 2)],
        out_specs=[cur(ATT_WIDTH), cur(LANES)],
        out_shape=[jax.ShapeDtypeStruct(shape6(ATT_WIDTH), BF16),
                   jax.ShapeDtypeStruct(shape6(LANES), F32)],
        compiler_params=_params("arbitrary", "arbitrary", "arbitrary"),
        name=f"attn_dil{r}",
    )(_band_bias(row_pos), qkvg, qkvg, qkvg, qkvg, qkvg)
    grouped5 = (BATCH, tiles, REGROUP, rows)
    return o.reshape(grouped5 + (ATT_WIDTH,)), lse.reshape(grouped5 + (LANES,))


def _woven_time(i):
    return ((i & (SUBLANES - 1)) << 4) | (i >> 3)


def _ssd_kernel(xbc_ref, dt_ref, dtT_ref, zs_ref,
                alog_row_ref, alog_col_ref, dskip_ref, nw_ref, unweave_ref,
                o_ref, state_ref, y_ref):
    q = SSD_CHUNK
    heads = SSD_HEADS // SSD_GROUPS
    gw = heads * SSD_HEAD_DIM

    @pl.when(pl.program_id(1) == 0)
    def _():
        state_ref[...] = jnp.zeros_like(state_ref)

    dtT = dtT_ref[...]
    a_dt_col = dt_ref[...] * (-jnp.exp(alog_row_ref[...]))
    a_dt_row = dtT * (-jnp.exp(alog_col_ref[...]))
    t_row = _woven_time(lax.broadcasted_iota(jnp.int32, (q, q), 0))
    t_col = _woven_time(lax.broadcasted_iota(jnp.int32, (q, q), 1))
    causal = t_row >= t_col
    hi = lax.Precision.HIGHEST
    cs_col = jnp.dot(causal.astype(F32), a_dt_col, precision=hi, preferred_element_type=F32)
    cs_row = jnp.dot(a_dt_row, (t_row <= t_col).astype(F32), precision=hi,
                     preferred_element_type=F32)
    total = cs_row[:, q - 1:q]
    dt_decay_row = cs_row - jnp.log(dtT)
    to_end = dtT * jnp.exp(total - cs_row)
    chunk_decay = jnp.exp(total)
    r2 = lax.broadcasted_iota(jnp.int32, (gw, gw), 0)
    c2 = lax.broadcasted_iota(jnp.int32, (gw, gw), 1)
    eye = (r2 == c2).astype(BF16)
    no_rows = jnp.zeros((SSD_HEAD_DIM, q + SSD_STATE), BF16)
    sq_sum = jnp.zeros((q, LANES), F32)

    for g in range(SSD_GROUPS):
        b_at = SSD_INNER + g * SSD_STATE
        c_at = b_at + SSD_GROUPS * SSD_STATE
        b_g = xbc_ref[:, b_at:b_at + SSD_STATE]
        c_g = xbc_ref[:, c_at:c_at + SSD_STATE]
        c_f = c_g.astype(F32)
        cb = lax.dot_general(c_g, b_g, _NT, preferred_element_type=F32)
        x_g = xbc_ref[:, g * gw:(g + 1) * gw]
        state = state_ref[g]
        x_t = lax.dot_general(eye, x_g, _NT, preferred_element_type=F32)
        rhs = jnp.concatenate([x_t.astype(BF16), state.astype(BF16)], axis=1)
        w_parts, rhs_rows, end_rows, decay_rows = [], [], [], []
        for j in range(heads):
            h = g * heads + j
            cs_l = jnp.broadcast_to(cs_col[:, h:h + 1], (q, q))
            w_parts.append(jnp.where(causal, cb * jnp.exp(cs_l - dt_decay_row[h:h + 1]), 0.0)
                           .astype(BF16))
            w_parts.append((c_f * jnp.exp(cs_l)).astype(BF16))
            mine = rhs[j * SSD_HEAD_DIM:(j + 1) * SSD_HEAD_DIM]
            rhs_rows.append(jnp.concatenate([mine if jj == j else no_rows for jj in range(heads)],
                                            axis=1))
            end_rows.append(jnp.broadcast_to(to_end[h:h + 1], (SSD_HEAD_DIM, q)))
            decay_rows.append(jnp.broadcast_to(chunk_decay[h:h + 1], (SSD_HEAD_DIM, SSD_STATE)))
        y = lax.dot_general(jnp.concatenate(w_parts, axis=1), jnp.concatenate(rhs_rows, axis=0),
                            _NT, preferred_element_type=F32)
        xcols = slice(g * gw, (g + 1) * gw)
        y = (y + dskip_ref[:, xcols] * x_g.astype(F32)) * zs_ref[:, xcols].astype(F32)
        y_ref[:, xcols] = y
        sq = y * y
        sq_sum = sq_sum + sq[:, :LANES] + sq[:, LANES:]
        upd = jnp.dot((x_t * jnp.concatenate(end_rows, axis=0)).astype(BF16), b_g,
                      preferred_element_type=F32)
        state_ref[g] = state * jnp.concatenate(decay_rows, axis=0) + upd

    ms = jnp.sum(sq_sum, axis=-1, keepdims=True) * (1.0 / SSD_INNER)
    y = (y_ref[...] * lax.rsqrt(ms + NORM_EPS) * nw_ref[...]).astype(BF16)
    o_ref[...] = jnp.dot(unweave_ref[...], y, preferred_element_type=F32).astype(BF16)


def _ssd(xbc, dt, dtT, zs, alog_row, alog_col, dskip, nw, unweave):
    n_chunks = SEQ // SSD_CHUNK
    tok = lambda w: pl.BlockSpec((SSD_CHUNK, w), lambda b, c: (b * n_chunks + c, 0))
    gw = SSD_INNER // SSD_GROUPS
    return pl.pallas_call(
        _ssd_kernel,
        grid=(BATCH, n_chunks),
        in_specs=[tok(SSD_CONV_CH), tok(SSD_HEADS),
                  pl.BlockSpec((SSD_HEADS, SSD_CHUNK), lambda b, c: (0, b * n_chunks + c)),
                  tok(SSD_INNER),
                  _resident((1, SSD_HEADS)), _resident((SSD_HEADS, 1)),
                  _resident((1, SSD_INNER)), _resident((1, SSD_INNER)),
                  _resident(unweave.shape)],
        out_specs=tok(SSD_INNER),
        out_shape=jax.ShapeDtypeStruct((TOKENS, SSD_INNER), BF16),
        scratch_shapes=[pltpu.VMEM((SSD_GROUPS, gw, SSD_STATE), F32),
                        pltpu.VMEM((SSD_CHUNK, SSD_INNER), F32)],
        compiler_params=_params("arbitrary", "arbitrary"),
        name="ssd_scan",
    )(xbc, dt, dtT, zs, alog_row, alog_col, dskip, nw, unweave)


def _out_kernel(x_ref, mod_ref, o1_ref, l1_ref, o4_ref, l4_ref, o16_ref, l16_ref,
                yn_ref, gates_ref, expand_ref, restore_ref, watt_ref, wssd_ref, wmix_ref,
                n3w_ref, wgu_ref, wd_ref, fnw_ref, out_ref):
    tm = x_ref.shape[0]
    mod = mod_ref[0]
    expand = lambda a: jnp.dot(a.astype(BF16), expand_ref[...], preferred_element_type=F32)
    rows = lambda ref: ref[...].reshape(tm, ref.shape[-1])

    def mix(lse_a, o_a, lse_b, o_b):
        m = jnp.maximum(lse_a, lse_b)
        e_a, e_b = jnp.exp(lse_a - m), jnp.exp(lse_b - m)
        tot = e_a + e_b
        inv = 1.0 / tot
        return m + jnp.log(tot), expand(e_a * inv) * o_a + expand(e_b * inv) * o_b

    lse_g, o_g = mix(rows(l4_ref), rows(o4_ref).astype(F32), rows(l16_ref), rows(o16_ref).astype(F32))
    restore = restore_ref[...]
    o_g = jnp.dot(restore, o_g.astype(BF16), preferred_element_type=F32)
    lse_rest, lse_g_tok = lse_g, None
    for _ in range(3):
        piece = lse_rest.astype(BF16)
        moved = jnp.dot(restore, piece, preferred_element_type=F32)
        lse_g_tok = moved if lse_g_tok is None else lse_g_tok + moved
        lse_rest = lse_rest - piece.astype(F32)
    _, o_att = mix(l1_ref[...], o1_ref[...].astype(F32), lse_g_tok, o_g)

    y_att = jnp.dot(o_att.astype(BF16), watt_ref[...], preferred_element_type=F32)
    y_ssd = jnp.dot(yn_ref[...], wssd_ref[...], preferred_element_type=F32)
    gates = gates_ref[...].astype(F32)
    merged = gates[:, :D_MODEL] * y_att + gates[:, D_MODEL:] * y_ssd
    y = jnp.dot(merged.astype(BF16), wmix_ref[...], preferred_element_type=F32)
    x2 = x_ref[...] + mod[5:6] * y
    h = _rms_mod(x2, n3w_ref[...], mod[6:7], mod[7:8]).astype(BF16)
    x3 = x2 + (0.5 * mod[8:9]) * _swiglu(h, wgu_ref, wd_ref)
    out_ref[...] = _rms(x3, fnw_ref[...])


def _out(x1, mod, o1, l1, o4, l4, o16, l16, yn, gates, expand, restore,
         watt, wssd, wmix, n3w, wgu, wd, fnw):
    tm = TOKEN_TILE
    tiles = SEQ // tm
    grouped = lambda w: pl.BlockSpec((None, None, REGROUP, tm // REGROUP, w),
                                     lambda i: (i // tiles, i % tiles, 0, 0, 0))
    return pl.pallas_call(
        _out_kernel,
        grid=(TOKENS // tm,),
        in_specs=[_tile_spec(D_MODEL), _mod_spec(), _tile_spec(ATT_WIDTH), _tile_spec(LANES),
                  grouped(ATT_WIDTH), grouped(LANES), grouped(ATT_WIDTH), grouped(LANES),
                  _tile_spec(SSD_INNER), _tile_spec(2 * D_MODEL)]
                 + [_resident(a.shape) for a in (expand, restore, watt, wssd, wmix, n3w, wgu, wd, fnw)],
        out_specs=_tile_spec(D_MODEL),
        out_shape=jax.ShapeDtypeStruct((TOKENS, D_MODEL), F32),
        compiler_params=_params("arbitrary"),
        name="mix_out_ffn2",
    )(x1, mod, o1, l1, o4, l4, o16, l16, yn, gates, expand, restore,
      watt, wssd, wmix, n3w, wgu, wd, fnw)


def _rope_tables():
    inv_freq = ROPE_THETA ** (-jnp.arange(0, ROPE_DIM, 2, dtype=F32) / ROPE_DIM)
    d = np.arange(LANES) % ATT_HEAD_DIM
    half = ROPE_DIM // 2
    invf = jnp.where(d < ROPE_DIM, inv_freq[d % half], 0.0).astype(F32).reshape(1, LANES)
    sgn = np.where(d < half, -1.0, np.where(d < ROPE_DIM, 1.0, 0.0)).astype(np.float32)
    return invf, jnp.asarray(sgn).reshape(1, LANES)


def _regroup_matrix(tm):
    rho = np.arange(tm)
    src = REGROUP * (rho % (tm // REGROUP)) + rho // (tm // REGROUP)
    return (np.arange(tm)[None, :] == src[:, None]).astype(np.float32)


def _weave_matrix(tm):
    rho = np.arange(tm)
    src = (rho // SSD_CHUNK) * SSD_CHUNK + _woven_time(rho % SSD_CHUNK)
    return (np.arange(tm)[None, :] == src[:, None]).astype(np.float32)


def kernel(x, c, positions, w_ada, b_ada, norm1_w, ffn1_w_gu, ffn1_w_down, norm2_w, w_in,
           conv_w, conv_b, dt_bias, a_log, d_skip, ssd_norm_w, w_att_out, w_ssd_out,
           w_mix_out, norm3_w, ffn2_w_gu, ffn2_w_down, final_norm_w):
    l = 0
    row = lambda t: t.reshape(1, -1).astype(F32)
    xf = x.reshape(TOKENS, D_MODEL)
    mod = _ada(c, w_ada, b_ada, l).reshape(BATCH, N_MOD, D_MODEL)

    x1 = _ffn1(xf, mod, row(norm1_w[l]), ffn1_w_gu[l].astype(BF16), ffn1_w_down[l].astype(BF16))

    o_z = 3 * ATT_WIDTH
    o_xbc = o_z + SSD_INNER
    o_dt = o_xbc + SSD_CONV_CH
    o_g = o_dt + SSD_HEADS
    w = w_in[l]
    w_dt = w[:, o_dt:o_g]
    invf, sgn = _rope_tables()
    perm = _regroup_matrix(TOKEN_TILE)
    posf = jnp.broadcast_to(positions.reshape(TOKENS, 1).astype(F32), (TOKENS, LANES))
    qkv, qkvg, zs, xbc, dt, dtT, gates = _inproj(
        x1, mod, row(norm2_w[l]), posf, invf, sgn, jnp.asarray(perm, BF16),
        jnp.asarray(_weave_matrix(TOKEN_TILE), BF16),
        w[:, :o_z].astype(BF16), w[:, o_z:o_xbc].astype(BF16), w[:, o_xbc:o_dt].astype(BF16),
        jnp.pad(w_dt, ((0, 0), (0, LANES - SSD_HEADS))).astype(BF16), w_dt.T.astype(BF16),
        w[:, o_g:].astype(BF16),
        conv_w[l], row(conv_b[l]), row(dt_bias[l]), dt_bias[l].reshape(SSD_HEADS, 1))

    o1, l1 = _attention_natural(qkv)
    o4, l4 = _attention_regrouped(qkvg, 4)
    o16, l16 = _attention_regrouped(qkvg, 16)

    yn = _ssd(xbc, dt, dtT, zs, row(a_log[l]), a_log[l].reshape(SSD_HEADS, 1),
              row(jnp.repeat(d_skip[l], SSD_HEAD_DIM)), row(ssd_norm_w[l]),
              jnp.asarray(_weave_matrix(SSD_CHUNK).T, BF16))

    head_of_col = np.arange(ATT_WIDTH) // ATT_HEAD_DIM
    expand = jnp.asarray(np.arange(LANES)[:, None] == head_of_col[None, :], BF16)
    out = _out(x1, mod, o1, l1, o4, l4, o16, l16, yn, gates, expand, jnp.asarray(perm.T, BF16),
               w_att_out[l].astype(BF16), w_ssd_out[l].astype(BF16), w_mix_out[l].astype(BF16),
               row(norm3_w[l]), ffn2_w_gu[l].astype(BF16), ffn2_w_down[l].astype(BF16),
               row(final_norm_w))
    return out.reshape(BATCH, SEQ, D_MODEL)
```

```python
import jax
import jax.numpy as jnp
import numpy as np
from jax import lax
from jax.experimental import pallas as pl
from jax.experimental.pallas import tpu as pltpu

F32 = jnp.float32
BF16 = jnp.bfloat16

D_MODEL = 1024
BATCH = 2
SEQ = 8192
TOKENS = BATCH * SEQ
ATT_HEADS = 12
ATT_HEAD_DIM = 64
ATT_WIDTH = ATT_HEADS * ATT_HEAD_DIM
ROPE_DIM = ATT_HEAD_DIM // 4
ROPE_THETA = 500000.0
ATT_BLOCK = 128
SSD_INNER = 2 * D_MODEL
SSD_HEAD_DIM = 64
SSD_HEADS = SSD_INNER // SSD_HEAD_DIM
SSD_GROUPS = 8
SSD_STATE = 128
SSD_CONV = 4
SSD_CHUNK = 128
SSD_CONV_CH = SSD_INNER + 2 * SSD_GROUPS * SSD_STATE
D_FF = 2816
N_MOD = 9
NORM_EPS = 1e-6

LANES = 128
SUBLANES = 8
VMEM_LIMIT = 56 * 1024 * 1024
TOKEN_TILE = 256
COL_CHUNK = 256
PROJ_CHUNK = 512
REGROUP = 16
ATT_STEP_BLOCKS = 4
ADA_COL_TILE = 1536
NEG_BIG = -1e30

_NT = (((1,), (1,)), ((), ()))


def _params(*sem):
    return pltpu.CompilerParams(dimension_semantics=sem, vmem_limit_bytes=VMEM_LIMIT)


def _resident(shape):
    zeros = (0,) * len(shape)
    return pl.BlockSpec(shape, lambda *_: zeros, pipeline_mode=pl.Buffered(1))


def _sigmoid(x):
    return 0.5 * jnp.tanh(0.5 * x) + 0.5


def _silu(x):
    return x * _sigmoid(x)


def _rms(x, w):
    ms = jnp.sum(x * x, axis=-1, keepdims=True) * (1.0 / x.shape[-1])
    return x * lax.rsqrt(ms + NORM_EPS) * w


def _rms_mod(x, w, shift, scale):
    return _rms(x, w) * (1.0 + scale) + shift


def _swiglu(h, wgu_ref, wd_ref):
    gu = jnp.dot(h, wgu_ref[...], preferred_element_type=F32)
    a = (_silu(gu[:, :D_FF]) * gu[:, D_FF:]).astype(BF16)
    return jnp.dot(a, wd_ref[...], preferred_element_type=F32)


def _ada_kernel(ct_ref, w_ref, b_ref, o_ref):
    act = _silu(ct_ref[...])
    w = w_ref[...]
    rows = [jnp.sum(w * act[:, b:b + 1], axis=0, keepdims=True) for b in range(BATCH)]
    o_ref[...] = jnp.concatenate(rows, axis=0) + b_ref[...]


def _ada(c, w_ada, b_ada, layer):
    n = N_MOD * D_MODEL
    return pl.pallas_call(
        _ada_kernel,
        grid=(n // ADA_COL_TILE,),
        in_specs=[
            pl.BlockSpec((D_MODEL, BATCH), lambda j: (0, 0)),
            pl.BlockSpec((None, D_MODEL, ADA_COL_TILE), lambda j: (layer, 0, j)),
            pl.BlockSpec((None, 1, ADA_COL_TILE), lambda j: (layer, 0, j)),
        ],
        out_specs=pl.BlockSpec((BATCH, ADA_COL_TILE), lambda j: (0, j)),
        out_shape=jax.ShapeDtypeStruct((BATCH, n), F32),
        compiler_params=_params("arbitrary"),
        name="ada_mod",
    )(c.T, w_ada, b_ada.reshape(-1, 1, n))


def _ffn1_kernel(x_ref, mod_ref, nw_ref, wgu_ref, wd_ref, o_ref):
    x = x_ref[...]
    mod = mod_ref[0]
    h = _rms_mod(x, nw_ref[...], mod[0:1], mod[1:2]).astype(BF16)
    o_ref[...] = x + (0.5 * mod[2:3]) * _swiglu(h, wgu_ref, wd_ref)


def _tile_spec(width, tm=TOKEN_TILE):
    return pl.BlockSpec((tm, width), lambda i: (i, 0))


def _mod_spec(tm=TOKEN_TILE):
    per_batch = SEQ // tm
    return pl.BlockSpec((1, N_MOD, D_MODEL), lambda i: (i // per_batch, 0, 0))


def _ffn1(x, mod, nw, wgu, wd):
    return pl.pallas_call(
        _ffn1_kernel,
        grid=(TOKENS // TOKEN_TILE,),
        in_specs=[_tile_spec(D_MODEL), _mod_spec(), _resident((1, D_MODEL)),
                  _resident(wgu.shape), _resident(wd.shape)],
        out_specs=_tile_spec(D_MODEL),
        out_shape=jax.ShapeDtypeStruct((TOKENS, D_MODEL), F32),
        compiler_params=_params("arbitrary"),
        name="ffn1",
    )(x, mod, nw, wgu, wd)


def _softplus(x):
    return jnp.maximum(x, 0.0) + jnp.log1p(jnp.exp(-jnp.abs(x)))


def _inproj_kernel(*refs):
    *io_refs, raw_a_ref, raw_b_ref = refs
    hist_ref = io_refs[-3]
    step = pl.program_id(0)
    tiles_per_seq = SEQ // io_refs[0].shape[0]

    @pl.when(step == 0)
    def _():
        raw_b_ref[...] = jnp.zeros_like(raw_b_ref)

    @pl.when((step == 0) | (step % tiles_per_seq == 1))
    def _():
        hist_ref[...] = jnp.zeros_like(hist_ref)

    x_ref, mod_ref, nw_ref = io_refs[:3]
    weave_ref = io_refs[7]
    h_ref, hw_ref = io_refs[-5:-3]
    mod = mod_ref[0]
    h_ref[...] = _rms_mod(x_ref[...], nw_ref[...], mod[3:4], mod[4:5]).astype(BF16)
    hw_ref[...] = jnp.dot(weave_ref[...], h_ref[...], preferred_element_type=F32).astype(BF16)

    @pl.when(step % 2 == 0)
    def _():
        _inproj_step(*io_refs, raw_a_ref, raw_b_ref)

    @pl.when(step % 2 == 1)
    def _():
        _inproj_step(*io_refs, raw_b_ref, raw_a_ref)


def _inproj_step(x_ref, mod_ref, nw_ref, pos_ref, invf_ref, sgn_ref, perm_ref, weave_ref,
                 wqkv_ref, wz_ref, wxbc_ref, wdt_ref, wdtT_ref,
                 convw_ref, convb_ref, dtb_row_ref, dtb_col_ref,
                 qkv_ref, qkvg_ref, zs_ref, xbc_ref, dt_ref, dtT_ref,
                 h_ref, hw_ref, hist_ref, cos_ref, sin_ref, raw_out_ref, raw_in_ref):
    tm = x_ref.shape[0]

    def xbc_part(cols):
        raw_out_ref[:, cols] = jnp.dot(hw_ref[...], wxbc_ref[:, cols], preferred_element_type=F32)

    def rope_tables():
        ang = pos_ref[...] * invf_ref[...]
        cos_ref[...] = jnp.cos(ang)
        sin_ref[...] = jnp.sin(ang) * sgn_ref[...]

    def rope(t):
        reps = ATT_WIDTH // LANES
        cos = jnp.concatenate([cos_ref[...]] * reps, axis=1)
        sin = jnp.concatenate([sin_ref[...]] * reps, axis=1)
        lane = lax.broadcasted_iota(jnp.int32, (tm, ATT_WIDTH), 1)
        first_half = (lane & (ATT_HEAD_DIM - 1)) < (ROPE_DIM // 2)
        partner = jnp.where(first_half,
                            pltpu.roll(t, ATT_WIDTH - ROPE_DIM // 2, 1),
                            pltpu.roll(t, ROPE_DIM // 2, 1))
        return t * cos + partner * sin

    def qkv_part(idx):
        cols = slice(idx * ATT_WIDTH, (idx + 1) * ATT_WIDTH)
        t = jnp.dot(h_ref[...], wqkv_ref[:, cols], preferred_element_type=F32)
        if idx == 0:
            t = rope(t) * (ATT_HEAD_DIM ** -0.5)
        elif idx == 1:
            t = rope(t)
        t = t.astype(BF16)
        qkv_ref[:, cols] = t
        regrouped = jnp.dot(perm_ref[...], t, preferred_element_type=F32).astype(BF16)
        qkvg_ref[:, :, cols] = regrouped.reshape(qkvg_ref.shape[:2] + (ATT_WIDTH,))

    def z_part(cols):
        z = jnp.dot(hw_ref[...], wz_ref[:, cols], preferred_element_type=F32)
        zs_ref[:, cols] = _silu(z).astype(BF16)

    def dt_part():
        hw = hw_ref[...]
        dt_raw = jnp.dot(hw, wdt_ref[...], preferred_element_type=F32)[:, :SSD_HEADS]
        dt_ref[...] = _softplus(dt_raw + dtb_row_ref[...])
        dtT_raw = lax.dot_general(wdtT_ref[...], hw, _NT, preferred_element_type=F32)
        dtT_ref[...] = _softplus(dtT_raw + dtb_col_ref[...])

    tail_rows = (SSD_CONV - 1) * SUBLANES
    tail_start = SSD_CHUNK - tail_rows
    first_sublane = lax.broadcasted_iota(jnp.int32, (SUBLANES, COL_CHUNK), 0) == 0

    def conv_chunk(src):
        raw = raw_in_ref[:, src]
        taps = [convw_ref[k:k + 1, src] for k in range(SSD_CONV)]
        prev_tail = hist_ref[:, src]
        pieces = []
        for ck in range(tm // SSD_CHUNK):
            cur = raw[ck * SSD_CHUNK:(ck + 1) * SSD_CHUNK]
            cur_tail = cur[tail_start:]
            wrapped = []
            for k in range(SSD_CONV - 1):
                rows = slice(k * SUBLANES, (k + 1) * SUBLANES)
                wrapped.append(jnp.where(first_sublane, pltpu.roll(prev_tail[rows], 1, 0),
                                         pltpu.roll(cur_tail[rows], 1, 0)))
            ext = jnp.concatenate(wrapped + [cur], axis=0)
            acc = cur * taps[SSD_CONV - 1] + convb_ref[:, src]
            for s in range(1, SSD_CONV):
                lo = (SSD_CONV - 1 - s) * SUBLANES
                acc = acc + ext[lo:lo + SSD_CHUNK] * taps[SSD_CONV - 1 - s]
            pieces.append(acc)
            prev_tail = cur_tail
        hist_ref[:, src] = prev_tail
        xbc_ref[:, src] = _silu(jnp.concatenate(pieces, axis=0)).astype(BF16)

    chunks = lambda width, w: [pl.ds(c * w, w) for c in range(width // w)]
    conv = chunks(SSD_CONV_CH, COL_CHUNK)
    parts = ([(xbc_part, (c,)) for c in chunks(SSD_CONV_CH, PROJ_CHUNK)]
             + [(z_part, (c,)) for c in chunks(SSD_INNER, PROJ_CHUNK)]
             + [(qkv_part, (2,)), (rope_tables, ()), (qkv_part, (0,)), (qkv_part, (1,)),
                (dt_part, ())])
    for i, (fn, args) in enumerate(parts):
        fn(*args)
        if i < len(conv):
            conv_chunk(conv[i])
    assert len(conv) <= len(parts)


def _inproj(x1, mod, nw, posf, invf, sgn, perm, weave, wqkv, wz, wxbc, wdt, wdtT,
            conv_w, conv_b, dtb_row, dtb_col):
    tm = TOKEN_TILE
    tiles = SEQ // tm
    n_tiles = TOKENS // tm
    cur = lambda i: jnp.minimum(i, n_tiles - 1)
    done = lambda i: jnp.maximum(i - 1, 0)
    tok = lambda w, d, at=cur: (jax.ShapeDtypeStruct((TOKENS, w), d),
                                pl.BlockSpec((tm, w), lambda i: (at(i), 0)))
    grouped = (jax.ShapeDtypeStruct((BATCH, tiles, REGROUP, tm // REGROUP, 3 * ATT_WIDTH), BF16),
               pl.BlockSpec((None, None, REGROUP, tm // REGROUP, 3 * ATT_WIDTH),
                            lambda i: (cur(i) // tiles, cur(i) % tiles, 0, 0, 0)))
    outs = [tok(3 * ATT_WIDTH, BF16), grouped,
        tok(SSD_INNER, BF16),
        tok(SSD_CONV_CH, BF16, done),
        tok(SSD_HEADS, F32),
        (jax.ShapeDtypeStruct((SSD_HEADS, TOKENS), F32),
         pl.BlockSpec((SSD_HEADS, tm), lambda i: (0, cur(i)))),
    ]
    return pl.pallas_call(
        _inproj_kernel,
        grid=(n_tiles + 1,),
        in_specs=[tok(D_MODEL, F32)[1],
                  pl.BlockSpec((1, N_MOD, D_MODEL), lambda i: (cur(i) // tiles, 0, 0)),
                  _resident((1, D_MODEL)),
                  tok(LANES, F32)[1], _resident((1, LANES)), _resident((1, LANES)),
                  _resident(perm.shape), _resident(weave.shape),
                  _resident(wqkv.shape), _resident(wz.shape), _resident(wxbc.shape),
                  _resident(wdt.shape), _resident(wdtT.shape),
                  _resident(conv_w.shape), _resident(conv_b.shape),
                  _resident(dtb_row.shape), _resident(dtb_col.shape)],
        out_specs=[s for _, s in outs],
        out_shape=[s for s, _ in outs],
        scratch_shapes=[pltpu.VMEM((tm, D_MODEL), BF16), pltpu.VMEM((tm, D_MODEL), BF16),
                        pltpu.VMEM(((SSD_CONV - 1) * SUBLANES, SSD_CONV_CH), F32),
                        pltpu.VMEM((tm, LANES), F32), pltpu.VMEM((tm, LANES), F32),
                        pltpu.VMEM((tm, SSD_CONV_CH), F32), pltpu.VMEM((tm, SSD_CONV_CH), F32)],
        compiler_params=_params("arbitrary"),
        name="in_proj",
    )(x1, mod, nw, posf, invf, sgn, perm, weave, wqkv, wz, wxbc, wdt, wdtT,
      conv_w, conv_b, dtb_row, dtb_col)


def _band_bias(row_pos):
    blk = ATT_BLOCK
    pos = row_pos(np.arange(blk))
    k_pos = np.concatenate([pos, pos + blk])
    dist = pos[:, None] + blk - k_pos[None, :]
    return jnp.asarray(np.where((dist >= 0) & (dist <= blk), 0.0, NEG_BIG), F32)


def _attn_kernel(band_ref, q_ref, kp_ref, kc_ref, vp_ref, vc_ref, o_ref, lse_ref):
    blk = ATT_BLOCK
    has_prev = pl.program_id(2) > 0
    flat = lambda ref: ref[...].reshape(-1, ref.shape[-1])
    q = flat(q_ref)
    k = jnp.concatenate([flat(kp_ref), flat(kc_ref)], axis=0)
    v = jnp.concatenate([flat(vp_ref), flat(vc_ref)], axis=0)
    band = band_ref[...]
    ki = lax.broadcasted_iota(jnp.int32, (blk, 2 * blk), 1)
    first_bias = jnp.where((ki >= blk) | has_prev, band, NEG_BIG)
    lane_row = lax.broadcasted_iota(jnp.int32, (1, LANES), 1)
    lane = lax.broadcasted_iota(jnp.int32, (blk, LANES), 1)
    low = lane < ATT_HEAD_DIM
    head_mask = [(lane_row < ATT_HEAD_DIM).astype(BF16), (lane_row >= ATT_HEAD_DIM).astype(BF16)]
    def scores(unit):
        j, pair, half = unit
        cols = slice(pair * LANES, (pair + 1) * LANES)
        bias = first_bias if j == 0 else band
        qh = q[j * blk:(j + 1) * blk, cols] * head_mask[half]
        return lax.dot_general(qh, k[j * blk:(j + 2) * blk, cols], _NT,
                               preferred_element_type=F32) + bias

    def attend(unit, s):
        j, pair, _ = unit
        m = jnp.max(s, axis=1, keepdims=True)
        p = jnp.exp(s - m)
        den = jnp.sum(p, axis=1, keepdims=True)
        vpair = v[j * blk:(j + 2) * blk, pair * LANES:(pair + 1) * LANES]
        o = jnp.dot(p.astype(BF16), vpair, preferred_element_type=F32)
        return o * (1.0 / den), m + jnp.log(den)

    units = [(j, pair, half) for j in range(ATT_STEP_BLOCKS)
             for pair in range(ATT_HEADS // 2) for half in range(2)]
    pending = scores(units[0])
    o_blocks, lse_blocks, o_pairs, outs = [], [], [], []
    lse_tile = jnp.zeros((blk, LANES), F32)
    for n, unit in enumerate(units):
        s = pending
        if n + 1 < len(units):
            pending = scores(units[n + 1])
        o, lse = attend(unit, s)
        j, pair, half = unit
        outs.append(o)
        lse_tile = jnp.where(lane == 2 * pair + half, lse, lse_tile)
        if half == 1:
            o_pairs.append(jnp.where(low, outs[0], outs[1]).astype(BF16))
            outs = []
            if pair == ATT_HEADS // 2 - 1:
                o_blocks.append(jnp.concatenate(o_pairs, axis=1))
                lse_blocks.append(lse_tile)
                o_pairs, lse_tile = [], jnp.zeros((blk, LANES), F32)
    o_ref[...] = jnp.concatenate(o_blocks, axis=0).reshape(o_ref.shape)
    lse_ref[...] = jnp.concatenate(lse_blocks, axis=0).reshape(lse_ref.shape)


def _attention_natural(qkv):
    step = ATT_STEP_BLOCKS * ATT_BLOCK
    n_blk = SEQ // step
    qkv = qkv.reshape(BATCH, SEQ, 3 * ATT_WIDTH)
    cur = lambda part: pl.BlockSpec((None, step, ATT_WIDTH), lambda b, j, i: (b, i, part))
    prev = lambda part: pl.BlockSpec(
        (None, ATT_BLOCK, ATT_WIDTH),
        lambda b, j, i: (b, jnp.maximum(ATT_STEP_BLOCKS * i - 1, 0), part))
    o, lse = pl.pallas_call(
        _attn_kernel,
        grid=(BATCH, 1, n_blk),
        in_specs=[_resident((ATT_BLOCK, 2 * ATT_BLOCK)), cur(0), prev(1), cur(1), prev(2), cur(2)],
        out_specs=[cur(0), pl.BlockSpec((None, step, LANES), lambda b, j, i: (b, i, 0))],
        out_shape=[jax.ShapeDtypeStruct((BATCH, SEQ, ATT_WIDTH), BF16),
                   jax.ShapeDtypeStruct((BATCH, SEQ, LANES), F32)],
        compiler_params=_params("arbitrary", "arbitrary", "arbitrary"),
        name="attn_dil1",
    )(_band_bias(lambda r: r), qkv, qkv, qkv, qkv, qkv)
    return o.reshape(TOKENS, ATT_WIDTH), lse.reshape(TOKENS, LANES)


def _attention_regrouped(qkvg, r):
    tiles, rows = qkvg.shape[1], qkvg.shape[3]
    fold = REGROUP // r
    tiles_per_blk = ATT_BLOCK // (fold * rows)
    n_blk = tiles // (ATT_STEP_BLOCKS * tiles_per_blk)
    shape6 = lambda w: (BATCH, tiles, fold, r, rows, w)
    qkvg = qkvg.reshape(shape6(3 * ATT_WIDTH))
    blk6 = lambda n_tiles, w: (None, n_tiles, fold, None, rows, w)
    cur = lambda w, part=0: pl.BlockSpec(blk6(ATT_STEP_BLOCKS * tiles_per_blk, w),
                                         lambda b, j, i: (b, i, 0, j, 0, part))
    prev = lambda part: pl.BlockSpec(
        blk6(tiles_per_blk, ATT_WIDTH),
        lambda b, j, i: (b, jnp.maximum(ATT_STEP_BLOCKS * i - 1, 0), 0, j, 0, part))
    per_tile = fold * rows

    def row_pos(rho):
        t = rho // per_tile
        a = (rho // rows) % fold
        return t * per_tile + (rho % rows) * fold + a

    o, lse = pl.pallas_call(
        _attn_kernel,
        grid=(BATCH, r, n_blk),
        in_specs=[_resident((ATT_BLOCK, 2 * ATT_BLOCK)), cur(ATT_WIDTH, 0),
                  prev(1), cur(ATT_WIDTH, 1), prev(2), cur(ATT_WIDTH, 2)],
        out_specs=[cur(ATT_WIDTH), cur(LANES)],
        out_shape=[jax.ShapeDtypeStruct(shape6(ATT_WIDTH), BF16),
                   jax.ShapeDtypeStruct(shape6(LANES), F32)],
        compiler_params=_params("arbitrary", "arbitrary", "arbitrary"),
        name=f"attn_dil{r}",
    )(_band_bias(row_pos), qkvg, qkvg, qkvg, qkvg, qkvg)
    grouped5 = (BATCH, tiles, REGROUP, rows)
    return o.reshape(grouped5 + (ATT_WIDTH,)), lse.reshape(grouped5 + (LANES,))


def _woven_time(i):
    return ((i & (SUBLANES - 1)) << 4) | (i >> 3)


def _ssd_kernel(xbc_ref, dt_ref, dtT_ref, zs_ref,
                alog_row_ref, alog_col_ref, dskip_ref, nw_ref, unweave_ref,
                o_ref, state_ref, y_ref):
    q = SSD_CHUNK
    heads = SSD_HEADS // SSD_GROUPS
    gw = heads * SSD_HEAD_DIM

    @pl.when(pl.program_id(1) == 0)
    def _():
        state_ref[...] = jnp.zeros_like(state_ref)

    dtT = dtT_ref[...]
    a_dt_col = dt_ref[...] * (-jnp.exp(alog_row_ref[...]))
    a_dt_row = dtT * (-jnp.exp(alog_col_ref[...]))
    t_row = _woven_time(lax.broadcasted_iota(jnp.int32, (q, q), 0))
    t_col = _woven_time(lax.broadcasted_iota(jnp.int32, (q, q), 1))
    causal = t_row >= t_col
    hi = lax.Precision.HIGHEST
    cs_col = jnp.dot(causal.astype(F32), a_dt_col, precision=hi, preferred_element_type=F32)
    cs_row = jnp.dot(a_dt_row, (t_row <= t_col).astype(F32), precision=hi,
                     preferred_element_type=F32)
    total = cs_row[:, q - 1:q]
    dt_decay_row = cs_row - jnp.log(dtT)
    to_end = dtT * jnp.exp(total - cs_row)
    chunk_decay = jnp.exp(total)
    r2 = lax.broadcasted_iota(jnp.int32, (gw, gw), 0)
    c2 = lax.broadcasted_iota(jnp.int32, (gw, gw), 1)
    eye = (r2 == c2).astype(BF16)
    no_rows = jnp.zeros((SSD_HEAD_DIM, q + SSD_STATE), BF16)
    sq_sum = jnp.zeros((q, LANES), F32)

    def load(g):
        b_at = SSD_INNER + g * SSD_STATE
        c_at = b_at + SSD_GROUPS * SSD_STATE
        b_g = xbc_ref[:, b_at:b_at + SSD_STATE]
        c_g = xbc_ref[:, c_at:c_at + SSD_STATE]
        cb = lax.dot_general(c_g, b_g, _NT, preferred_element_type=F32)
        x_g = xbc_ref[:, g * gw:(g + 1) * gw]
        x_t = lax.dot_general(eye, x_g, _NT, preferred_element_type=F32)
        return b_g, c_g, cb, x_g, x_t

    def weights(g, loaded):
        _, c_g, cb, _, _ = loaded
        c_f = c_g.astype(F32)
        w_parts = []
        for j in range(heads):
            h = g * heads + j
            cs_l = jnp.broadcast_to(cs_col[:, h:h + 1], (q, q))
            w_parts.append(jnp.where(causal, cb * jnp.exp(cs_l - dt_decay_row[h:h + 1]), 0.0)
                           .astype(BF16))
            w_parts.append((c_f * jnp.exp(cs_l)).astype(BF16))
        return jnp.concatenate(w_parts, axis=1)

    def finish(g, loaded, w_all, sq_sum):
        b_g, _, _, x_g, x_t = loaded
        state = state_ref[g]
        rhs = jnp.concatenate([x_t.astype(BF16), state.astype(BF16)], axis=1)
        rhs_rows, end_rows, decay_rows = [], [], []
        for j in range(heads):
            h = g * heads + j
            mine = rhs[j * SSD_HEAD_DIM:(j + 1) * SSD_HEAD_DIM]
            rhs_rows.append(jnp.concatenate([mine if jj == j else no_rows for jj in range(heads)],
                                            axis=1))
            end_rows.append(jnp.broadcast_to(to_end[h:h + 1], (SSD_HEAD_DIM, q)))
            decay_rows.append(jnp.broadcast_to(chunk_decay[h:h + 1], (SSD_HEAD_DIM, SSD_STATE)))
        y = lax.dot_general(w_all, jnp.concatenate(rhs_rows, axis=0), _NT,
                            preferred_element_type=F32)
        xcols = slice(g * gw, (g + 1) * gw)
        y = (y + dskip_ref[:, xcols] * x_g.astype(F32)) * zs_ref[:, xcols].astype(F32)
        y_ref[:, xcols] = y
        sq = y * y
        upd = jnp.dot((x_t * jnp.concatenate(end_rows, axis=0)).astype(BF16), b_g,
                      preferred_element_type=F32)
        state_ref[g] = state * jnp.concatenate(decay_rows, axis=0) + upd
        return sq_sum + sq[:, :LANES] + sq[:, LANES:]

    loaded = {0: load(0)}
    w_all = {0: weights(0, loaded[0])}
    for g in range(SSD_GROUPS):
        if g + 1 < SSD_GROUPS:
            loaded[g + 1] = load(g + 1)
        if g + 1 < SSD_GROUPS:
            w_all[g + 1] = weights(g + 1, loaded[g + 1])
        sq_sum = finish(g, loaded.pop(g), w_all.pop(g), sq_sum)

    ms = jnp.sum(sq_sum, axis=-1, keepdims=True) * (1.0 / SSD_INNER)
    y = (y_ref[...] * lax.rsqrt(ms + NORM_EPS) * nw_ref[...]).astype(BF16)
    o_ref[...] = jnp.dot(unweave_ref[...], y, preferred_element_type=F32).astype(BF16)


def _ssd(xbc, dt, dtT, zs, alog_row, alog_col, dskip, nw, unweave):
    n_chunks = SEQ // SSD_CHUNK
    tok = lambda w: pl.BlockSpec((SSD_CHUNK, w), lambda b, c: (b * n_chunks + c, 0))
    gw = SSD_INNER // SSD_GROUPS
    return pl.pallas_call(
        _ssd_kernel,
        grid=(BATCH, n_chunks),
        in_specs=[tok(SSD_CONV_CH), tok(SSD_HEADS),
                  pl.BlockSpec((SSD_HEADS, SSD_CHUNK), lambda b, c: (0, b * n_chunks + c)),
                  tok(SSD_INNER),
                  _resident((1, SSD_HEADS)), _resident((SSD_HEADS, 1)),
                  _resident((1, SSD_INNER)), _resident((1, SSD_INNER)),
                  _resident(unweave.shape)],
        out_specs=tok(SSD_INNER),
        out_shape=jax.ShapeDtypeStruct((TOKENS, SSD_INNER), BF16),
        scratch_shapes=[pltpu.VMEM((SSD_GROUPS, gw, SSD_STATE), F32),
                        pltpu.VMEM((SSD_CHUNK, SSD_INNER), F32)],
        compiler_params=_params("arbitrary", "arbitrary"),
        name="ssd_scan",
    )(xbc, dt, dtT, zs, alog_row, alog_col, dskip, nw, unweave)


def _out_kernel(x_ref, mod_ref, o1_ref, l1_ref, o4_ref, l4_ref, o16_ref, l16_ref,
                yn_ref, expand_ref, restore_ref, n2w_ref, wg_ref, watt_ref, wssd_ref, wmix_ref,
                n3w_ref, wgu_ref, wd_ref, fnw_ref, out_ref):
    tm = x_ref.shape[0]
    mod = mod_ref[0]
    expand = lambda a: jnp.dot(a.astype(BF16), expand_ref[...], preferred_element_type=F32)
    rows = lambda ref: ref[...].reshape(tm, ref.shape[-1])

    def mix(lse_a, o_a, lse_b, o_b):
        m = jnp.maximum(lse_a, lse_b)
        e_a, e_b = jnp.exp(lse_a - m), jnp.exp(lse_b - m)
        tot = e_a + e_b
        inv = 1.0 / tot
        return m + jnp.log(tot), expand(e_a * inv) * o_a + expand(e_b * inv) * o_b

    lse_g, o_g = mix(rows(l4_ref), rows(o4_ref).astype(F32), rows(l16_ref), rows(o16_ref).astype(F32))
    restore = restore_ref[...]
    o_g = jnp.dot(restore, o_g.astype(BF16), preferred_element_type=F32)
    lse_rest, lse_g_tok = lse_g, None
    for _ in range(3):
        piece = lse_rest.astype(BF16)
        moved = jnp.dot(restore, piece, preferred_element_type=F32)
        lse_g_tok = moved if lse_g_tok is None else lse_g_tok + moved
        lse_rest = lse_rest - piece.astype(F32)
    _, o_att = mix(l1_ref[...], o1_ref[...].astype(F32), lse_g_tok, o_g)

    y_att = jnp.dot(o_att.astype(BF16), watt_ref[...], preferred_element_type=F32)
    y_ssd = jnp.dot(yn_ref[...], wssd_ref[...], preferred_element_type=F32)
    x1 = x_ref[...]
    h_mix = _rms_mod(x1, n2w_ref[...], mod[3:4], mod[4:5]).astype(BF16)
    gates = _sigmoid(jnp.dot(h_mix, wg_ref[...], preferred_element_type=F32))
    merged = gates[:, :D_MODEL] * y_att + gates[:, D_MODEL:] * y_ssd
    y = jnp.dot(merged.astype(BF16), wmix_ref[...], preferred_element_type=F32)
    x2 = x1 + mod[5:6] * y
    h = _rms_mod(x2, n3w_ref[...], mod[6:7], mod[7:8]).astype(BF16)
    x3 = x2 + (0.5 * mod[8:9]) * _swiglu(h, wgu_ref, wd_ref)
    out_ref[...] = _rms(x3, fnw_ref[...])


def _out(x1, mod, o1, l1, o4, l4, o16, l16, yn, expand, restore, n2w, wg,
         watt, wssd, wmix, n3w, wgu, wd, fnw):
    tm = TOKEN_TILE
    tiles = SEQ // tm
    grouped = lambda w: pl.BlockSpec((None, None, REGROUP, tm // REGROUP, w),
                                     lambda i: (i // tiles, i % tiles, 0, 0, 0))
    return pl.pallas_call(
        _out_kernel,
        grid=(TOKENS // tm,),
        in_specs=[_tile_spec(D_MODEL), _mod_spec(), _tile_spec(ATT_WIDTH), _tile_spec(LANES),
                  grouped(ATT_WIDTH), grouped(LANES), grouped(ATT_WIDTH), grouped(LANES),
                  _tile_spec(SSD_INNER)]
                 + [_resident(a.shape) for a in (expand, restore, n2w, wg, watt, wssd, wmix, n3w,
                                                 wgu, wd, fnw)],
        out_specs=_tile_spec(D_MODEL),
        out_shape=jax.ShapeDtypeStruct((TOKENS, D_MODEL), F32),
        compiler_params=_params("arbitrary"),
        name="mix_out_ffn2",
    )(x1, mod, o1, l1, o4, l4, o16, l16, yn, expand, restore, n2w, wg,
      watt, wssd, wmix, n3w, wgu, wd, fnw)


def _rope_tables():
    inv_freq = ROPE_THETA ** (-jnp.arange(0, ROPE_DIM, 2, dtype=F32) / ROPE_DIM)
    d = np.arange(LANES) % ATT_HEAD_DIM
    half = ROPE_DIM // 2
    invf = jnp.where(d < ROPE_DIM, inv_freq[d % half], 0.0).astype(F32).reshape(1, LANES)
    sgn = np.where(d < half, -1.0, np.where(d < ROPE_DIM, 1.0, 0.0)).astype(np.float32)
    return invf, jnp.asarray(sgn).reshape(1, LANES)


def _regroup_matrix(tm):
    rho = np.arange(tm)
    src = REGROUP * (rho % (tm // REGROUP)) + rho // (tm // REGROUP)
    return (np.arange(tm)[None, :] == src[:, None]).astype(np.float32)


def _weave_matrix(tm):
    rho = np.arange(tm)
    src = (rho // SSD_CHUNK) * SSD_CHUNK + _woven_time(rho % SSD_CHUNK)
    return (np.arange(tm)[None, :] == src[:, None]).astype(np.float32)


def kernel(x, c, positions, w_ada, b_ada, norm1_w, ffn1_w_gu, ffn1_w_down, norm2_w, w_in,
           conv_w, conv_b, dt_bias, a_log, d_skip, ssd_norm_w, w_att_out, w_ssd_out,
           w_mix_out, norm3_w, ffn2_w_gu, ffn2_w_down, final_norm_w):
    l = 0
    row = lambda t: t.reshape(1, -1).astype(F32)
    xf = x.reshape(TOKENS, D_MODEL)
    mod = _ada(c, w_ada, b_ada, l).reshape(BATCH, N_MOD, D_MODEL)

    x1 = _ffn1(xf, mod, row(norm1_w[l]), ffn1_w_gu[l].astype(BF16), ffn1_w_down[l].astype(BF16))

    o_z = 3 * ATT_WIDTH
    o_xbc = o_z + SSD_INNER
    o_dt = o_xbc + SSD_CONV_CH
    o_g = o_dt + SSD_HEADS
    w = w_in[l]
    w_dt = w[:, o_dt:o_g]
    invf, sgn = _rope_tables()
    perm = _regroup_matrix(TOKEN_TILE)
    posf = jnp.broadcast_to(positions.reshape(TOKENS, 1).astype(F32), (TOKENS, LANES))
    qkv, qkvg, zs, xbc, dt, dtT = _inproj(
        x1, mod, row(norm2_w[l]), posf, invf, sgn, jnp.asarray(perm, BF16),
        jnp.asarray(_weave_matrix(TOKEN_TILE), BF16),
        w[:, :o_z].astype(BF16), w[:, o_z:o_xbc].astype(BF16), w[:, o_xbc:o_dt].astype(BF16),
        jnp.pad(w_dt, ((0, 0), (0, LANES - SSD_HEADS))).astype(BF16), w_dt.T.astype(BF16),
        conv_w[l], row(conv_b[l]), row(dt_bias[l]), dt_bias[l].reshape(SSD_HEADS, 1))

    o1, l1 = _attention_natural(qkv)
    o4, l4 = _attention_regrouped(qkvg, 4)
    o16, l16 = _attention_regrouped(qkvg, 16)

    yn = _ssd(xbc, dt, dtT, zs, row(a_log[l]), a_log[l].reshape(SSD_HEADS, 1),
              row(jnp.repeat(d_skip[l], SSD_HEAD_DIM)), row(ssd_norm_w[l]),
              jnp.asarray(_weave_matrix(SSD_CHUNK).T, BF16))

    head_of_col = np.arange(ATT_WIDTH) // ATT_HEAD_DIM
    expand = jnp.asarray(np.arange(LANES)[:, None] == head_of_col[None, :], BF16)
    out = _out(x1, mod, o1, l1, o4, l4, o16, l16, yn, expand, jnp.asarray(perm.T, BF16),
               row(norm2_w[l]), w[:, o_g:].astype(BF16), w_att_out[l].astype(BF16), w_ssd_out[l].astype(BF16), w_mix_out[l].astype(BF16),
               row(norm3_w[l]), ffn2_w_gu[l].astype(BF16), ffn2_w_down[l].astype(BF16),
               row(final_norm_w))
    return out.reshape(BATCH, SEQ, D_MODEL)
```

```python
import jax
import jax.numpy as jnp
import numpy as np
from jax import lax
from jax.experimental import pallas as pl
from jax.experimental.pallas import tpu as pltpu

F32 = jnp.float32
BF16 = jnp.bfloat16

D_MODEL = 1024
BATCH = 2
SEQ = 8192
TOKENS = BATCH * SEQ
ATT_HEADS = 12
ATT_HEAD_DIM = 64
ATT_WIDTH = ATT_HEADS * ATT_HEAD_DIM
ROPE_DIM = ATT_HEAD_DIM // 4
ROPE_THETA = 500000.0
ATT_BLOCK = 128
SSD_INNER = 2 * D_MODEL
SSD_HEAD_DIM = 64
SSD_HEADS = SSD_INNER // SSD_HEAD_DIM
SSD_GROUPS = 8
SSD_STATE = 128
SSD_CONV = 4
SSD_CHUNK = 128
SSD_CONV_CH = SSD_INNER + 2 * SSD_GROUPS * SSD_STATE
D_FF = 2816
N_MOD = 9
NORM_EPS = 1e-6

LANES = 128
SUBLANES = 8
VMEM_LIMIT = 56 * 1024 * 1024
TOKEN_TILE = 256
COL_CHUNK = 256
PROJ_CHUNK = 512
REGROUP = 16
ATT_STEP_BLOCKS = 4
ADA_COL_TILE = 1536
NEG_BIG = -1e30

_NT = (((1,), (1,)), ((), ()))


def _params(*sem):
    return pltpu.CompilerParams(dimension_semantics=sem, vmem_limit_bytes=VMEM_LIMIT)


def _resident(shape):
    zeros = (0,) * len(shape)
    return pl.BlockSpec(shape, lambda *_: zeros, pipeline_mode=pl.Buffered(1))


def _sigmoid(x):
    return 0.5 * jnp.tanh(0.5 * x) + 0.5


def _silu(x):
    return x * _sigmoid(x)


def _rms(x, w):
    ms = jnp.sum(x * x, axis=-1, keepdims=True) * (1.0 / x.shape[-1])
    return x * lax.rsqrt(ms + NORM_EPS) * w


def _rms_mod(x, w, shift, scale):
    return _rms(x, w) * (1.0 + scale) + shift


def _swiglu(h, wgu_ref, wd_ref):
    gu = jnp.dot(h, wgu_ref[...], preferred_element_type=F32)
    a = (_silu(gu[:, :D_FF]) * gu[:, D_FF:]).astype(BF16)
    return jnp.dot(a, wd_ref[...], preferred_element_type=F32)


def _ada_kernel(ct_ref, w_ref, b_ref, o_ref):
    act = _silu(ct_ref[...])
    w = w_ref[...]
    rows = [jnp.sum(w * act[:, b:b + 1], axis=0, keepdims=True) for b in range(BATCH)]
    o_ref[...] = jnp.concatenate(rows, axis=0) + b_ref[...]


def _ada(c, w_ada, b_ada, layer):
    n = N_MOD * D_MODEL
    return pl.pallas_call(
        _ada_kernel,
        grid=(n // ADA_COL_TILE,),
        in_specs=[
            pl.BlockSpec((D_MODEL, BATCH), lambda j: (0, 0)),
            pl.BlockSpec((None, D_MODEL, ADA_COL_TILE), lambda j: (layer, 0, j)),
            pl.BlockSpec((None, 1, ADA_COL_TILE), lambda j: (layer, 0, j)),
        ],
        out_specs=pl.BlockSpec((BATCH, ADA_COL_TILE), lambda j: (0, j)),
        out_shape=jax.ShapeDtypeStruct((BATCH, n), F32),
        compiler_params=_params("arbitrary"),
        name="ada_mod",
    )(c.T, w_ada, b_ada.reshape(-1, 1, n))


def _ffn1_kernel(x_ref, mod_ref, nw_ref, wgu_ref, wd_ref, o_ref):
    x = x_ref[...]
    mod = mod_ref[0]
    h = _rms_mod(x, nw_ref[...], mod[0:1], mod[1:2]).astype(BF16)
    o_ref[...] = x + (0.5 * mod[2:3]) * _swiglu(h, wgu_ref, wd_ref)


def _tile_spec(width, tm=TOKEN_TILE):
    return pl.BlockSpec((tm, width), lambda i: (i, 0))


def _mod_spec(tm=TOKEN_TILE):
    per_batch = SEQ // tm
    return pl.BlockSpec((1, N_MOD, D_MODEL), lambda i: (i // per_batch, 0, 0))


def _ffn1(x, mod, nw, wgu, wd):
    return pl.pallas_call(
        _ffn1_kernel,
        grid=(TOKENS // TOKEN_TILE,),
        in_specs=[_tile_spec(D_MODEL), _mod_spec(), _resident((1, D_MODEL)),
                  _resident(wgu.shape), _resident(wd.shape)],
        out_specs=_tile_spec(D_MODEL),
        out_shape=jax.ShapeDtypeStruct((TOKENS, D_MODEL), F32),
        compiler_params=_params("arbitrary"),
        name="ffn1",
    )(x, mod, nw, wgu, wd)


def _softplus(x):
    return jnp.maximum(x, 0.0) + jnp.log1p(jnp.exp(-jnp.abs(x)))


def _inproj_kernel(*refs):
    *io_refs, raw_a_ref, raw_b_ref = refs
    hist_ref = io_refs[-3]
    step = pl.program_id(0)
    tiles_per_seq = SEQ // io_refs[0].shape[0]

    @pl.when(step == 0)
    def _():
        raw_b_ref[...] = jnp.zeros_like(raw_b_ref)

    @pl.when((step == 0) | (step % tiles_per_seq == 1))
    def _():
        hist_ref[...] = jnp.zeros_like(hist_ref)

    x_ref, mod_ref, nw_ref = io_refs[:3]
    weave_ref = io_refs[7]
    h_ref, hw_ref = io_refs[-5:-3]
    mod = mod_ref[0]
    h_ref[...] = _rms_mod(x_ref[...], nw_ref[...], mod[3:4], mod[4:5]).astype(BF16)
    hw_ref[...] = jnp.dot(weave_ref[...], h_ref[...], preferred_element_type=F32).astype(BF16)

    @pl.when(step % 2 == 0)
    def _():
        _inproj_step(*io_refs, raw_a_ref, raw_b_ref)

    @pl.when(step % 2 == 1)
    def _():
        _inproj_step(*io_refs, raw_b_ref, raw_a_ref)


def _inproj_step(x_ref, mod_ref, nw_ref, pos_ref, invf_ref, sgn_ref, perm_ref, weave_ref,
                 wqkv_ref, wz_ref, wxbc_ref, wdt_ref, wdtT_ref,
                 convw_ref, convb_ref, dtb_row_ref, dtb_col_ref,
                 qkv_ref, qkvg_ref, zs_ref, xbc_ref, dt_ref, dtT_ref,
                 h_ref, hw_ref, hist_ref, cos_ref, sin_ref, raw_out_ref, raw_in_ref):
    tm = x_ref.shape[0]

    def xbc_part(cols):
        raw_out_ref[:, cols] = jnp.dot(hw_ref[...], wxbc_ref[:, cols], preferred_element_type=F32)

    def rope_tables():
        ang = pos_ref[...] * invf_ref[...]
        cos_ref[...] = jnp.cos(ang)
        sin_ref[...] = jnp.sin(ang) * sgn_ref[...]

    def rope(t):
        reps = ATT_WIDTH // LANES
        cos = jnp.concatenate([cos_ref[...]] * reps, axis=1)
        sin = jnp.concatenate([sin_ref[...]] * reps, axis=1)
        lane = lax.broadcasted_iota(jnp.int32, (tm, ATT_WIDTH), 1)
        first_half = (lane & (ATT_HEAD_DIM - 1)) < (ROPE_DIM // 2)
        partner = jnp.where(first_half,
                            pltpu.roll(t, ATT_WIDTH - ROPE_DIM // 2, 1),
                            pltpu.roll(t, ROPE_DIM // 2, 1))
        return t * cos + partner * sin

    def qkv_part(idx):
        cols = slice(idx * ATT_WIDTH, (idx + 1) * ATT_WIDTH)
        t = jnp.dot(h_ref[...], wqkv_ref[:, cols], preferred_element_type=F32)
        if idx == 0:
            t = rope(t) * (ATT_HEAD_DIM ** -0.5)
        elif idx == 1:
            t = rope(t)
        t = t.astype(BF16)
        qkv_ref[:, cols] = t
        regrouped = jnp.dot(perm_ref[...], t, preferred_element_type=F32).astype(BF16)
        qkvg_ref[:, :, cols] = regrouped.reshape(qkvg_ref.shape[:2] + (ATT_WIDTH,))

    def z_part(cols):
        z = jnp.dot(hw_ref[...], wz_ref[:, cols], preferred_element_type=F32)
        zs_ref[:, cols] = _silu(z).astype(BF16)

    def dt_part():
        hw = hw_ref[...]
        dt_raw = jnp.dot(hw, wdt_ref[...], preferred_element_type=F32)[:, :SSD_HEADS]
        dt_ref[...] = _softplus(dt_raw + dtb_row_ref[...])
        dtT_raw = lax.dot_general(wdtT_ref[...], hw, _NT, preferred_element_type=F32)
        dtT_ref[...] = _softplus(dtT_raw + dtb_col_ref[...])

    tail_rows = (SSD_CONV - 1) * SUBLANES
    tail_start = SSD_CHUNK - tail_rows
    first_sublane = lax.broadcasted_iota(jnp.int32, (SUBLANES, COL_CHUNK), 0) == 0

    def conv_chunk(src):
        raw = raw_in_ref[:, src]
        taps = [convw_ref[k:k + 1, src] for k in range(SSD_CONV)]
        prev_tail = hist_ref[:, src]
        pieces = []
        for ck in range(tm // SSD_CHUNK):
            cur = raw[ck * SSD_CHUNK:(ck + 1) * SSD_CHUNK]
            cur_tail = cur[tail_start:]
            wrapped = []
            for k in range(SSD_CONV - 1):
                rows = slice(k * SUBLANES, (k + 1) * SUBLANES)
                wrapped.append(jnp.where(first_sublane, pltpu.roll(prev_tail[rows], 1, 0),
                                         pltpu.roll(cur_tail[rows], 1, 0)))
            ext = jnp.concatenate(wrapped + [cur], axis=0)
            acc = cur * taps[SSD_CONV - 1] + convb_ref[:, src]
            for s in range(1, SSD_CONV):
                lo = (SSD_CONV - 1 - s) * SUBLANES
                acc = acc + ext[lo:lo + SSD_CHUNK] * taps[SSD_CONV - 1 - s]
            pieces.append(acc)
            prev_tail = cur_tail
        hist_ref[:, src] = prev_tail
        xbc_ref[:, src] = _silu(jnp.concatenate(pieces, axis=0)).astype(BF16)

    chunks = lambda width, w: [pl.ds(c * w, w) for c in range(width // w)]
    conv = chunks(SSD_CONV_CH, COL_CHUNK)
    parts = ([(xbc_part, (c,)) for c in chunks(SSD_CONV_CH, PROJ_CHUNK)]
             + [(z_part, (c,)) for c in chunks(SSD_INNER, PROJ_CHUNK)]
             + [(qkv_part, (2,)), (rope_tables, ()), (qkv_part, (0,)), (qkv_part, (1,)),
                (dt_part, ())])
    for i, (fn, args) in enumerate(parts):
        fn(*args)
        if i < len(conv):
            conv_chunk(conv[i])
    assert len(conv) <= len(parts)


def _inproj(x1, mod, nw, posf, invf, sgn, perm, weave, wqkv, wz, wxbc, wdt, wdtT,
            conv_w, conv_b, dtb_row, dtb_col):
    tm = TOKEN_TILE
    tiles = SEQ // tm
    n_tiles = TOKENS // tm
    cur = lambda i: jnp.minimum(i, n_tiles - 1)
    done = lambda i: jnp.maximum(i - 1, 0)
    tok = lambda w, d, at=cur: (jax.ShapeDtypeStruct((TOKENS, w), d),
                                pl.BlockSpec((tm, w), lambda i: (at(i), 0)))
    grouped = (jax.ShapeDtypeStruct((BATCH, tiles, REGROUP, tm // REGROUP, 3 * ATT_WIDTH), BF16),
               pl.BlockSpec((None, None, REGROUP, tm // REGROUP, 3 * ATT_WIDTH),
                            lambda i: (cur(i) // tiles, cur(i) % tiles, 0, 0, 0)))
    outs = [tok(3 * ATT_WIDTH, BF16), grouped,
        tok(SSD_INNER, BF16),
        tok(SSD_CONV_CH, BF16, done),
        tok(SSD_HEADS, F32),
        (jax.ShapeDtypeStruct((SSD_HEADS, TOKENS), F32),
         pl.BlockSpec((SSD_HEADS, tm), lambda i: (0, cur(i)))),
    ]
    return pl.pallas_call(
        _inproj_kernel,
        grid=(n_tiles + 1,),
        in_specs=[tok(D_MODEL, F32)[1],
                  pl.BlockSpec((1, N_MOD, D_MODEL), lambda i: (cur(i) // tiles, 0, 0)),
                  _resident((1, D_MODEL)),
                  tok(LANES, F32)[1], _resident((1, LANES)), _resident((1, LANES)),
                  _resident(perm.shape), _resident(weave.shape),
                  _resident(wqkv.shape), _resident(wz.shape), _resident(wxbc.shape),
                  _resident(wdt.shape), _resident(wdtT.shape),
                  _resident(conv_w.shape), _resident(conv_b.shape),
                  _resident(dtb_row.shape), _resident(dtb_col.shape)],
        out_specs=[s for _, s in outs],
        out_shape=[s for s, _ in outs],
        scratch_shapes=[pltpu.VMEM((tm, D_MODEL), BF16), pltpu.VMEM((tm, D_MODEL), BF16),
                        pltpu.VMEM(((SSD_CONV - 1) * SUBLANES, SSD_CONV_CH), F32),
                        pltpu.VMEM((tm, LANES), F32), pltpu.VMEM((tm, LANES), F32),
                        pltpu.VMEM((tm, SSD_CONV_CH), F32), pltpu.VMEM((tm, SSD_CONV_CH), F32)],
        compiler_params=_params("arbitrary"),
        name="in_proj",
    )(x1, mod, nw, posf, invf, sgn, perm, weave, wqkv, wz, wxbc, wdt, wdtT,
      conv_w, conv_b, dtb_row, dtb_col)


def _band_bias(row_pos):
    blk = ATT_BLOCK
    pos = row_pos(np.arange(blk))
    k_pos = np.concatenate([pos, pos + blk])
    dist = pos[:, None] + blk - k_pos[None, :]
    return jnp.asarray(np.where((dist >= 0) & (dist <= blk), 0.0, NEG_BIG), F32)


def _attn_kernel(band_ref, q_ref, kp_ref, kc_ref, vp_ref, vc_ref, o_ref, lse_ref):
    blk = ATT_BLOCK
    has_prev = pl.program_id(2) > 0
    flat = lambda ref: ref[...].reshape(-1, ref.shape[-1])
    q = flat(q_ref)
    k = jnp.concatenate([flat(kp_ref), flat(kc_ref)], axis=0)
    v = jnp.concatenate([flat(vp_ref), flat(vc_ref)], axis=0)
    band = band_ref[...]
    ki = lax.broadcasted_iota(jnp.int32, (blk, 2 * blk), 1)
    first_bias = jnp.where((ki >= blk) | has_prev, band, NEG_BIG)
    lane_row = lax.broadcasted_iota(jnp.int32, (1, LANES), 1)
    lane = lax.broadcasted_iota(jnp.int32, (blk, LANES), 1)
    low = lane < ATT_HEAD_DIM
    head_mask = [(lane_row < ATT_HEAD_DIM).astype(BF16), (lane_row >= ATT_HEAD_DIM).astype(BF16)]
    def scores(unit):
        j, pair, half = unit
        cols = slice(pair * LANES, (pair + 1) * LANES)
        bias = first_bias if j == 0 else band
        qh = q[j * blk:(j + 1) * blk, cols] * head_mask[half]
        return lax.dot_general(qh, k[j * blk:(j + 2) * blk, cols], _NT,
                               preferred_element_type=F32) + bias

    def attend(unit, s):
        j, pair, _ = unit
        m = jnp.max(s, axis=1, keepdims=True)
        p = jnp.exp(s - m)
        den = jnp.sum(p, axis=1, keepdims=True)
        vpair = v[j * blk:(j + 2) * blk, pair * LANES:(pair + 1) * LANES]
        o = jnp.dot(p.astype(BF16), vpair, preferred_element_type=F32)
        return o * (1.0 / den), m + jnp.log(den)

    units = [(j, pair, half) for j in range(ATT_STEP_BLOCKS)
             for pair in range(ATT_HEADS // 2) for half in range(2)]
    pending = scores(units[0])
    o_blocks, lse_blocks, o_pairs, outs = [], [], [], []
    lse_tile = jnp.zeros((blk, LANES), F32)
    for n, unit in enumerate(units):
        s = pending
        if n + 1 < len(units):
            pending = scores(units[n + 1])
        o, lse = attend(unit, s)
        j, pair, half = unit
        outs.append(o)
        lse_tile = jnp.where(lane == 2 * pair + half, lse, lse_tile)
        if half == 1:
            o_pairs.append(jnp.where(low, outs[0], outs[1]).astype(BF16))
            outs = []
            if pair == ATT_HEADS // 2 - 1:
                o_blocks.append(jnp.concatenate(o_pairs, axis=1))
                lse_blocks.append(lse_tile)
                o_pairs, lse_tile = [], jnp.zeros((blk, LANES), F32)
    o_ref[...] = jnp.concatenate(o_blocks, axis=0).reshape(o_ref.shape)
    lse_ref[...] = jnp.concatenate(lse_blocks, axis=0).reshape(lse_ref.shape)


def _attention_natural(qkv):
    step = ATT_STEP_BLOCKS * ATT_BLOCK
    n_blk = SEQ // step
    qkv = qkv.reshape(BATCH, SEQ, 3 * ATT_WIDTH)
    cur = lambda part: pl.BlockSpec((None, step, ATT_WIDTH), lambda b, j, i: (b, i, part))
    prev = lambda part: pl.BlockSpec(
        (None, ATT_BLOCK, ATT_WIDTH),
        lambda b, j, i: (b, jnp.maximum(ATT_STEP_BLOCKS * i - 1, 0), part))
    o, lse = pl.pallas_call(
        _attn_kernel,
        grid=(BATCH, 1, n_blk),
        in_specs=[_resident((ATT_BLOCK, 2 * ATT_BLOCK)), cur(0), prev(1), cur(1), prev(2), cur(2)],
        out_specs=[cur(0), pl.BlockSpec((None, step, LANES), lambda b, j, i: (b, i, 0))],
        out_shape=[jax.ShapeDtypeStruct((BATCH, SEQ, ATT_WIDTH), BF16),
                   jax.ShapeDtypeStruct((BATCH, SEQ, LANES), F32)],
        compiler_params=_params("arbitrary", "arbitrary", "arbitrary"),
        name="attn_dil1",
    )(_band_bias(lambda r: r), qkv, qkv, qkv, qkv, qkv)
    return o.reshape(TOKENS, ATT_WIDTH), lse.reshape(TOKENS, LANES)


def _attention_regrouped(qkvg, r):
    tiles, rows = qkvg.shape[1], qkvg.shape[3]
    fold = REGROUP // r
    tiles_per_blk = ATT_BLOCK // (fold * rows)
    n_blk = tiles // (ATT_STEP_BLOCKS * tiles_per_blk)
    shape6 = lambda w: (BATCH, tiles, fold, r, rows, w)
    qkvg = qkvg.reshape(shape6(3 * ATT_WIDTH))
    blk6 = lambda n_tiles, w: (None, n_tiles, fold, None, rows, w)
    cur = lambda w, part=0: pl.BlockSpec(blk6(ATT_STEP_BLOCKS * tiles_per_blk, w),
                                         lambda b, j, i: (b, i, 0, j, 0, part))
    prev = lambda part: pl.BlockSpec(
        blk6(tiles_per_blk, ATT_WIDTH),
        lambda b, j, i: (b, jnp.maximum(ATT_STEP_BLOCKS * i - 1, 0), 0, j, 0, part))
    per_tile = fold * rows

    def row_pos(rho):
        t = rho // per_tile
        a = (rho // rows) % fold
        return t * per_tile + (rho % rows) * fold + a

    o, lse = pl.pallas_call(
        _attn_kernel,
        grid=(BATCH, r, n_blk),
        in_specs=[_resident((ATT_BLOCK, 2 * ATT_BLOCK)), cur(ATT_WIDTH, 0),
                  prev(1), cur(ATT_WIDTH, 1), prev(2), cur(ATT_WIDTH, 2)],
        out_specs=[cur(ATT_WIDTH), cur(LANES)],
        out_shape=[jax.ShapeDtypeStruct(shape6(ATT_WIDTH), BF16),
                   jax.ShapeDtypeStruct(shape6(LANES), F32)],
        compiler_params=_params("arbitrary", "arbitrary", "arbitrary"),
        name=f"attn_dil{r}",
    )(_band_bias(row_pos), qkvg, qkvg, qkvg, qkvg, qkvg)
    grouped5 = (BATCH, tiles, REGROUP, rows)
    return o.reshape(grouped5 + (ATT_WIDTH,)), lse.reshape(grouped5 + (LANES,))


def _woven_time(i):
    return ((i & (SUBLANES - 1)) << 4) | (i >> 3)


def _ssd_kernel(*refs):
    *io_refs, y_a_ref, sq_a_ref, y_b_ref, sq_b_ref = refs
    state_ref = io_refs[-1]
    first = (pl.program_id(0) == 0) & (pl.program_id(1) == 0)

    @pl.when(first)
    def _():
        y_b_ref[...] = jnp.zeros_like(y_b_ref)
        sq_b_ref[...] = jnp.zeros_like(sq_b_ref)

    @pl.when(pl.program_id(1) == 0)
    def _():
        state_ref[...] = jnp.zeros_like(state_ref)

    @pl.when(pl.program_id(1) % 2 == 0)
    def _():
        _ssd_step(*io_refs, y_a_ref, sq_a_ref, y_b_ref, sq_b_ref)

    @pl.when(pl.program_id(1) % 2 == 1)
    def _():
        _ssd_step(*io_refs, y_b_ref, sq_b_ref, y_a_ref, sq_a_ref)


def _ssd_step(xbc_ref, dt_ref, dtT_ref, zs_ref,
              alog_row_ref, alog_col_ref, dskip_ref, nw_ref, unweave_ref,
              o_ref, state_ref, y_ref, sq_ref, y_done_ref, sq_done_ref):
    q = SSD_CHUNK
    heads = SSD_HEADS // SSD_GROUPS
    gw = heads * SSD_HEAD_DIM

    dtT = dtT_ref[...]
    a_dt_col = dt_ref[...] * (-jnp.exp(alog_row_ref[...]))
    a_dt_row = dtT * (-jnp.exp(alog_col_ref[...]))
    t_row = _woven_time(lax.broadcasted_iota(jnp.int32, (q, q), 0))
    t_col = _woven_time(lax.broadcasted_iota(jnp.int32, (q, q), 1))
    causal = t_row >= t_col
    hi = lax.Precision.HIGHEST
    cs_col = jnp.dot(causal.astype(F32), a_dt_col, precision=hi, preferred_element_type=F32)
    cs_row = jnp.dot(a_dt_row, (t_row <= t_col).astype(F32), precision=hi,
                     preferred_element_type=F32)
    total = cs_row[:, q - 1:q]
    dt_decay_row = cs_row - jnp.log(dtT)
    to_end = dtT * jnp.exp(total - cs_row)
    chunk_decay = jnp.exp(total)
    r2 = lax.broadcasted_iota(jnp.int32, (gw, gw), 0)
    c2 = lax.broadcasted_iota(jnp.int32, (gw, gw), 1)
    eye = (r2 == c2).astype(BF16)
    no_rows = jnp.zeros((SSD_HEAD_DIM, q + SSD_STATE), BF16)
    sq_sum = jnp.zeros((q, LANES), F32)

    def load(g):
        b_at = SSD_INNER + g * SSD_STATE
        c_at = b_at + SSD_GROUPS * SSD_STATE
        b_g = xbc_ref[:, b_at:b_at + SSD_STATE]
        c_g = xbc_ref[:, c_at:c_at + SSD_STATE]
        cb = lax.dot_general(c_g, b_g, _NT, preferred_element_type=F32)
        x_g = xbc_ref[:, g * gw:(g + 1) * gw]
        x_t = lax.dot_general(eye, x_g, _NT, preferred_element_type=F32)
        return b_g, c_g, cb, x_g, x_t

    def weights(g, loaded):
        _, c_g, cb, _, _ = loaded
        c_f = c_g.astype(F32)
        w_parts = []
        for j in range(heads):
            h = g * heads + j
            cs_l = jnp.broadcast_to(cs_col[:, h:h + 1], (q, q))
            w_parts.append(jnp.where(causal, cb * jnp.exp(cs_l - dt_decay_row[h:h + 1]), 0.0)
                           .astype(BF16))
            w_parts.append((c_f * jnp.exp(cs_l)).astype(BF16))
        return jnp.concatenate(w_parts, axis=1)

    def finish(g, loaded, w_all, sq_sum):
        b_g, _, _, x_g, x_t = loaded
        state = state_ref[g]
        rhs = jnp.concatenate([x_t.astype(BF16), state.astype(BF16)], axis=1)
        rhs_rows, end_rows, decay_rows = [], [], []
        for j in range(heads):
            h = g * heads + j
            mine = rhs[j * SSD_HEAD_DIM:(j + 1) * SSD_HEAD_DIM]
            rhs_rows.append(jnp.concatenate([mine if jj == j else no_rows for jj in range(heads)],
                                            axis=1))
            end_rows.append(jnp.broadcast_to(to_end[h:h + 1], (SSD_HEAD_DIM, q)))
            decay_rows.append(jnp.broadcast_to(chunk_decay[h:h + 1], (SSD_HEAD_DIM, SSD_STATE)))
        y = lax.dot_general(w_all, jnp.concatenate(rhs_rows, axis=0), _NT,
                            preferred_element_type=F32)
        xcols = slice(g * gw, (g + 1) * gw)
        y = (y + dskip_ref[:, xcols] * x_g.astype(F32)) * zs_ref[:, xcols].astype(F32)
        y_ref[:, xcols] = y
        sq = y * y
        upd = jnp.dot((x_t * jnp.concatenate(end_rows, axis=0)).astype(BF16), b_g,
                      preferred_element_type=F32)
        state_ref[g] = state * jnp.concatenate(decay_rows, axis=0) + upd
        return sq_sum + sq[:, :LANES] + sq[:, LANES:]

    ms = jnp.sum(sq_done_ref[...], axis=-1, keepdims=True) * (1.0 / SSD_INNER)
    inv_rms = lax.rsqrt(ms + NORM_EPS)

    def emit(g):
        cols = slice(g * gw, (g + 1) * gw)
        y = (y_done_ref[:, cols] * inv_rms * nw_ref[:, cols]).astype(BF16)
        o_ref[:, cols] = jnp.dot(unweave_ref[...], y, preferred_element_type=F32).astype(BF16)

    loaded = {0: load(0)}
    w_all = {0: weights(0, loaded[0])}
    for g in range(SSD_GROUPS):
        if g + 1 < SSD_GROUPS:
            loaded[g + 1] = load(g + 1)
            w_all[g + 1] = weights(g + 1, loaded[g + 1])
        emit(g)
        sq_sum = finish(g, loaded.pop(g), w_all.pop(g), sq_sum)
    sq_ref[...] = sq_sum


def _ssd(xbc, dt, dtT, zs, alog_row, alog_col, dskip, nw, unweave):
    n_chunks = SEQ // SSD_CHUNK
    cur = lambda b, c: b * n_chunks + jnp.minimum(c, n_chunks - 1)
    done = lambda b, c: b * n_chunks + jnp.maximum(c - 1, 0)
    tok = lambda w, at=cur: pl.BlockSpec((SSD_CHUNK, w), lambda b, c: (at(b, c), 0))
    gw = SSD_INNER // SSD_GROUPS
    return pl.pallas_call(
        _ssd_kernel,
        grid=(BATCH, n_chunks + 1),
        in_specs=[tok(SSD_CONV_CH), tok(SSD_HEADS),
                  pl.BlockSpec((SSD_HEADS, SSD_CHUNK), lambda b, c: (0, cur(b, c))),
                  tok(SSD_INNER),
                  _resident((1, SSD_HEADS)), _resident((SSD_HEADS, 1)),
                  _resident((1, SSD_INNER)), _resident((1, SSD_INNER)),
                  _resident(unweave.shape)],
        out_specs=tok(SSD_INNER, done),
        out_shape=jax.ShapeDtypeStruct((TOKENS, SSD_INNER), BF16),
        scratch_shapes=[pltpu.VMEM((SSD_GROUPS, gw, SSD_STATE), F32)]
                       + [pltpu.VMEM((SSD_CHUNK, SSD_INNER), F32),
                          pltpu.VMEM((SSD_CHUNK, LANES), F32)] * 2,
        compiler_params=_params("arbitrary", "arbitrary"),
        name="ssd_scan",
    )(xbc, dt, dtT, zs, alog_row, alog_col, dskip, nw, unweave)


def _out_kernel(x_ref, mod_ref, o1_ref, l1_ref, o4_ref, l4_ref, o16_ref, l16_ref,
                yn_ref, expand_ref, restore_ref, n2w_ref, wg_ref, watt_ref, wssd_ref, wmix_ref,
                n3w_ref, wgu_ref, wd_ref, fnw_ref, out_ref):
    tm = x_ref.shape[0]
    mod = mod_ref[0]
    expand = lambda a: jnp.dot(a.astype(BF16), expand_ref[...], preferred_element_type=F32)
    rows = lambda ref: ref[...].reshape(tm, ref.shape[-1])

    def mix(lse_a, o_a, lse_b, o_b):
        m = jnp.maximum(lse_a, lse_b)
        e_a, e_b = jnp.exp(lse_a - m), jnp.exp(lse_b - m)
        tot = e_a + e_b
        inv = 1.0 / tot
        return m + jnp.log(tot), expand(e_a * inv) * o_a + expand(e_b * inv) * o_b

    lse_g, o_g = mix(rows(l4_ref), rows(o4_ref).astype(F32), rows(l16_ref), rows(o16_ref).astype(F32))
    restore = restore_ref[...]
    o_g = jnp.dot(restore, o_g.astype(BF16), preferred_element_type=F32)
    lse_rest, lse_g_tok = lse_g, None
    for _ in range(3):
        piece = lse_rest.astype(BF16)
        moved = jnp.dot(restore, piece, preferred_element_type=F32)
        lse_g_tok = moved if lse_g_tok is None else lse_g_tok + moved
        lse_rest = lse_rest - piece.astype(F32)
    _, o_att = mix(l1_ref[...], o1_ref[...].astype(F32), lse_g_tok, o_g)

    y_att = jnp.dot(o_att.astype(BF16), watt_ref[...], preferred_element_type=F32)
    y_ssd = jnp.dot(yn_ref[...], wssd_ref[...], preferred_element_type=F32)
    x1 = x_ref[...]
    h_mix = _rms_mod(x1, n2w_ref[...], mod[3:4], mod[4:5]).astype(BF16)
    gates = _sigmoid(jnp.dot(h_mix, wg_ref[...], preferred_element_type=F32))
    merged = gates[:, :D_MODEL] * y_att + gates[:, D_MODEL:] * y_ssd
    y = jnp.dot(merged.astype(BF16), wmix_ref[...], preferred_element_type=F32)
    x2 = x1 + mod[5:6] * y
    h = _rms_mod(x2, n3w_ref[...], mod[6:7], mod[7:8]).astype(BF16)
    x3 = x2 + (0.5 * mod[8:9]) * _swiglu(h, wgu_ref, wd_ref)
    out_ref[...] = _rms(x3, fnw_ref[...])


def _out(x1, mod, o1, l1, o4, l4, o16, l16, yn, expand, restore, n2w, wg,
         watt, wssd, wmix, n3w, wgu, wd, fnw):
    tm = TOKEN_TILE
    tiles = SEQ // tm
    grouped = lambda w: pl.BlockSpec((None, None, REGROUP, tm // REGROUP, w),
                                     lambda i: (i // tiles, i % tiles, 0, 0, 0))
    return pl.pallas_call(
        _out_kernel,
        grid=(TOKENS // tm,),
        in_specs=[_tile_spec(D_MODEL), _mod_spec(), _tile_spec(ATT_WIDTH), _tile_spec(LANES),
                  grouped(ATT_WIDTH), grouped(LANES), grouped(ATT_WIDTH), grouped(LANES),
                  _tile_spec(SSD_INNER)]
                 + [_resident(a.shape) for a in (expand, restore, n2w, wg, watt, wssd, wmix, n3w,
                                                 wgu, wd, fnw)],
        out_specs=_tile_spec(D_MODEL),
        out_shape=jax.ShapeDtypeStruct((TOKENS, D_MODEL), F32),
        compiler_params=_params("arbitrary"),
        name="mix_out_ffn2",
    )(x1, mod, o1, l1, o4, l4, o16, l16, yn, expand, restore, n2w, wg,
      watt, wssd, wmix, n3w, wgu, wd, fnw)


def _rope_tables():
    inv_freq = ROPE_THETA ** (-jnp.arange(0, ROPE_DIM, 2, dtype=F32) / ROPE_DIM)
    d = np.arange(LANES) % ATT_HEAD_DIM
    half = ROPE_DIM // 2
    invf = jnp.where(d < ROPE_DIM, inv_freq[d % half], 0.0).astype(F32).reshape(1, LANES)
    sgn = np.where(d < half, -1.0, np.where(d < ROPE_DIM, 1.0, 0.0)).astype(np.float32)
    return invf, jnp.asarray(sgn).reshape(1, LANES)


def _regroup_matrix(tm):
    rho = np.arange(tm)
    src = REGROUP * (rho % (tm // REGROUP)) + rho // (tm // REGROUP)
    return (np.arange(tm)[None, :] == src[:, None]).astype(np.float32)


def _weave_matrix(tm):
    rho = np.arange(tm)
    src = (rho // SSD_CHUNK) * SSD_CHUNK + _woven_time(rho % SSD_CHUNK)
    return (np.arange(tm)[None, :] == src[:, None]).astype(np.float32)


def kernel(x, c, positions, w_ada, b_ada, norm1_w, ffn1_w_gu, ffn1_w_down, norm2_w, w_in,
           conv_w, conv_b, dt_bias, a_log, d_skip, ssd_norm_w, w_att_out, w_ssd_out,
           w_mix_out, norm3_w, ffn2_w_gu, ffn2_w_down, final_norm_w):
    l = 0
    row = lambda t: t.reshape(1, -1).astype(F32)
    xf = x.reshape(TOKENS, D_MODEL)
    mod = _ada(c, w_ada, b_ada, l).reshape(BATCH, N_MOD, D_MODEL)

    x1 = _ffn1(xf, mod, row(norm1_w[l]), ffn1_w_gu[l].astype(BF16), ffn1_w_down[l].astype(BF16))

    o_z = 3 * ATT_WIDTH
    o_xbc = o_z + SSD_INNER
    o_dt = o_xbc + SSD_CONV_CH
    o_g = o_dt + SSD_HEADS
    w = w_in[l]
    w_dt = w[:, o_dt:o_g]
    invf, sgn = _rope_tables()
    perm = _regroup_matrix(TOKEN_TILE)
    posf = jnp.broadcast_to(positions.reshape(TOKENS, 1).astype(F32), (TOKENS, LANES))
    qkv, qkvg, zs, xbc, dt, dtT = _inproj(
        x1, mod, row(norm2_w[l]), posf, invf, sgn, jnp.asarray(perm, BF16),
        jnp.asarray(_weave_matrix(TOKEN_TILE), BF16),
        w[:, :o_z].astype(BF16), w[:, o_z:o_xbc].astype(BF16), w[:, o_xbc:o_dt].astype(BF16),
        jnp.pad(w_dt, ((0, 0), (0, LANES - SSD_HEADS))).astype(BF16), w_dt.T.astype(BF16),
        conv_w[l], row(conv_b[l]), row(dt_bias[l]), dt_bias[l].reshape(SSD_HEADS, 1))

    o1, l1 = _attention_natural(qkv)
    o4, l4 = _attention_regrouped(qkvg, 4)
    o16, l16 = _attention_regrouped(qkvg, 16)

    yn = _ssd(xbc, dt, dtT, zs, row(a_log[l]), a_log[l].reshape(SSD_HEADS, 1),
              row(jnp.repeat(d_skip[l], SSD_HEAD_DIM)), row(ssd_norm_w[l]),
              jnp.asarray(_weave_matrix(SSD_CHUNK).T, BF16))

    head_of_col = np.arange(ATT_WIDTH) // ATT_HEAD_DIM
    expand = jnp.asarray(np.arange(LANES)[:, None] == head_of_col[None, :], BF16)
    out = _out(x1, mod, o1, l1, o4, l4, o16, l16, yn, expand, jnp.asarray(perm.T, BF16),
               row(norm2_w[l]), w[:, o_g:].astype(BF16), w_att_out[l].astype(BF16), w_ssd_out[l].astype(BF16), w_mix_out[l].astype(BF16),
               row(norm3_w[l]), ffn2_w_gu[l].astype(BF16), ffn2_w_down[l].astype(BF16),
               row(final_norm_w))
    return out.reshape(BATCH, SEQ, D_MODEL)
```

```python
import jax
import jax.numpy as jnp
import numpy as np
from jax import lax
from jax.experimental import pallas as pl
from jax.experimental.pallas import tpu as pltpu

F32 = jnp.float32
BF16 = jnp.bfloat16

D_MODEL = 1024
BATCH = 2
SEQ = 8192
TOKENS = BATCH * SEQ
ATT_HEADS = 12
ATT_HEAD_DIM = 64
ATT_WIDTH = ATT_HEADS * ATT_HEAD_DIM
ROPE_DIM = ATT_HEAD_DIM // 4
ROPE_THETA = 500000.0
ATT_BLOCK = 128
SSD_INNER = 2 * D_MODEL
SSD_HEAD_DIM = 64
SSD_HEADS = SSD_INNER // SSD_HEAD_DIM
SSD_GROUPS = 8
SSD_STATE = 128
SSD_CONV = 4
SSD_CHUNK = 128
SSD_CONV_CH = SSD_INNER + 2 * SSD_GROUPS * SSD_STATE
D_FF = 2816
N_MOD = 9
NORM_EPS = 1e-6

LANES = 128
SUBLANES = 8
VMEM_LIMIT = 56 * 1024 * 1024
TOKEN_TILE = 256
COL_CHUNK = 256
PROJ_CHUNK = 512
REGROUP = 16
ATT_STEP_BLOCKS = 4
ADA_COL_TILE = 1536
NEG_BIG = -1e30

_NT = (((1,), (1,)), ((), ()))


def _params(*sem):
    return pltpu.CompilerParams(dimension_semantics=sem, vmem_limit_bytes=VMEM_LIMIT)


def _resident(shape):
    zeros = (0,) * len(shape)
    return pl.BlockSpec(shape, lambda *_: zeros, pipeline_mode=pl.Buffered(1))


def _sigmoid(x):
    return 0.5 * jnp.tanh(0.5 * x) + 0.5


def _silu(x):
    return x * _sigmoid(x)


def _rms(x, w):
    ms = jnp.sum(x * x, axis=-1, keepdims=True) * (1.0 / x.shape[-1])
    return x * lax.rsqrt(ms + NORM_EPS) * w


def _rms_mod(x, w, shift, scale):
    return _rms(x, w) * (1.0 + scale) + shift


def _swiglu(h, wgu_ref, wd_ref):
    gu = jnp.dot(h, wgu_ref[...], preferred_element_type=F32)
    a = (_silu(gu[:, :D_FF]) * gu[:, D_FF:]).astype(BF16)
    return jnp.dot(a, wd_ref[...], preferred_element_type=F32)


def _ada_kernel(ct_ref, w_ref, b_ref, o_ref):
    act = _silu(ct_ref[...])
    w = w_ref[...]
    rows = [jnp.sum(w * act[:, b:b + 1], axis=0, keepdims=True) for b in range(BATCH)]
    o_ref[...] = jnp.concatenate(rows, axis=0) + b_ref[...]


def _ada(c, w_ada, b_ada, layer):
    n = N_MOD * D_MODEL
    return pl.pallas_call(
        _ada_kernel,
        grid=(n // ADA_COL_TILE,),
        in_specs=[
            pl.BlockSpec((D_MODEL, BATCH), lambda j: (0, 0)),
            pl.BlockSpec((None, D_MODEL, ADA_COL_TILE), lambda j: (layer, 0, j)),
            pl.BlockSpec((None, 1, ADA_COL_TILE), lambda j: (layer, 0, j)),
        ],
        out_specs=pl.BlockSpec((BATCH, ADA_COL_TILE), lambda j: (0, j)),
        out_shape=jax.ShapeDtypeStruct((BATCH, n), F32),
        compiler_params=_params("arbitrary"),
        name="ada_mod",
    )(c.T, w_ada, b_ada.reshape(-1, 1, n))


def _ffn1_kernel(x_ref, mod_ref, nw_ref, wgu_ref, wd_ref, o_ref):
    x = x_ref[...]
    mod = mod_ref[0]
    h = _rms_mod(x, nw_ref[...], mod[0:1], mod[1:2]).astype(BF16)
    o_ref[...] = x + (0.5 * mod[2:3]) * _swiglu(h, wgu_ref, wd_ref)


def _tile_spec(width, tm=TOKEN_TILE):
    return pl.BlockSpec((tm, width), lambda i: (i, 0))


def _mod_spec(tm=TOKEN_TILE):
    per_batch = SEQ // tm
    return pl.BlockSpec((1, N_MOD, D_MODEL), lambda i: (i // per_batch, 0, 0))


def _ffn1(x, mod, nw, wgu, wd):
    return pl.pallas_call(
        _ffn1_kernel,
        grid=(TOKENS // TOKEN_TILE,),
        in_specs=[_tile_spec(D_MODEL), _mod_spec(), _resident((1, D_MODEL)),
                  _resident(wgu.shape), _resident(wd.shape)],
        out_specs=_tile_spec(D_MODEL),
        out_shape=jax.ShapeDtypeStruct((TOKENS, D_MODEL), F32),
        compiler_params=_params("arbitrary"),
        name="ffn1",
    )(x, mod, nw, wgu, wd)


def _softplus(x):
    return jnp.maximum(x, 0.0) + jnp.log1p(jnp.exp(-jnp.abs(x)))


def _inproj_kernel(*refs):
    *io_refs, raw_a_ref, raw_b_ref = refs
    x_ref, mod_ref, nw_ref = io_refs[:3]
    weave_ref = io_refs[7]
    h_ref, hw_ref, hist_ref, _, _, qkv_ref, kv_prev_ref = io_refs[-7:]
    step = pl.program_id(0)
    tm = x_ref.shape[0]
    tiles_per_seq = SEQ // tm
    n_tiles = pl.num_programs(0) - 1

    @pl.when(step == 0)
    def _():
        raw_b_ref[...] = jnp.zeros_like(raw_b_ref)
        kv_prev_ref[...] = jnp.zeros_like(kv_prev_ref)

    @pl.when((step == 0) | (step % tiles_per_seq == 1))
    def _():
        hist_ref[...] = jnp.zeros_like(hist_ref)

    mod = mod_ref[0]
    h_ref[...] = _rms_mod(x_ref[...], nw_ref[...], mod[3:4], mod[4:5]).astype(BF16)
    hw_ref[...] = jnp.dot(weave_ref[...], h_ref[...], preferred_element_type=F32).astype(BF16)

    @pl.when(step % 2 == 0)
    def _():
        _inproj_step(*io_refs, raw_a_ref, raw_b_ref)

    @pl.when(step % 2 == 1)
    def _():
        _inproj_step(*io_refs, raw_b_ref, raw_a_ref)

    @pl.when(step < n_tiles - 1)
    def _():
        for part in range(2):
            cols = slice((part + 1) * ATT_WIDTH, (part + 2) * ATT_WIDTH)
            kv_prev_ref[part] = qkv_ref[tm - ATT_BLOCK:, cols]


def _inproj_step(x_ref, mod_ref, nw_ref, pos_ref, invf_ref, sgn_ref, perm_ref, weave_ref,
                 band_ref, wqkv_ref, wz_ref, wxbc_ref, wdt_ref, wdtT_ref,
                 convw_ref, convb_ref, dtb_row_ref, dtb_col_ref,
                 o1_ref, l1_ref, qkvg_ref, zs_ref, xbc_ref, dt_ref, dtT_ref,
                 h_ref, hw_ref, hist_ref, cos_ref, sin_ref, qkv_ref, kv_prev_ref,
                 raw_out_ref, raw_in_ref):
    tm = x_ref.shape[0]
    tile = jnp.minimum(pl.program_id(0), pl.num_programs(0) - 2)
    has_prev = tile % (SEQ // tm) != 0

    def xbc_part(cols):
        raw_out_ref[:, cols] = jnp.dot(hw_ref[...], wxbc_ref[:, cols], preferred_element_type=F32)

    def rope_tables():
        ang = pos_ref[...] * invf_ref[...]
        cos_ref[...] = jnp.cos(ang)
        sin_ref[...] = jnp.sin(ang) * sgn_ref[...]

    def rope(t):
        reps = ATT_WIDTH // LANES
        cos = jnp.concatenate([cos_ref[...]] * reps, axis=1)
        sin = jnp.concatenate([sin_ref[...]] * reps, axis=1)
        lane = lax.broadcasted_iota(jnp.int32, (tm, ATT_WIDTH), 1)
        first_half = (lane & (ATT_HEAD_DIM - 1)) < (ROPE_DIM // 2)
        partner = jnp.where(first_half,
                            pltpu.roll(t, ATT_WIDTH - ROPE_DIM // 2, 1),
                            pltpu.roll(t, ROPE_DIM // 2, 1))
        return t * cos + partner * sin

    def qkv_part(idx):
        cols = slice(idx * ATT_WIDTH, (idx + 1) * ATT_WIDTH)
        t = jnp.dot(h_ref[...], wqkv_ref[:, cols], preferred_element_type=F32)
        if idx == 0:
            t = rope(t) * (ATT_HEAD_DIM ** -0.5)
        elif idx == 1:
            t = rope(t)
        t = t.astype(BF16)
        qkv_ref[:, cols] = t
        regrouped = jnp.dot(perm_ref[...], t, preferred_element_type=F32).astype(BF16)
        qkvg_ref[:, :, cols] = regrouped.reshape(qkvg_ref.shape[:2] + (ATT_WIDTH,))

    def z_part(cols):
        z = jnp.dot(hw_ref[...], wz_ref[:, cols], preferred_element_type=F32)
        zs_ref[:, cols] = _silu(z).astype(BF16)

    def dt_part():
        hw = hw_ref[...]
        dt_raw = jnp.dot(hw, wdt_ref[...], preferred_element_type=F32)[:, :SSD_HEADS]
        dt_ref[...] = _softplus(dt_raw + dtb_row_ref[...])
        dtT_raw = lax.dot_general(wdtT_ref[...], hw, _NT, preferred_element_type=F32)
        dtT_ref[...] = _softplus(dtT_raw + dtb_col_ref[...])

    tail_rows = (SSD_CONV - 1) * SUBLANES
    tail_start = SSD_CHUNK - tail_rows
    first_sublane = lax.broadcasted_iota(jnp.int32, (SUBLANES, COL_CHUNK), 0) == 0

    def conv_chunk(src):
        raw = raw_in_ref[:, src]
        taps = [convw_ref[k:k + 1, src] for k in range(SSD_CONV)]
        prev_tail = hist_ref[:, src]
        pieces = []
        for ck in range(tm // SSD_CHUNK):
            cur = raw[ck * SSD_CHUNK:(ck + 1) * SSD_CHUNK]
            cur_tail = cur[tail_start:]
            wrapped = []
            for k in range(SSD_CONV - 1):
                rows = slice(k * SUBLANES, (k + 1) * SUBLANES)
                wrapped.append(jnp.where(first_sublane, pltpu.roll(prev_tail[rows], 1, 0),
                                         pltpu.roll(cur_tail[rows], 1, 0)))
            ext = jnp.concatenate(wrapped + [cur], axis=0)
            acc = cur * taps[SSD_CONV - 1] + convb_ref[:, src]
            for s in range(1, SSD_CONV):
                lo = (SSD_CONV - 1 - s) * SUBLANES
                acc = acc + ext[lo:lo + SSD_CHUNK] * taps[SSD_CONV - 1 - s]
            pieces.append(acc)
            prev_tail = cur_tail
        hist_ref[:, src] = prev_tail
        xbc_ref[:, src] = _silu(jnp.concatenate(pieces, axis=0)).astype(BF16)

    blk = ATT_BLOCK
    k_at, v_at = ATT_WIDTH, 2 * ATT_WIDTH

    def with_prev(j, part, at, cols):
        cur_cols = slice(at + cols.start, at + cols.stop)
        if j == 0:
            return jnp.concatenate([kv_prev_ref[part, :, cols], qkv_ref[:blk, cur_cols]], axis=0)
        return qkv_ref[(j - 1) * blk:(j + 1) * blk, cur_cols]

    def emit(j, o, lse):
        o1_ref[j * blk:(j + 1) * blk, :] = o
        l1_ref[j * blk:(j + 1) * blk, :] = lse

    band = band_ref[...]
    attention = _attention_units(lambda j, cols: qkv_ref[j * blk:(j + 1) * blk, cols],
                                 lambda j, cols: with_prev(j, 0, k_at, cols),
                                 lambda j, cols: with_prev(j, 1, v_at, cols),
                                 band, _first_block_bias(band, has_prev), tm // blk, emit)

    chunks = lambda width, w: [pl.ds(c * w, w) for c in range(width // w)]
    conv = chunks(SSD_CONV_CH, COL_CHUNK)
    for fn, args in ((qkv_part, (2,)), (rope_tables, ()), (qkv_part, (0,)), (qkv_part, (1,))):
        fn(*args)
        conv_chunk(conv.pop(0))
    parts = ([(xbc_part, (c,)) for c in chunks(SSD_CONV_CH, PROJ_CHUNK)]
             + [(z_part, (c,)) for c in chunks(SSD_INNER, PROJ_CHUNK)] + [(dt_part, ())])
    per_part = -(-len(attention) // len(parts))
    for fn, args in parts:
        fn(*args)
        if conv:
            conv_chunk(conv.pop(0))
        for unit in attention[:per_part]:
            unit()
        attention = attention[per_part:]
    assert not conv and not attention


def _inproj(x1, mod, nw, posf, invf, sgn, perm, weave, band, wqkv, wz, wxbc, wdt, wdtT,
            conv_w, conv_b, dtb_row, dtb_col):
    tm = TOKEN_TILE
    tiles = SEQ // tm
    n_tiles = TOKENS // tm
    cur = lambda i: jnp.minimum(i, n_tiles - 1)
    done = lambda i: jnp.maximum(i - 1, 0)
    tok = lambda w, d, at=cur: (jax.ShapeDtypeStruct((TOKENS, w), d),
                                pl.BlockSpec((tm, w), lambda i: (at(i), 0)))
    grouped = (jax.ShapeDtypeStruct((BATCH, tiles, REGROUP, tm // REGROUP, 3 * ATT_WIDTH), BF16),
               pl.BlockSpec((None, None, REGROUP, tm // REGROUP, 3 * ATT_WIDTH),
                            lambda i: (cur(i) // tiles, cur(i) % tiles, 0, 0, 0)))
    outs = [tok(ATT_WIDTH, BF16), tok(LANES, F32),
        grouped,
        tok(SSD_INNER, BF16),
        tok(SSD_CONV_CH, BF16, done),
        tok(SSD_HEADS, F32),
        (jax.ShapeDtypeStruct((SSD_HEADS, TOKENS), F32),
         pl.BlockSpec((SSD_HEADS, tm), lambda i: (0, cur(i)))),
    ]
    return pl.pallas_call(
        _inproj_kernel,
        grid=(n_tiles + 1,),
        in_specs=[tok(D_MODEL, F32)[1],
                  pl.BlockSpec((1, N_MOD, D_MODEL), lambda i: (cur(i) // tiles, 0, 0)),
                  _resident((1, D_MODEL)),
                  tok(LANES, F32)[1], _resident((1, LANES)), _resident((1, LANES)),
                  _resident(perm.shape), _resident(weave.shape), _resident(band.shape),
                  _resident(wqkv.shape), _resident(wz.shape), _resident(wxbc.shape),
                  _resident(wdt.shape), _resident(wdtT.shape),
                  _resident(conv_w.shape), _resident(conv_b.shape),
                  _resident(dtb_row.shape), _resident(dtb_col.shape)],
        out_specs=[s for _, s in outs],
        out_shape=[s for s, _ in outs],
        scratch_shapes=[pltpu.VMEM((tm, D_MODEL), BF16), pltpu.VMEM((tm, D_MODEL), BF16),
                        pltpu.VMEM(((SSD_CONV - 1) * SUBLANES, SSD_CONV_CH), F32),
                        pltpu.VMEM((tm, LANES), F32), pltpu.VMEM((tm, LANES), F32),
                        pltpu.VMEM((tm, 3 * ATT_WIDTH), BF16),
                        pltpu.VMEM((2, ATT_BLOCK, ATT_WIDTH), BF16),
                        pltpu.VMEM((tm, SSD_CONV_CH), F32), pltpu.VMEM((tm, SSD_CONV_CH), F32)],
        compiler_params=_params("arbitrary"),
        name="in_proj",
    )(x1, mod, nw, posf, invf, sgn, perm, weave, band, wqkv, wz, wxbc, wdt, wdtT,
      conv_w, conv_b, dtb_row, dtb_col)


def _band_bias(row_pos):
    blk = ATT_BLOCK
    pos = row_pos(np.arange(blk))
    k_pos = np.concatenate([pos, pos + blk])
    dist = pos[:, None] + blk - k_pos[None, :]
    return jnp.asarray(np.where((dist >= 0) & (dist <= blk), 0.0, NEG_BIG), F32)


def _first_block_bias(band, has_prev):
    ki = lax.broadcasted_iota(jnp.int32, band.shape, 1)
    return jnp.where((ki >= ATT_BLOCK) | has_prev, band, NEG_BIG)


def _attention_units(get_q, get_k, get_v, band, first_bias, n_blocks, emit):
    blk = ATT_BLOCK
    lane_row = lax.broadcasted_iota(jnp.int32, (1, LANES), 1)
    lane = lax.broadcasted_iota(jnp.int32, (blk, LANES), 1)
    low = lane < ATT_HEAD_DIM
    head_mask = [(lane_row < ATT_HEAD_DIM).astype(BF16), (lane_row >= ATT_HEAD_DIM).astype(BF16)]
    units = [(j, pair, half) for j in range(n_blocks)
             for pair in range(ATT_HEADS // 2) for half in range(2)]
    carry = {"scores": None, "outs": [], "o_pairs": [], "lse": jnp.zeros((blk, LANES), F32)}

    def scores(unit):
        j, pair, half = unit
        cols = slice(pair * LANES, (pair + 1) * LANES)
        s = lax.dot_general(get_q(j, cols) * head_mask[half], get_k(j, cols), _NT,
                            preferred_element_type=F32)
        return s + (first_bias if j == 0 else band)

    def run(n):
        j, pair, half = units[n]
        s = scores(units[0]) if n == 0 else carry["scores"]
        if n + 1 < len(units):
            carry["scores"] = scores(units[n + 1])
        m = jnp.max(s, axis=1, keepdims=True)
        p = jnp.exp(s - m)
        den = jnp.sum(p, axis=1, keepdims=True)
        o = jnp.dot(p.astype(BF16), get_v(j, slice(pair * LANES, (pair + 1) * LANES)),
                    preferred_element_type=F32)
        carry["outs"].append(o * (1.0 / den))
        carry["lse"] = jnp.where(lane == 2 * pair + half, m + jnp.log(den), carry["lse"])
        if half == 1:
            outs = carry["outs"]
            carry["o_pairs"].append(jnp.where(low, outs[0], outs[1]).astype(BF16))
            carry["outs"] = []
            if pair == ATT_HEADS // 2 - 1:
                emit(j, jnp.concatenate(carry["o_pairs"], axis=1), carry["lse"])
                carry["o_pairs"], carry["lse"] = [], jnp.zeros((blk, LANES), F32)

    return [lambda n=n: run(n) for n in range(len(units))]


def _attn_kernel(band_ref, q_ref, kp_ref, kc_ref, vp_ref, vc_ref, o_ref, lse_ref):
    blk = ATT_BLOCK
    flat = lambda ref: ref[...].reshape(-1, ref.shape[-1])
    q = flat(q_ref)
    k = jnp.concatenate([flat(kp_ref), flat(kc_ref)], axis=0)
    v = jnp.concatenate([flat(vp_ref), flat(vc_ref)], axis=0)
    band = band_ref[...]
    o_blocks, lse_blocks = [], []

    def emit(j, o, lse):
        o_blocks.append(o)
        lse_blocks.append(lse)

    for unit in _attention_units(lambda j, cols: q[j * blk:(j + 1) * blk, cols],
                                 lambda j, cols: k[j * blk:(j + 2) * blk, cols],
                                 lambda j, cols: v[j * blk:(j + 2) * blk, cols],
                                 band, _first_block_bias(band, pl.program_id(2) > 0),
                                 ATT_STEP_BLOCKS, emit):
        unit()
    o_ref[...] = jnp.concatenate(o_blocks, axis=0).reshape(o_ref.shape)
    lse_ref[...] = jnp.concatenate(lse_blocks, axis=0).reshape(lse_ref.shape)


def _attention_regrouped(qkvg, r):
    tiles, rows = qkvg.shape[1], qkvg.shape[3]
    fold = REGROUP // r
    tiles_per_blk = ATT_BLOCK // (fold * rows)
    n_blk = tiles // (ATT_STEP_BLOCKS * tiles_per_blk)
    shape6 = lambda w: (BATCH, tiles, fold, r, rows, w)
    qkvg = qkvg.reshape(shape6(3 * ATT_WIDTH))
    blk6 = lambda n_tiles, w: (None, n_tiles, fold, None, rows, w)
    cur = lambda w, part=0: pl.BlockSpec(blk6(ATT_STEP_BLOCKS * tiles_per_blk, w),
                                         lambda b, j, i: (b, i, 0, j, 0, part))
    prev = lambda part: pl.BlockSpec(
        blk6(tiles_per_blk, ATT_WIDTH),
        lambda b, j, i: (b, jnp.maximum(ATT_STEP_BLOCKS * i - 1, 0), 0, j, 0, part))
    per_tile = fold * rows

    def row_pos(rho):
        t = rho // per_tile
        a = (rho // rows) % fold
        return t * per_tile + (rho % rows) * fold + a

    o, lse = pl.pallas_call(
        _attn_kernel,
        grid=(BATCH, r, n_blk),
        in_specs=[_resident((ATT_BLOCK, 2 * ATT_BLOCK)), cur(ATT_WIDTH, 0),
                  prev(1), cur(ATT_WIDTH, 1), prev(2), cur(ATT_WIDTH, 2)],
        out_specs=[cur(ATT_WIDTH), cur(LANES)],
        out_shape=[jax.ShapeDtypeStruct(shape6(ATT_WIDTH), BF16),
                   jax.ShapeDtypeStruct(shape6(LANES), F32)],
        compiler_params=_params("arbitrary", "arbitrary", "arbitrary"),
        name=f"attn_dil{r}",
    )(_band_bias(row_pos), qkvg, qkvg, qkvg, qkvg, qkvg)
    grouped5 = (BATCH, tiles, REGROUP, rows)
    return o.reshape(grouped5 + (ATT_WIDTH,)), lse.reshape(grouped5 + (LANES,))


def _woven_time(i):
    return ((i & (SUBLANES - 1)) << 4) | (i >> 3)


def _ssd_kernel(*refs):
    *io_refs, y_a_ref, sq_a_ref, y_b_ref, sq_b_ref = refs
    state_ref = io_refs[-1]
    first = (pl.program_id(0) == 0) & (pl.program_id(1) == 0)

    @pl.when(first)
    def _():
        y_b_ref[...] = jnp.zeros_like(y_b_ref)
        sq_b_ref[...] = jnp.zeros_like(sq_b_ref)

    @pl.when(pl.program_id(1) == 0)
    def _():
        state_ref[...] = jnp.zeros_like(state_ref)

    @pl.when(pl.program_id(1) % 2 == 0)
    def _():
        _ssd_step(*io_refs, y_a_ref, sq_a_ref, y_b_ref, sq_b_ref)

    @pl.when(pl.program_id(1) % 2 == 1)
    def _():
        _ssd_step(*io_refs, y_b_ref, sq_b_ref, y_a_ref, sq_a_ref)


def _ssd_step(xbc_ref, dt_ref, dtT_ref, zs_ref,
              alog_row_ref, alog_col_ref, dskip_ref, nw_ref, unweave_ref,
              o_ref, state_ref, y_ref, sq_ref, y_done_ref, sq_done_ref):
    q = SSD_CHUNK
    heads = SSD_HEADS // SSD_GROUPS
    gw = heads * SSD_HEAD_DIM

    dtT = dtT_ref[...]
    a_dt_col = dt_ref[...] * (-jnp.exp(alog_row_ref[...]))
    a_dt_row = dtT * (-jnp.exp(alog_col_ref[...]))
    t_row = _woven_time(lax.broadcasted_iota(jnp.int32, (q, q), 0))
    t_col = _woven_time(lax.broadcasted_iota(jnp.int32, (q, q), 1))
    causal = t_row >= t_col
    hi = lax.Precision.HIGHEST
    cs_col = jnp.dot(causal.astype(F32), a_dt_col, precision=hi, preferred_element_type=F32)
    cs_row = jnp.dot(a_dt_row, (t_row <= t_col).astype(F32), precision=hi,
                     preferred_element_type=F32)
    total = cs_row[:, q - 1:q]
    dt_decay_row = cs_row - jnp.log(dtT)
    to_end = dtT * jnp.exp(total - cs_row)
    chunk_decay = jnp.exp(total)
    r2 = lax.broadcasted_iota(jnp.int32, (gw, gw), 0)
    c2 = lax.broadcasted_iota(jnp.int32, (gw, gw), 1)
    eye = (r2 == c2).astype(BF16)
    no_rows = jnp.zeros((SSD_HEAD_DIM, q + SSD_STATE), BF16)
    sq_sum = jnp.zeros((q, LANES), F32)

    def load(g):
        b_at = SSD_INNER + g * SSD_STATE
        c_at = b_at + SSD_GROUPS * SSD_STATE
        b_g = xbc_ref[:, b_at:b_at + SSD_STATE]
        c_g = xbc_ref[:, c_at:c_at + SSD_STATE]
        cb = lax.dot_general(c_g, b_g, _NT, preferred_element_type=F32)
        x_g = xbc_ref[:, g * gw:(g + 1) * gw]
        x_t = lax.dot_general(eye, x_g, _NT, preferred_element_type=F32)
        return b_g, c_g, cb, x_g, x_t

    def weights(g, loaded):
        _, c_g, cb, _, _ = loaded
        c_f = c_g.astype(F32)
        w_parts = []
        for j in range(heads):
            h = g * heads + j
            cs_l = jnp.broadcast_to(cs_col[:, h:h + 1], (q, q))
            w_parts.append(jnp.where(causal, cb * jnp.exp(cs_l - dt_decay_row[h:h + 1]), 0.0)
                           .astype(BF16))
            w_parts.append((c_f * jnp.exp(cs_l)).astype(BF16))
        return jnp.concatenate(w_parts, axis=1)

    def finish(g, loaded, w_all, sq_sum):
        b_g, _, _, x_g, x_t = loaded
        state = state_ref[g]
        rhs = jnp.concatenate([x_t.astype(BF16), state.astype(BF16)], axis=1)
        rhs_rows, end_rows, decay_rows = [], [], []
        for j in range(heads):
            h = g * heads + j
            mine = rhs[j * SSD_HEAD_DIM:(j + 1) * SSD_HEAD_DIM]
            rhs_rows.append(jnp.concatenate([mine if jj == j else no_rows for jj in range(heads)],
                                            axis=1))
            end_rows.append(jnp.broadcast_to(to_end[h:h + 1], (SSD_HEAD_DIM, q)))
            decay_rows.append(jnp.broadcast_to(chunk_decay[h:h + 1], (SSD_HEAD_DIM, SSD_STATE)))
        y = lax.dot_general(w_all, jnp.concatenate(rhs_rows, axis=0), _NT,
                            preferred_element_type=F32)
        xcols = slice(g * gw, (g + 1) * gw)
        y = (y + dskip_ref[:, xcols] * x_g.astype(F32)) * zs_ref[:, xcols].astype(F32)
        y_ref[:, xcols] = y
        sq = y * y
        upd = jnp.dot((x_t * jnp.concatenate(end_rows, axis=0)).astype(BF16), b_g,
                      preferred_element_type=F32)
        state_ref[g] = state * jnp.concatenate(decay_rows, axis=0) + upd
        return sq_sum + sq[:, :LANES] + sq[:, LANES:]

    ms = jnp.sum(sq_done_ref[...], axis=-1, keepdims=True) * (1.0 / SSD_INNER)
    inv_rms = lax.rsqrt(ms + NORM_EPS)

    def emit(g):
        cols = slice(g * gw, (g + 1) * gw)
        y = (y_done_ref[:, cols] * inv_rms * nw_ref[:, cols]).astype(BF16)
        o_ref[:, cols] = jnp.dot(unweave_ref[...], y, preferred_element_type=F32).astype(BF16)

    loaded = {0: load(0)}
    w_all = {0: weights(0, loaded[0])}
    for g in range(SSD_GROUPS):
        if g + 1 < SSD_GROUPS:
            loaded[g + 1] = load(g + 1)
            w_all[g + 1] = weights(g + 1, loaded[g + 1])
        emit(g)
        sq_sum = finish(g, loaded.pop(g), w_all.pop(g), sq_sum)
    sq_ref[...] = sq_sum


def _ssd(xbc, dt, dtT, zs, alog_row, alog_col, dskip, nw, unweave):
    n_chunks = SEQ // SSD_CHUNK
    cur = lambda b, c: b * n_chunks + jnp.minimum(c, n_chunks - 1)
    done = lambda b, c: b * n_chunks + jnp.maximum(c - 1, 0)
    tok = lambda w, at=cur: pl.BlockSpec((SSD_CHUNK, w), lambda b, c: (at(b, c), 0))
    gw = SSD_INNER // SSD_GROUPS
    return pl.pallas_call(
        _ssd_kernel,
        grid=(BATCH, n_chunks + 1),
        in_specs=[tok(SSD_CONV_CH), tok(SSD_HEADS),
                  pl.BlockSpec((SSD_HEADS, SSD_CHUNK), lambda b, c: (0, cur(b, c))),
                  tok(SSD_INNER),
                  _resident((1, SSD_HEADS)), _resident((SSD_HEADS, 1)),
                  _resident((1, SSD_INNER)), _resident((1, SSD_INNER)),
                  _resident(unweave.shape)],
        out_specs=tok(SSD_INNER, done),
        out_shape=jax.ShapeDtypeStruct((TOKENS, SSD_INNER), BF16),
        scratch_shapes=[pltpu.VMEM((SSD_GROUPS, gw, SSD_STATE), F32)]
                       + [pltpu.VMEM((SSD_CHUNK, SSD_INNER), F32),
                          pltpu.VMEM((SSD_CHUNK, LANES), F32)] * 2,
        compiler_params=_params("arbitrary", "arbitrary"),
        name="ssd_scan",
    )(xbc, dt, dtT, zs, alog_row, alog_col, dskip, nw, unweave)


def _out_kernel(x_ref, mod_ref, o1_ref, l1_ref, o4_ref, l4_ref, o16_ref, l16_ref,
                yn_ref, expand_ref, restore_ref, n2w_ref, wg_ref, watt_ref, wssd_ref, wmix_ref,
                n3w_ref, wgu_ref, wd_ref, fnw_ref, out_ref):
    tm = x_ref.shape[0]
    mod = mod_ref[0]
    expand = lambda a: jnp.dot(a.astype(BF16), expand_ref[...], preferred_element_type=F32)
    rows = lambda ref: ref[...].reshape(tm, ref.shape[-1])

    def mix(lse_a, o_a, lse_b, o_b):
        m = jnp.maximum(lse_a, lse_b)
        e_a, e_b = jnp.exp(lse_a - m), jnp.exp(lse_b - m)
        tot = e_a + e_b
        inv = 1.0 / tot
        return m + jnp.log(tot), expand(e_a * inv) * o_a + expand(e_b * inv) * o_b

    lse_g, o_g = mix(rows(l4_ref), rows(o4_ref).astype(F32), rows(l16_ref), rows(o16_ref).astype(F32))
    restore = restore_ref[...]
    o_g = jnp.dot(restore, o_g.astype(BF16), preferred_element_type=F32)
    lse_rest, lse_g_tok = lse_g, None
    for _ in range(3):
        piece = lse_rest.astype(BF16)
        moved = jnp.dot(restore, piece, preferred_element_type=F32)
        lse_g_tok = moved if lse_g_tok is None else lse_g_tok + moved
        lse_rest = lse_rest - piece.astype(F32)
    _, o_att = mix(l1_ref[...], o1_ref[...].astype(F32), lse_g_tok, o_g)

    y_att = jnp.dot(o_att.astype(BF16), watt_ref[...], preferred_element_type=F32)
    y_ssd = jnp.dot(yn_ref[...], wssd_ref[...], preferred_element_type=F32)
    x1 = x_ref[...]
    h_mix = _rms_mod(x1, n2w_ref[...], mod[3:4], mod[4:5]).astype(BF16)
    gates = _sigmoid(jnp.dot(h_mix, wg_ref[...], preferred_element_type=F32))
    merged = gates[:, :D_MODEL] * y_att + gates[:, D_MODEL:] * y_ssd
    y = jnp.dot(merged.astype(BF16), wmix_ref[...], preferred_element_type=F32)
    x2 = x1 + mod[5:6] * y
    h = _rms_mod(x2, n3w_ref[...], mod[6:7], mod[7:8]).astype(BF16)
    x3 = x2 + (0.5 * mod[8:9]) * _swiglu(h, wgu_ref, wd_ref)
    out_ref[...] = _rms(x3, fnw_ref[...])


def _out(x1, mod, o1, l1, o4, l4, o16, l16, yn, expand, restore, n2w, wg,
         watt, wssd, wmix, n3w, wgu, wd, fnw):
    tm = TOKEN_TILE
    tiles = SEQ // tm
    grouped = lambda w: pl.BlockSpec((None, None, REGROUP, tm // REGROUP, w),
                                     lambda i: (i // tiles, i % tiles, 0, 0, 0))
    return pl.pallas_call(
        _out_kernel,
        grid=(TOKENS // tm,),
        in_specs=[_tile_spec(D_MODEL), _mod_spec(), _tile_spec(ATT_WIDTH), _tile_spec(LANES),
                  grouped(ATT_WIDTH), grouped(LANES), grouped(ATT_WIDTH), grouped(LANES),
                  _tile_spec(SSD_INNER)]
                 + [_resident(a.shape) for a in (expand, restore, n2w, wg, watt, wssd, wmix, n3w,
                                                 wgu, wd, fnw)],
        out_specs=_tile_spec(D_MODEL),
        out_shape=jax.ShapeDtypeStruct((TOKENS, D_MODEL), F32),
        compiler_params=_params("arbitrary"),
        name="mix_out_ffn2",
    )(x1, mod, o1, l1, o4, l4, o16, l16, yn, expand, restore, n2w, wg,
      watt, wssd, wmix, n3w, wgu, wd, fnw)


def _rope_tables():
    inv_freq = ROPE_THETA ** (-jnp.arange(0, ROPE_DIM, 2, dtype=F32) / ROPE_DIM)
    d = np.arange(LANES) % ATT_HEAD_DIM
    half = ROPE_DIM // 2
    invf = jnp.where(d < ROPE_DIM, inv_freq[d % half], 0.0).astype(F32).reshape(1, LANES)
    sgn = np.where(d < half, -1.0, np.where(d < ROPE_DIM, 1.0, 0.0)).astype(np.float32)
    return invf, jnp.asarray(sgn).reshape(1, LANES)


def _regroup_matrix(tm):
    rho = np.arange(tm)
    src = REGROUP * (rho % (tm // REGROUP)) + rho // (tm // REGROUP)
    return (np.arange(tm)[None, :] == src[:, None]).astype(np.float32)


def _weave_matrix(tm):
    rho = np.arange(tm)
    src = (rho // SSD_CHUNK) * SSD_CHUNK + _woven_time(rho % SSD_CHUNK)
    return (np.arange(tm)[None, :] == src[:, None]).astype(np.float32)


def kernel(x, c, positions, w_ada, b_ada, norm1_w, ffn1_w_gu, ffn1_w_down, norm2_w, w_in,
           conv_w, conv_b, dt_bias, a_log, d_skip, ssd_norm_w, w_att_out, w_ssd_out,
           w_mix_out, norm3_w, ffn2_w_gu, ffn2_w_down, final_norm_w):
    l = 0
    row = lambda t: t.reshape(1, -1).astype(F32)
    xf = x.reshape(TOKENS, D_MODEL)
    mod = _ada(c, w_ada, b_ada, l).reshape(BATCH, N_MOD, D_MODEL)

    x1 = _ffn1(xf, mod, row(norm1_w[l]), ffn1_w_gu[l].astype(BF16), ffn1_w_down[l].astype(BF16))

    o_z = 3 * ATT_WIDTH
    o_xbc = o_z + SSD_INNER
    o_dt = o_xbc + SSD_CONV_CH
    o_g = o_dt + SSD_HEADS
    w = w_in[l]
    w_dt = w[:, o_dt:o_g]
    invf, sgn = _rope_tables()
    perm = _regroup_matrix(TOKEN_TILE)
    posf = jnp.broadcast_to(positions.reshape(TOKENS, 1).astype(F32), (TOKENS, LANES))
    o1, l1, qkvg, zs, xbc, dt, dtT = _inproj(
        x1, mod, row(norm2_w[l]), posf, invf, sgn, jnp.asarray(perm, BF16),
        jnp.asarray(_weave_matrix(TOKEN_TILE), BF16), _band_bias(lambda r: r),
        w[:, :o_z].astype(BF16), w[:, o_z:o_xbc].astype(BF16), w[:, o_xbc:o_dt].astype(BF16),
        jnp.pad(w_dt, ((0, 0), (0, LANES - SSD_HEADS))).astype(BF16), w_dt.T.astype(BF16),
        conv_w[l], row(conv_b[l]), row(dt_bias[l]), dt_bias[l].reshape(SSD_HEADS, 1))

    o4, l4 = _attention_regrouped(qkvg, 4)
    o16, l16 = _attention_regrouped(qkvg, 16)

    yn = _ssd(xbc, dt, dtT, zs, row(a_log[l]), a_log[l].reshape(SSD_HEADS, 1),
              row(jnp.repeat(d_skip[l], SSD_HEAD_DIM)), row(ssd_norm_w[l]),
              jnp.asarray(_weave_matrix(SSD_CHUNK).T, BF16))

    head_of_col = np.arange(ATT_WIDTH) // ATT_HEAD_DIM
    expand = jnp.asarray(np.arange(LANES)[:, None] == head_of_col[None, :], BF16)
    out = _out(x1, mod, o1, l1, o4, l4, o16, l16, yn, expand, jnp.asarray(perm.T, BF16),
               row(norm2_w[l]), w[:, o_g:].astype(BF16), w_att_out[l].astype(BF16), w_ssd_out[l].astype(BF16), w_mix_out[l].astype(BF16),
               row(norm3_w[l]), ffn2_w_gu[l].astype(BF16), ffn2_w_down[l].astype(BF16),
               row(final_norm_w))
    return out.reshape(BATCH, SEQ, D_MODEL)
```

```python
import jax
import jax.numpy as jnp
import numpy as np
from jax import lax
from jax.experimental import pallas as pl
from jax.experimental.pallas import tpu as pltpu

F32 = jnp.float32
BF16 = jnp.bfloat16

D_MODEL = 1024
BATCH = 2
SEQ = 8192
TOKENS = BATCH * SEQ
ATT_HEADS = 12
ATT_HEAD_DIM = 64
ATT_WIDTH = ATT_HEADS * ATT_HEAD_DIM
ROPE_DIM = ATT_HEAD_DIM // 4
ROPE_THETA = 500000.0
ATT_BLOCK = 128
SSD_INNER = 2 * D_MODEL
SSD_HEAD_DIM = 64
SSD_HEADS = SSD_INNER // SSD_HEAD_DIM
SSD_GROUPS = 8
SSD_STATE = 128
SSD_CONV = 4
SSD_CHUNK = 128
SSD_CONV_CH = SSD_INNER + 2 * SSD_GROUPS * SSD_STATE
D_FF = 2816
N_MOD = 9
NORM_EPS = 1e-6
IN_Z = 3 * ATT_WIDTH
IN_XBC = IN_Z + SSD_INNER
IN_DT = IN_XBC + SSD_CONV_CH
IN_GATES = IN_DT + SSD_HEADS

LANES = 128
SUBLANES = 8
VMEM_LIMIT = 56 * 1024 * 1024
TOKEN_TILE = 256
FFN1_TILE = 512
COL_CHUNK = 256
PROJ_CHUNK = 512
REGROUP = 16
ATT_STEP_BLOCKS = 4
ADA_COL_TILE = 1536
NEG_BIG = -1e30

_NT = (((1,), (1,)), ((), ()))


def _params(*sem):
    return pltpu.CompilerParams(dimension_semantics=sem, vmem_limit_bytes=VMEM_LIMIT)


def _resident(shape):
    zeros = (0,) * len(shape)
    return pl.BlockSpec(shape, lambda *_: zeros, pipeline_mode=pl.Buffered(1))


def _sigmoid(x):
    return 0.5 * jnp.tanh(0.5 * x) + 0.5


def _silu(x):
    return x * _sigmoid(x)


def _rms(x, w):
    ms = jnp.sum(x * x, axis=-1, keepdims=True) * (1.0 / x.shape[-1])
    return x * lax.rsqrt(ms + NORM_EPS) * w


def _rms_mod(x, w, shift, scale):
    return _rms(x, w) * (1.0 + scale) + shift


def _swiglu(h, wgu_ref, wd_ref):
    gu = jnp.dot(h, wgu_ref[...], preferred_element_type=F32)
    a = (_silu(gu[:, :D_FF]) * gu[:, D_FF:]).astype(BF16)
    return jnp.dot(a, wd_ref[...], preferred_element_type=F32)


def _ada_kernel(ct_ref, w_ref, b_ref, o_ref):
    act = _silu(ct_ref[...])
    w = w_ref[...]
    rows = [jnp.sum(w * act[:, b:b + 1], axis=0, keepdims=True) for b in range(BATCH)]
    o_ref[...] = jnp.concatenate(rows, axis=0) + b_ref[...]


def _ada(c, w_ada, b_ada, layer):
    n = N_MOD * D_MODEL
    return pl.pallas_call(
        _ada_kernel,
        grid=(n // ADA_COL_TILE,),
        in_specs=[
            pl.BlockSpec((D_MODEL, BATCH), lambda j: (0, 0)),
            pl.BlockSpec((None, D_MODEL, ADA_COL_TILE), lambda j: (layer, 0, j)),
            pl.BlockSpec((None, 1, ADA_COL_TILE), lambda j: (layer, 0, j)),
        ],
        out_specs=pl.BlockSpec((BATCH, ADA_COL_TILE), lambda j: (0, j)),
        out_shape=jax.ShapeDtypeStruct((BATCH, n), F32),
        compiler_params=_params("arbitrary"),
        name="ada_mod",
    )(c.T, w_ada, b_ada.reshape(-1, 1, n))


def _ffn1_kernel(x_ref, mod_ref, nw_ref, wgu_ref, wd_ref, o_ref):
    x = x_ref[...]
    mod = mod_ref[0]
    h = _rms_mod(x, nw_ref[...], mod[0:1], mod[1:2]).astype(BF16)
    o_ref[...] = x + (0.5 * mod[2:3]) * _swiglu(h, wgu_ref, wd_ref)


def _tile_spec(width, tm=TOKEN_TILE):
    return pl.BlockSpec((tm, width), lambda i: (i, 0))


def _mod_spec(tm=TOKEN_TILE):
    per_batch = SEQ // tm
    return pl.BlockSpec((1, N_MOD, D_MODEL), lambda i: (i // per_batch, 0, 0))


def _ffn1(x, mod, nw, wgu, wd):
    return pl.pallas_call(
        _ffn1_kernel,
        grid=(TOKENS // FFN1_TILE,),
        in_specs=[_tile_spec(D_MODEL, FFN1_TILE), _mod_spec(FFN1_TILE), _resident((1, D_MODEL)),
                  _resident(wgu.shape), _resident(wd.shape)],
        out_specs=_tile_spec(D_MODEL, FFN1_TILE),
        out_shape=jax.ShapeDtypeStruct((TOKENS, D_MODEL), F32),
        compiler_params=_params("arbitrary"),
        name="ffn1",
    )(x, mod, nw, wgu, wd)


def _softplus(x):
    return jnp.maximum(x, 0.0) + jnp.log1p(jnp.exp(-jnp.abs(x)))


def _inproj_kernel(*refs):
    *io_refs, raw_a_ref, raw_b_ref = refs
    x_ref, mod_ref, nw_ref = io_refs[:3]
    weave_ref = io_refs[7]
    h_ref, hw_ref, hist_ref, _, _, qkv_ref, kv_prev_ref = io_refs[-7:]
    step = pl.program_id(0)
    tm = x_ref.shape[0]
    tiles_per_seq = SEQ // tm
    n_tiles = pl.num_programs(0) - 1

    @pl.when(step == 0)
    def _():
        raw_b_ref[...] = jnp.zeros_like(raw_b_ref)
        kv_prev_ref[...] = jnp.zeros_like(kv_prev_ref)

    @pl.when((step == 0) | (step % tiles_per_seq == 1))
    def _():
        hist_ref[...] = jnp.zeros_like(hist_ref)

    mod = mod_ref[0]
    h_ref[...] = _rms_mod(x_ref[...], nw_ref[...], mod[3:4], mod[4:5]).astype(BF16)
    hw_ref[...] = jnp.dot(weave_ref[...], h_ref[...], preferred_element_type=F32).astype(BF16)

    @pl.when(step % 2 == 0)
    def _():
        _inproj_step(*io_refs, raw_a_ref, raw_b_ref)

    @pl.when(step % 2 == 1)
    def _():
        _inproj_step(*io_refs, raw_b_ref, raw_a_ref)

    @pl.when(step < n_tiles - 1)
    def _():
        for part in range(2):
            cols = slice((part + 1) * ATT_WIDTH, (part + 2) * ATT_WIDTH)
            kv_prev_ref[part] = qkv_ref[tm - ATT_BLOCK:, cols]


def _inproj_step(x_ref, mod_ref, nw_ref, pos_ref, invf_ref, sgn_ref, perm_ref, weave_ref,
                 band_ref, win_ref, wdtT_ref,
                 convw_ref, convb_ref, dtb_row_ref, dtb_col_ref,
                 o1_ref, l1_ref, qkvg_ref, zs_ref, xbc_ref, dt_ref, dtT_ref,
                 h_ref, hw_ref, hist_ref, cos_ref, sin_ref, qkv_ref, kv_prev_ref,
                 raw_out_ref, raw_in_ref):
    tm = x_ref.shape[0]
    tile = jnp.minimum(pl.program_id(0), pl.num_programs(0) - 2)
    has_prev = tile % (SEQ // tm) != 0

    def xbc_part(cols):
        raw_out_ref[:, cols] = jnp.dot(hw_ref[...], win_ref[:, pl.ds(IN_XBC + cols.start, cols.size)],
                                       preferred_element_type=F32)

    def rope_tables():
        ang = pos_ref[...] * invf_ref[...]
        cos_ref[...] = jnp.cos(ang)
        sin_ref[...] = jnp.sin(ang) * sgn_ref[...]

    def rope(t):
        reps = ATT_WIDTH // LANES
        cos = jnp.concatenate([cos_ref[...]] * reps, axis=1)
        sin = jnp.concatenate([sin_ref[...]] * reps, axis=1)
        lane = lax.broadcasted_iota(jnp.int32, (tm, ATT_WIDTH), 1)
        first_half = (lane & (ATT_HEAD_DIM - 1)) < (ROPE_DIM // 2)
        partner = jnp.where(first_half,
                            pltpu.roll(t, ATT_WIDTH - ROPE_DIM // 2, 1),
                            pltpu.roll(t, ROPE_DIM // 2, 1))
        return t * cos + partner * sin

    def qkv_part(idx):
        cols = slice(idx * ATT_WIDTH, (idx + 1) * ATT_WIDTH)
        t = jnp.dot(h_ref[...], win_ref[:, cols], preferred_element_type=F32)
        if idx == 0:
            t = rope(t) * (ATT_HEAD_DIM ** -0.5)
        elif idx == 1:
            t = rope(t)
        t = t.astype(BF16)
        qkv_ref[:, cols] = t
        regrouped = jnp.dot(perm_ref[...], t, preferred_element_type=F32).astype(BF16)
        qkvg_ref[:, :, cols] = regrouped.reshape(qkvg_ref.shape[:2] + (ATT_WIDTH,))

    def z_part(cols):
        z = jnp.dot(hw_ref[...], win_ref[:, pl.ds(IN_Z + cols.start, cols.size)],
                    preferred_element_type=F32)
        zs_ref[:, cols] = _silu(z).astype(BF16)

    def dt_part():
        hw = hw_ref[...]
        dt_raw = jnp.dot(hw, win_ref[:, IN_DT:IN_DT + LANES],
                         preferred_element_type=F32)[:, :SSD_HEADS]
        dt_ref[...] = _softplus(dt_raw + dtb_row_ref[...])
        dtT_raw = lax.dot_general(wdtT_ref[...], hw, _NT, preferred_element_type=F32)
        dtT_ref[...] = _softplus(dtT_raw + dtb_col_ref[...])

    tail_rows = (SSD_CONV - 1) * SUBLANES
    tail_start = SSD_CHUNK - tail_rows
    first_sublane = lax.broadcasted_iota(jnp.int32, (SUBLANES, COL_CHUNK), 0) == 0

    def conv_chunk(src):
        raw = raw_in_ref[:, src]
        taps = [convw_ref[k:k + 1, src] for k in range(SSD_CONV)]
        prev_tail = hist_ref[:, src]
        pieces = []
        for ck in range(tm // SSD_CHUNK):
            cur = raw[ck * SSD_CHUNK:(ck + 1) * SSD_CHUNK]
            cur_tail = cur[tail_start:]
            wrapped = []
            for k in range(SSD_CONV - 1):
                rows = slice(k * SUBLANES, (k + 1) * SUBLANES)
                wrapped.append(jnp.where(first_sublane, pltpu.roll(prev_tail[rows], 1, 0),
                                         pltpu.roll(cur_tail[rows], 1, 0)))
            ext = jnp.concatenate(wrapped + [cur], axis=0)
            acc = cur * taps[SSD_CONV - 1] + convb_ref[:, src]
            for s in range(1, SSD_CONV):
                lo = (SSD_CONV - 1 - s) * SUBLANES
                acc = acc + ext[lo:lo + SSD_CHUNK] * taps[SSD_CONV - 1 - s]
            pieces.append(acc)
            prev_tail = cur_tail
        hist_ref[:, src] = prev_tail
        xbc_ref[:, src] = _silu(jnp.concatenate(pieces, axis=0)).astype(BF16)

    blk = ATT_BLOCK
    k_at, v_at = ATT_WIDTH, 2 * ATT_WIDTH

    def with_prev(j, part, at, cols):
        cur_cols = slice(at + cols.start, at + cols.stop)
        if j == 0:
            return jnp.concatenate([kv_prev_ref[part, :, cols], qkv_ref[:blk, cur_cols]], axis=0)
        return qkv_ref[(j - 1) * blk:(j + 1) * blk, cur_cols]

    def emit(j, o, lse):
        o1_ref[j * blk:(j + 1) * blk, :] = o
        l1_ref[j * blk:(j + 1) * blk, :] = lse

    band = band_ref[...]
    attention = _attention_units(lambda j, cols: qkv_ref[j * blk:(j + 1) * blk, cols],
                                 lambda j, cols: with_prev(j, 0, k_at, cols),
                                 lambda j, cols: with_prev(j, 1, v_at, cols),
                                 band, _first_block_bias(band, has_prev), tm // blk, emit)

    chunks = lambda width, w: [pl.ds(c * w, w) for c in range(width // w)]
    conv = chunks(SSD_CONV_CH, COL_CHUNK)
    for fn, args in ((qkv_part, (2,)), (rope_tables, ()), (qkv_part, (0,)), (qkv_part, (1,))):
        fn(*args)
        conv_chunk(conv.pop(0))
    parts = ([(xbc_part, (c,)) for c in chunks(SSD_CONV_CH, PROJ_CHUNK)]
             + [(z_part, (c,)) for c in chunks(SSD_INNER, PROJ_CHUNK)] + [(dt_part, ())])
    per_part = -(-len(attention) // len(parts))
    for fn, args in parts:
        fn(*args)
        if conv:
            conv_chunk(conv.pop(0))
        for unit in attention[:per_part]:
            unit()
        attention = attention[per_part:]
    assert not conv and not attention


def _inproj(x1, mod, nw, posf, invf, sgn, perm, weave, band, w_in, wdtT,
            conv_w, conv_b, dtb_row, dtb_col):
    tm = TOKEN_TILE
    tiles = SEQ // tm
    n_tiles = TOKENS // tm
    cur = lambda i: jnp.minimum(i, n_tiles - 1)
    done = lambda i: jnp.maximum(i - 1, 0)
    tok = lambda w, d, at=cur: (jax.ShapeDtypeStruct((TOKENS, w), d),
                                pl.BlockSpec((tm, w), lambda i: (at(i), 0)))
    grouped = (jax.ShapeDtypeStruct((BATCH, tiles, REGROUP, tm // REGROUP, 3 * ATT_WIDTH), BF16),
               pl.BlockSpec((None, None, REGROUP, tm // REGROUP, 3 * ATT_WIDTH),
                            lambda i: (cur(i) // tiles, cur(i) % tiles, 0, 0, 0)))
    outs = [tok(ATT_WIDTH, BF16), tok(LANES, F32),
        grouped,
        tok(SSD_INNER, BF16),
        tok(SSD_CONV_CH, BF16, done),
        tok(SSD_HEADS, F32),
        (jax.ShapeDtypeStruct((SSD_HEADS, TOKENS), F32),
         pl.BlockSpec((SSD_HEADS, tm), lambda i: (0, cur(i)))),
    ]
    return pl.pallas_call(
        _inproj_kernel,
        grid=(n_tiles + 1,),
        in_specs=[tok(D_MODEL, F32)[1],
                  pl.BlockSpec((1, N_MOD, D_MODEL), lambda i: (cur(i) // tiles, 0, 0)),
                  _resident((1, D_MODEL)),
                  tok(LANES, F32)[1], _resident((1, LANES)), _resident((1, LANES)),
                  _resident(perm.shape), _resident(weave.shape), _resident(band.shape),
                  _resident(w_in.shape), _resident(wdtT.shape),
                  _resident(conv_w.shape), _resident(conv_b.shape),
                  _resident(dtb_row.shape), _resident(dtb_col.shape)],
        out_specs=[s for _, s in outs],
        out_shape=[s for s, _ in outs],
        scratch_shapes=[pltpu.VMEM((tm, D_MODEL), BF16), pltpu.VMEM((tm, D_MODEL), BF16),
                        pltpu.VMEM(((SSD_CONV - 1) * SUBLANES, SSD_CONV_CH), F32),
                        pltpu.VMEM((tm, LANES), F32), pltpu.VMEM((tm, LANES), F32),
                        pltpu.VMEM((tm, 3 * ATT_WIDTH), BF16),
                        pltpu.VMEM((2, ATT_BLOCK, ATT_WIDTH), BF16),
                        pltpu.VMEM((tm, SSD_CONV_CH), F32), pltpu.VMEM((tm, SSD_CONV_CH), F32)],
        compiler_params=_params("arbitrary"),
        name="in_proj",
    )(x1, mod, nw, posf, invf, sgn, perm, weave, band, w_in, wdtT,
      conv_w, conv_b, dtb_row, dtb_col)


def _band_bias(row_pos):
    blk = ATT_BLOCK
    pos = row_pos(np.arange(blk))
    k_pos = np.concatenate([pos, pos + blk])
    dist = pos[:, None] + blk - k_pos[None, :]
    return jnp.asarray(np.where((dist >= 0) & (dist <= blk), 0.0, NEG_BIG), F32)


def _first_block_bias(band, has_prev):
    ki = lax.broadcasted_iota(jnp.int32, band.shape, 1)
    return jnp.where((ki >= ATT_BLOCK) | has_prev, band, NEG_BIG)


def _attention_units(get_q, get_k, get_v, band, first_bias, n_blocks, emit):
    blk = ATT_BLOCK
    lane_row = lax.broadcasted_iota(jnp.int32, (1, LANES), 1)
    lane = lax.broadcasted_iota(jnp.int32, (blk, LANES), 1)
    low = lane < ATT_HEAD_DIM
    head_mask = [(lane_row < ATT_HEAD_DIM).astype(BF16), (lane_row >= ATT_HEAD_DIM).astype(BF16)]
    units = [(j, pair, half) for j in range(n_blocks)
             for pair in range(ATT_HEADS // 2) for half in range(2)]
    carry = {"scores": None, "outs": [], "o_pairs": [], "lse": jnp.zeros((blk, LANES), F32)}

    def scores(unit):
        j, pair, half = unit
        cols = slice(pair * LANES, (pair + 1) * LANES)
        s = lax.dot_general(get_q(j, cols) * head_mask[half], get_k(j, cols), _NT,
                            preferred_element_type=F32)
        return s + (first_bias if j == 0 else band)

    def run(n):
        j, pair, half = units[n]
        s = scores(units[0]) if n == 0 else carry["scores"]
        if n + 1 < len(units):
            carry["scores"] = scores(units[n + 1])
        m = jnp.max(s, axis=1, keepdims=True)
        p = jnp.exp(s - m)
        den = jnp.sum(p, axis=1, keepdims=True)
        o = jnp.dot(p.astype(BF16), get_v(j, slice(pair * LANES, (pair + 1) * LANES)),
                    preferred_element_type=F32)
        carry["outs"].append(o * (1.0 / den))
        carry["lse"] = jnp.where(lane == 2 * pair + half, m + jnp.log(den), carry["lse"])
        if half == 1:
            outs = carry["outs"]
            carry["o_pairs"].append(jnp.where(low, outs[0], outs[1]).astype(BF16))
            carry["outs"] = []
            if pair == ATT_HEADS // 2 - 1:
                emit(j, jnp.concatenate(carry["o_pairs"], axis=1), carry["lse"])
                carry["o_pairs"], carry["lse"] = [], jnp.zeros((blk, LANES), F32)

    return [lambda n=n: run(n) for n in range(len(units))]


def _attn_kernel(band_ref, q_ref, kp_ref, kc_ref, vp_ref, vc_ref, o_ref, lse_ref):
    blk = ATT_BLOCK
    flat = lambda ref: ref[...].reshape(-1, ref.shape[-1])
    q = flat(q_ref)
    k = jnp.concatenate([flat(kp_ref), flat(kc_ref)], axis=0)
    v = jnp.concatenate([flat(vp_ref), flat(vc_ref)], axis=0)
    band = band_ref[...]
    o_blocks, lse_blocks = [], []

    def emit(j, o, lse):
        o_blocks.append(o)
        lse_blocks.append(lse)

    for unit in _attention_units(lambda j, cols: q[j * blk:(j + 1) * blk, cols],
                                 lambda j, cols: k[j * blk:(j + 2) * blk, cols],
                                 lambda j, cols: v[j * blk:(j + 2) * blk, cols],
                                 band, _first_block_bias(band, pl.program_id(2) > 0),
                                 ATT_STEP_BLOCKS, emit):
        unit()
    o_ref[...] = jnp.concatenate(o_blocks, axis=0).reshape(o_ref.shape)
    lse_ref[...] = jnp.concatenate(lse_blocks, axis=0).reshape(lse_ref.shape)


def _attention_regrouped(qkvg, r):
    tiles, rows = qkvg.shape[1], qkvg.shape[3]
    fold = REGROUP // r
    tiles_per_blk = ATT_BLOCK // (fold * rows)
    n_blk = tiles // (ATT_STEP_BLOCKS * tiles_per_blk)
    shape6 = lambda w: (BATCH, tiles, fold, r, rows, w)
    qkvg = qkvg.reshape(shape6(3 * ATT_WIDTH))
    blk6 = lambda n_tiles, w: (None, n_tiles, fold, None, rows, w)
    cur = lambda w, part=0: pl.BlockSpec(blk6(ATT_STEP_BLOCKS * tiles_per_blk, w),
                                         lambda b, j, i: (b, i, 0, j, 0, part))
    prev = lambda part: pl.BlockSpec(
        blk6(tiles_per_blk, ATT_WIDTH),
        lambda b, j, i: (b, jnp.maximum(ATT_STEP_BLOCKS * i - 1, 0), 0, j, 0, part))
    per_tile = fold * rows

    def row_pos(rho):
        t = rho // per_tile
        a = (rho // rows) % fold
        return t * per_tile + (rho % rows) * fold + a

    o, lse = pl.pallas_call(
        _attn_kernel,
        grid=(BATCH, r, n_blk),
        in_specs=[_resident((ATT_BLOCK, 2 * ATT_BLOCK)), cur(ATT_WIDTH, 0),
                  prev(1), cur(ATT_WIDTH, 1), prev(2), cur(ATT_WIDTH, 2)],
        out_specs=[cur(ATT_WIDTH), cur(LANES)],
        out_shape=[jax.ShapeDtypeStruct(shape6(ATT_WIDTH), BF16),
                   jax.ShapeDtypeStruct(shape6(LANES), F32)],
        compiler_params=_params("arbitrary", "arbitrary", "arbitrary"),
        name=f"attn_dil{r}",
    )(_band_bias(row_pos), qkvg, qkvg, qkvg, qkvg, qkvg)
    grouped5 = (BATCH, tiles, REGROUP, rows)
    return o.reshape(grouped5 + (ATT_WIDTH,)), lse.reshape(grouped5 + (LANES,))


def _woven_time(i):
    return ((i & (SUBLANES - 1)) << 4) | (i >> 3)


def _ssd_kernel(*refs):
    *io_refs, y_a_ref, sq_a_ref, y_b_ref, sq_b_ref = refs
    state_ref = io_refs[-1]
    first = (pl.program_id(0) == 0) & (pl.program_id(1) == 0)

    @pl.when(first)
    def _():
        y_b_ref[...] = jnp.zeros_like(y_b_ref)
        sq_b_ref[...] = jnp.zeros_like(sq_b_ref)

    @pl.when(pl.program_id(1) == 0)
    def _():
        state_ref[...] = jnp.zeros_like(state_ref)

    @pl.when(pl.program_id(1) % 2 == 0)
    def _():
        _ssd_step(*io_refs, y_a_ref, sq_a_ref, y_b_ref, sq_b_ref)

    @pl.when(pl.program_id(1) % 2 == 1)
    def _():
        _ssd_step(*io_refs, y_b_ref, sq_b_ref, y_a_ref, sq_a_ref)


def _ssd_step(xbc_ref, dt_ref, dtT_ref, zs_ref,
              alog_row_ref, alog_col_ref, dskip_ref, nw_ref, unweave_ref,
              o_ref, state_ref, y_ref, sq_ref, y_done_ref, sq_done_ref):
    q = SSD_CHUNK
    heads = SSD_HEADS // SSD_GROUPS
    gw = heads * SSD_HEAD_DIM

    dtT = dtT_ref[...]
    a_dt_col = dt_ref[...] * (-jnp.exp(alog_row_ref[...]))
    a_dt_row = dtT * (-jnp.exp(alog_col_ref[...]))
    t_row = _woven_time(lax.broadcasted_iota(jnp.int32, (q, q), 0))
    t_col = _woven_time(lax.broadcasted_iota(jnp.int32, (q, q), 1))
    causal = t_row >= t_col
    hi = lax.Precision.HIGHEST
    cs_col = jnp.dot(causal.astype(F32), a_dt_col, precision=hi, preferred_element_type=F32)
    cs_row = jnp.dot(a_dt_row, (t_row <= t_col).astype(F32), precision=hi,
                     preferred_element_type=F32)
    total = cs_row[:, q - 1:q]
    dt_decay_row = cs_row - jnp.log(dtT)
    to_end = dtT * jnp.exp(total - cs_row)
    chunk_decay = jnp.exp(total)
    r2 = lax.broadcasted_iota(jnp.int32, (gw, gw), 0)
    c2 = lax.broadcasted_iota(jnp.int32, (gw, gw), 1)
    eye = (r2 == c2).astype(BF16)
    no_rows = jnp.zeros((SSD_HEAD_DIM, q + SSD_STATE), BF16)
    sq_sum = jnp.zeros((q, LANES), F32)

    def load(g):
        b_at = SSD_INNER + g * SSD_STATE
        c_at = b_at + SSD_GROUPS * SSD_STATE
        b_g = xbc_ref[:, b_at:b_at + SSD_STATE]
        c_g = xbc_ref[:, c_at:c_at + SSD_STATE]
        cb = lax.dot_general(c_g, b_g, _NT, preferred_element_type=F32)
        x_g = xbc_ref[:, g * gw:(g + 1) * gw]
        x_t = lax.dot_general(eye, x_g, _NT, preferred_element_type=F32)
        return b_g, c_g, cb, x_g, x_t

    def weights(g, loaded):
        _, c_g, cb, _, _ = loaded
        c_f = c_g.astype(F32)
        w_parts = []
        for j in range(heads):
            h = g * heads + j
            cs_l = jnp.broadcast_to(cs_col[:, h:h + 1], (q, q))
            w_parts.append(jnp.where(causal, cb * jnp.exp(cs_l - dt_decay_row[h:h + 1]), 0.0)
                           .astype(BF16))
            w_parts.append((c_f * jnp.exp(cs_l)).astype(BF16))
        return jnp.concatenate(w_parts, axis=1)

    def finish(g, loaded, w_all, sq_sum):
        b_g, _, _, x_g, x_t = loaded
        state = state_ref[g]
        rhs = jnp.concatenate([x_t.astype(BF16), state.astype(BF16)], axis=1)
        rhs_rows, end_rows, decay_rows = [], [], []
        for j in range(heads):
            h = g * heads + j
            mine = rhs[j * SSD_HEAD_DIM:(j + 1) * SSD_HEAD_DIM]
            rhs_rows.append(jnp.concatenate([mine if jj == j else no_rows for jj in range(heads)],
                                            axis=1))
            end_rows.append(jnp.broadcast_to(to_end[h:h + 1], (SSD_HEAD_DIM, q)))
            decay_rows.append(jnp.broadcast_to(chunk_decay[h:h + 1], (SSD_HEAD_DIM, SSD_STATE)))
        y = lax.dot_general(w_all, jnp.concatenate(rhs_rows, axis=0), _NT,
                            preferred_element_type=F32)
        xcols = slice(g * gw, (g + 1) * gw)
        y = (y + dskip_ref[:, xcols] * x_g.astype(F32)) * zs_ref[:, xcols].astype(F32)
        y_ref[:, xcols] = y
        sq = y * y
        upd = jnp.dot((x_t * jnp.concatenate(end_rows, axis=0)).astype(BF16), b_g,
                      preferred_element_type=F32)
        state_ref[g] = state * jnp.concatenate(decay_rows, axis=0) + upd
        return sq_sum + sq[:, :LANES] + sq[:, LANES:]

    ms = jnp.sum(sq_done_ref[...], axis=-1, keepdims=True) * (1.0 / SSD_INNER)
    inv_rms = lax.rsqrt(ms + NORM_EPS)

    def emit(g):
        cols = slice(g * gw, (g + 1) * gw)
        y = (y_done_ref[:, cols] * inv_rms * nw_ref[:, cols]).astype(BF16)
        o_ref[:, cols] = jnp.dot(unweave_ref[...], y, preferred_element_type=F32).astype(BF16)

    loaded = {0: load(0)}
    w_all = {0: weights(0, loaded[0])}
    for g in range(SSD_GROUPS):
        if g + 1 < SSD_GROUPS:
            loaded[g + 1] = load(g + 1)
            w_all[g + 1] = weights(g + 1, loaded[g + 1])
        emit(g)
        sq_sum = finish(g, loaded.pop(g), w_all.pop(g), sq_sum)
    sq_ref[...] = sq_sum


def _ssd(xbc, dt, dtT, zs, alog_row, alog_col, dskip, nw, unweave):
    n_chunks = SEQ // SSD_CHUNK
    cur = lambda b, c: b * n_chunks + jnp.minimum(c, n_chunks - 1)
    done = lambda b, c: b * n_chunks + jnp.maximum(c - 1, 0)
    tok = lambda w, at=cur: pl.BlockSpec((SSD_CHUNK, w), lambda b, c: (at(b, c), 0))
    gw = SSD_INNER // SSD_GROUPS
    return pl.pallas_call(
        _ssd_kernel,
        grid=(BATCH, n_chunks + 1),
        in_specs=[tok(SSD_CONV_CH), tok(SSD_HEADS),
                  pl.BlockSpec((SSD_HEADS, SSD_CHUNK), lambda b, c: (0, cur(b, c))),
                  tok(SSD_INNER),
                  _resident((1, SSD_HEADS)), _resident((SSD_HEADS, 1)),
                  _resident((1, SSD_INNER)), _resident((1, SSD_INNER)),
                  _resident(unweave.shape)],
        out_specs=tok(SSD_INNER, done),
        out_shape=jax.ShapeDtypeStruct((TOKENS, SSD_INNER), BF16),
        scratch_shapes=[pltpu.VMEM((SSD_GROUPS, gw, SSD_STATE), F32)]
                       + [pltpu.VMEM((SSD_CHUNK, SSD_INNER), F32),
                          pltpu.VMEM((SSD_CHUNK, LANES), F32)] * 2,
        compiler_params=_params("arbitrary", "arbitrary"),
        name="ssd_scan",
    )(xbc, dt, dtT, zs, alog_row, alog_col, dskip, nw, unweave)


def _out_kernel(x_ref, mod_ref, o1_ref, l1_ref, o4_ref, l4_ref, o16_ref, l16_ref,
                yn_ref, expand_ref, restore_ref, n2w_ref, wg_ref, watt_ref, wssd_ref, wmix_ref,
                n3w_ref, wgu_ref, wd_ref, fnw_ref, out_ref):
    tm = x_ref.shape[0]
    mod = mod_ref[0]
    expand = lambda a: jnp.dot(a.astype(BF16), expand_ref[...], preferred_element_type=F32)
    rows = lambda ref: ref[...].reshape(tm, ref.shape[-1])

    def mix(lse_a, o_a, lse_b, o_b):
        m = jnp.maximum(lse_a, lse_b)
        e_a, e_b = jnp.exp(lse_a - m), jnp.exp(lse_b - m)
        tot = e_a + e_b
        inv = 1.0 / tot
        return m + jnp.log(tot), expand(e_a * inv) * o_a + expand(e_b * inv) * o_b

    lse_g, o_g = mix(rows(l4_ref), rows(o4_ref).astype(F32), rows(l16_ref), rows(o16_ref).astype(F32))
    restore = restore_ref[...]
    o_g = jnp.dot(restore, o_g.astype(BF16), preferred_element_type=F32)
    lse_rest, lse_g_tok = lse_g, None
    for _ in range(3):
        piece = lse_rest.astype(BF16)
        moved = jnp.dot(restore, piece, preferred_element_type=F32)
        lse_g_tok = moved if lse_g_tok is None else lse_g_tok + moved
        lse_rest = lse_rest - piece.astype(F32)
    _, o_att = mix(l1_ref[...], o1_ref[...].astype(F32), lse_g_tok, o_g)

    y_att = jnp.dot(o_att.astype(BF16), watt_ref[...], preferred_element_type=F32)
    y_ssd = jnp.dot(yn_ref[...], wssd_ref[...], preferred_element_type=F32)
    x1 = x_ref[...]
    h_mix = _rms_mod(x1, n2w_ref[...], mod[3:4], mod[4:5]).astype(BF16)
    gates = _sigmoid(jnp.dot(h_mix, wg_ref[...], preferred_element_type=F32))
    merged = gates[:, :D_MODEL] * y_att + gates[:, D_MODEL:] * y_ssd
    y = jnp.dot(merged.astype(BF16), wmix_ref[...], preferred_element_type=F32)
    x2 = x1 + mod[5:6] * y
    h = _rms_mod(x2, n3w_ref[...], mod[6:7], mod[7:8]).astype(BF16)
    x3 = x2 + (0.5 * mod[8:9]) * _swiglu(h, wgu_ref, wd_ref)
    out_ref[...] = _rms(x3, fnw_ref[...])


def _out(x1, mod, o1, l1, o4, l4, o16, l16, yn, expand, restore, n2w, wg,
         watt, wssd, wmix, n3w, wgu, wd, fnw):
    tm = TOKEN_TILE
    tiles = SEQ // tm
    grouped = lambda w: pl.BlockSpec((None, None, REGROUP, tm // REGROUP, w),
                                     lambda i: (i // tiles, i % tiles, 0, 0, 0))
    return pl.pallas_call(
        _out_kernel,
        grid=(TOKENS // tm,),
        in_specs=[_tile_spec(D_MODEL), _mod_spec(), _tile_spec(ATT_WIDTH), _tile_spec(LANES),
                  grouped(ATT_WIDTH), grouped(LANES), grouped(ATT_WIDTH), grouped(LANES),
                  _tile_spec(SSD_INNER)]
                 + [_resident(a.shape) for a in (expand, restore, n2w, wg, watt, wssd, wmix, n3w,
                                                 wgu, wd, fnw)],
        out_specs=_tile_spec(D_MODEL),
        out_shape=jax.ShapeDtypeStruct((TOKENS, D_MODEL), F32),
        compiler_params=_params("arbitrary"),
        name="mix_out_ffn2",
    )(x1, mod, o1, l1, o4, l4, o16, l16, yn, expand, restore, n2w, wg,
      watt, wssd, wmix, n3w, wgu, wd, fnw)


def _rope_tables():
    inv_freq = ROPE_THETA ** (-jnp.arange(0, ROPE_DIM, 2, dtype=F32) / ROPE_DIM)
    d = np.arange(LANES) % ATT_HEAD_DIM
    half = ROPE_DIM // 2
    invf = jnp.where(d < ROPE_DIM, inv_freq[d % half], 0.0).astype(F32).reshape(1, LANES)
    sgn = np.where(d < half, -1.0, np.where(d < ROPE_DIM, 1.0, 0.0)).astype(np.float32)
    return invf, jnp.asarray(sgn).reshape(1, LANES)


def _regroup_matrix(tm):
    rho = np.arange(tm)
    src = REGROUP * (rho % (tm // REGROUP)) + rho // (tm // REGROUP)
    return (np.arange(tm)[None, :] == src[:, None]).astype(np.float32)


def _weave_matrix(tm):
    rho = np.arange(tm)
    src = (rho // SSD_CHUNK) * SSD_CHUNK + _woven_time(rho % SSD_CHUNK)
    return (np.arange(tm)[None, :] == src[:, None]).astype(np.float32)


def kernel(x, c, positions, w_ada, b_ada, norm1_w, ffn1_w_gu, ffn1_w_down, norm2_w, w_in,
           conv_w, conv_b, dt_bias, a_log, d_skip, ssd_norm_w, w_att_out, w_ssd_out,
           w_mix_out, norm3_w, ffn2_w_gu, ffn2_w_down, final_norm_w):
    l = 0
    row = lambda t: t.reshape(1, -1).astype(F32)
    xf = x.reshape(TOKENS, D_MODEL)
    mod = _ada(c, w_ada, b_ada, l).reshape(BATCH, N_MOD, D_MODEL)

    x1 = _ffn1(xf, mod, row(norm1_w[l]), ffn1_w_gu[l].astype(BF16), ffn1_w_down[l].astype(BF16))

    w = w_in[l].astype(BF16)
    invf, sgn = _rope_tables()
    perm = _regroup_matrix(TOKEN_TILE)
    posf = jnp.broadcast_to(positions.reshape(TOKENS, 1).astype(F32), (TOKENS, LANES))
    o1, l1, qkvg, zs, xbc, dt, dtT = _inproj(
        x1, mod, row(norm2_w[l]), posf, invf, sgn, jnp.asarray(perm, BF16),
        jnp.asarray(_weave_matrix(TOKEN_TILE), BF16), _band_bias(lambda r: r),
        w, w[:, IN_DT:IN_GATES].T,
        conv_w[l], row(conv_b[l]), row(dt_bias[l]), dt_bias[l].reshape(SSD_HEADS, 1))

    o4, l4 = _attention_regrouped(qkvg, 4)
    o16, l16 = _attention_regrouped(qkvg, 16)

    yn = _ssd(xbc, dt, dtT, zs, row(a_log[l]), a_log[l].reshape(SSD_HEADS, 1),
              row(jnp.repeat(d_skip[l], SSD_HEAD_DIM)), row(ssd_norm_w[l]),
              jnp.asarray(_weave_matrix(SSD_CHUNK).T, BF16))

    head_of_col = np.arange(ATT_WIDTH) // ATT_HEAD_DIM
    expand = jnp.asarray(np.arange(LANES)[:, None] == head_of_col[None, :], BF16)
    out = _out(x1, mod, o1, l1, o4, l4, o16, l16, yn, expand, jnp.asarray(perm.T, BF16),
               row(norm2_w[l]), w[:, IN_GATES:], w_att_out[l].astype(BF16),
               w_ssd_out[l].astype(BF16), w_mix_out[l].astype(BF16),
               row(norm3_w[l]), ffn2_w_gu[l].astype(BF16), ffn2_w_down[l].astype(BF16),
               row(final_norm_w))
    return out.reshape(BATCH, SEQ, D_MODEL)
```

```python
import jax
import jax.numpy as jnp
import numpy as np
from jax import lax
from jax.experimental import pallas as pl
from jax.experimental.pallas import tpu as pltpu

F32 = jnp.float32
BF16 = jnp.bfloat16

D_MODEL = 1024
BATCH = 2
SEQ = 8192
TOKENS = BATCH * SEQ
ATT_HEADS = 12
ATT_HEAD_DIM = 64
ATT_WIDTH = ATT_HEADS * ATT_HEAD_DIM
ROPE_DIM = ATT_HEAD_DIM // 4
ROPE_THETA = 500000.0
ATT_BLOCK = 128
SSD_INNER = 2 * D_MODEL
SSD_HEAD_DIM = 64
SSD_HEADS = SSD_INNER // SSD_HEAD_DIM
SSD_GROUPS = 8
SSD_STATE = 128
SSD_CONV = 4
SSD_CHUNK = 128
SSD_CONV_CH = SSD_INNER + 2 * SSD_GROUPS * SSD_STATE
D_FF = 2816
N_MOD = 9
NORM_EPS = 1e-6
IN_Z = 3 * ATT_WIDTH
IN_XBC = IN_Z + SSD_INNER
IN_DT = IN_XBC + SSD_CONV_CH
IN_GATES = IN_DT + SSD_HEADS

LANES = 128
SUBLANES = 8
VMEM_LIMIT = 56 * 1024 * 1024
TOKEN_TILE = 256
FFN1_TILE = 512
COL_CHUNK = 256
PROJ_CHUNK = 512
REGROUP = 16
ATT_STEP_BLOCKS = 8
ADA_COL_TILE = 1536
NEG_BIG = -1e30

_NT = (((1,), (1,)), ((), ()))


def _params(*sem):
    return pltpu.CompilerParams(dimension_semantics=sem, vmem_limit_bytes=VMEM_LIMIT)


def _resident(shape):
    zeros = (0,) * len(shape)
    return pl.BlockSpec(shape, lambda *_: zeros, pipeline_mode=pl.Buffered(1))


def _sigmoid(x):
    return 0.5 * jnp.tanh(0.5 * x) + 0.5


def _silu(x):
    return x * _sigmoid(x)


def _rms(x, w):
    ms = jnp.sum(x * x, axis=-1, keepdims=True) * (1.0 / x.shape[-1])
    return x * lax.rsqrt(ms + NORM_EPS) * w


def _rms_mod(x, w, shift, scale):
    return _rms(x, w) * (1.0 + scale) + shift


def _swiglu(h, wgu_ref, wd_ref):
    gu = jnp.dot(h, wgu_ref[...], preferred_element_type=F32)
    a = (_silu(gu[:, :D_FF]) * gu[:, D_FF:]).astype(BF16)
    return jnp.dot(a, wd_ref[...], preferred_element_type=F32)


def _ada_kernel(ct_ref, w_ref, b_ref, o_ref):
    act = _silu(ct_ref[...])
    w = w_ref[...]
    rows = [jnp.sum(w * act[:, b:b + 1], axis=0, keepdims=True) for b in range(BATCH)]
    o_ref[...] = jnp.concatenate(rows, axis=0) + b_ref[...]


def _ada(c, w_ada, b_ada, layer):
    n = N_MOD * D_MODEL
    return pl.pallas_call(
        _ada_kernel,
        grid=(n // ADA_COL_TILE,),
        in_specs=[
            pl.BlockSpec((D_MODEL, BATCH), lambda j: (0, 0)),
            pl.BlockSpec((None, D_MODEL, ADA_COL_TILE), lambda j: (layer, 0, j)),
            pl.BlockSpec((None, 1, ADA_COL_TILE), lambda j: (layer, 0, j)),
        ],
        out_specs=pl.BlockSpec((BATCH, ADA_COL_TILE), lambda j: (0, j)),
        out_shape=jax.ShapeDtypeStruct((BATCH, n), F32),
        compiler_params=_params("arbitrary"),
        name="ada_mod",
    )(c.T, w_ada, b_ada.reshape(-1, 1, n))


def _ffn1_kernel(x_ref, mod_ref, nw_ref, wgu_ref, wd_ref, o_ref):
    x = x_ref[...]
    mod = mod_ref[0]
    h = _rms_mod(x, nw_ref[...], mod[0:1], mod[1:2]).astype(BF16)
    o_ref[...] = x + (0.5 * mod[2:3]) * _swiglu(h, wgu_ref, wd_ref)


def _tile_spec(width, tm=TOKEN_TILE):
    return pl.BlockSpec((tm, width), lambda i: (i, 0))


def _mod_spec(tm=TOKEN_TILE):
    per_batch = SEQ // tm
    return pl.BlockSpec((1, N_MOD, D_MODEL), lambda i: (i // per_batch, 0, 0))


def _ffn1(x, mod, nw, wgu, wd):
    return pl.pallas_call(
        _ffn1_kernel,
        grid=(TOKENS // FFN1_TILE,),
        in_specs=[_tile_spec(D_MODEL, FFN1_TILE), _mod_spec(FFN1_TILE), _resident((1, D_MODEL)),
                  _resident(wgu.shape), _resident(wd.shape)],
        out_specs=_tile_spec(D_MODEL, FFN1_TILE),
        out_shape=jax.ShapeDtypeStruct((TOKENS, D_MODEL), F32),
        compiler_params=_params("arbitrary"),
        name="ffn1",
    )(x, mod, nw, wgu, wd)


def _softplus(x):
    return jnp.maximum(x, 0.0) + jnp.log1p(jnp.exp(-jnp.abs(x)))


def _inproj_kernel(*refs):
    *io_refs, raw_a_ref, raw_b_ref = refs
    x_ref, mod_ref, nw_ref = io_refs[:3]
    weave_ref = io_refs[7]
    h_ref, hw_ref, hist_ref, _, _, qkv_ref, kv_prev_ref = io_refs[-7:]
    step = pl.program_id(0)
    tm = x_ref.shape[0]
    tiles_per_seq = SEQ // tm
    n_tiles = pl.num_programs(0) - 1

    @pl.when(step == 0)
    def _():
        raw_b_ref[...] = jnp.zeros_like(raw_b_ref)
        kv_prev_ref[...] = jnp.zeros_like(kv_prev_ref)

    @pl.when((step == 0) | (step % tiles_per_seq == 1))
    def _():
        hist_ref[...] = jnp.zeros_like(hist_ref)

    mod = mod_ref[0]
    h_ref[...] = _rms_mod(x_ref[...], nw_ref[...], mod[3:4], mod[4:5]).astype(BF16)
    hw_ref[...] = jnp.dot(weave_ref[...], h_ref[...], preferred_element_type=F32).astype(BF16)

    @pl.when(step % 2 == 0)
    def _():
        _inproj_step(*io_refs, raw_a_ref, raw_b_ref)

    @pl.when(step % 2 == 1)
    def _():
        _inproj_step(*io_refs, raw_b_ref, raw_a_ref)

    @pl.when(step < n_tiles - 1)
    def _():
        for part in range(2):
            cols = slice((part + 1) * ATT_WIDTH, (part + 2) * ATT_WIDTH)
            kv_prev_ref[part] = qkv_ref[tm - ATT_BLOCK:, cols]


def _inproj_step(x_ref, mod_ref, nw_ref, pos_ref, invf_ref, sgn_ref, perm_ref, weave_ref,
                 band_ref, win_ref, wdtT_ref,
                 convw_ref, convb_ref, dtb_row_ref, dtb_col_ref,
                 o1_ref, l1_ref, qkvg_ref, zs_ref, xbc_ref, dt_ref, dtT_ref,
                 h_ref, hw_ref, hist_ref, cos_ref, sin_ref, qkv_ref, kv_prev_ref,
                 raw_out_ref, raw_in_ref):
    tm = x_ref.shape[0]
    tile = jnp.minimum(pl.program_id(0), pl.num_programs(0) - 2)
    has_prev = tile % (SEQ // tm) != 0

    def xbc_part(cols):
        raw_out_ref[:, cols] = jnp.dot(hw_ref[...], win_ref[:, pl.ds(IN_XBC + cols.start, cols.size)],
                                       preferred_element_type=F32)

    def rope_tables():
        ang = pos_ref[...] * invf_ref[...]
        cos_ref[...] = jnp.cos(ang)
        sin_ref[...] = jnp.sin(ang) * sgn_ref[...]

    def rope(t):
        reps = ATT_WIDTH // LANES
        cos = jnp.concatenate([cos_ref[...]] * reps, axis=1)
        sin = jnp.concatenate([sin_ref[...]] * reps, axis=1)
        lane = lax.broadcasted_iota(jnp.int32, (tm, ATT_WIDTH), 1)
        first_half = (lane & (ATT_HEAD_DIM - 1)) < (ROPE_DIM // 2)
        partner = jnp.where(first_half,
                            pltpu.roll(t, ATT_WIDTH - ROPE_DIM // 2, 1),
                            pltpu.roll(t, ROPE_DIM // 2, 1))
        return t * cos + partner * sin

    def qkv_part(idx):
        cols = slice(idx * ATT_WIDTH, (idx + 1) * ATT_WIDTH)
        t = jnp.dot(h_ref[...], win_ref[:, cols], preferred_element_type=F32)
        if idx == 0:
            t = rope(t) * (ATT_HEAD_DIM ** -0.5)
        elif idx == 1:
            t = rope(t)
        t = t.astype(BF16)
        qkv_ref[:, cols] = t
        regrouped = jnp.dot(perm_ref[...], t, preferred_element_type=F32).astype(BF16)
        qkvg_ref[:, :, cols] = regrouped.reshape(qkvg_ref.shape[:2] + (ATT_WIDTH,))

    def z_part(cols):
        z = jnp.dot(hw_ref[...], win_ref[:, pl.ds(IN_Z + cols.start, cols.size)],
                    preferred_element_type=F32)
        zs_ref[:, cols] = _silu(z).astype(BF16)

    def dt_part():
        hw = hw_ref[...]
        dt_raw = jnp.dot(hw, win_ref[:, IN_DT:IN_DT + LANES],
                         preferred_element_type=F32)[:, :SSD_HEADS]
        dt_ref[...] = _softplus(dt_raw + dtb_row_ref[...])
        dtT_raw = lax.dot_general(wdtT_ref[...], hw, _NT, preferred_element_type=F32)
        dtT_ref[...] = _softplus(dtT_raw + dtb_col_ref[...])

    tail_rows = (SSD_CONV - 1) * SUBLANES
    tail_start = SSD_CHUNK - tail_rows
    first_sublane = lax.broadcasted_iota(jnp.int32, (SUBLANES, COL_CHUNK), 0) == 0

    def conv_chunk(src):
        raw = raw_in_ref[:, src]
        taps = [convw_ref[k:k + 1, src] for k in range(SSD_CONV)]
        prev_tail = hist_ref[:, src]
        pieces = []
        for ck in range(tm // SSD_CHUNK):
            cur = raw[ck * SSD_CHUNK:(ck + 1) * SSD_CHUNK]
            cur_tail = cur[tail_start:]
            wrapped = []
            for k in range(SSD_CONV - 1):
                rows = slice(k * SUBLANES, (k + 1) * SUBLANES)
                wrapped.append(jnp.where(first_sublane, pltpu.roll(prev_tail[rows], 1, 0),
                                         pltpu.roll(cur_tail[rows], 1, 0)))
            ext = jnp.concatenate(wrapped + [cur], axis=0)
            acc = cur * taps[SSD_CONV - 1] + convb_ref[:, src]
            for s in range(1, SSD_CONV):
                lo = (SSD_CONV - 1 - s) * SUBLANES
                acc = acc + ext[lo:lo + SSD_CHUNK] * taps[SSD_CONV - 1 - s]
            pieces.append(acc)
            prev_tail = cur_tail
        hist_ref[:, src] = prev_tail
        xbc_ref[:, src] = _silu(jnp.concatenate(pieces, axis=0)).astype(BF16)

    blk = ATT_BLOCK
    k_at, v_at = ATT_WIDTH, 2 * ATT_WIDTH

    def with_prev(j, part, at, cols):
        cur_cols = slice(at + cols.start, at + cols.stop)
        if j == 0:
            return jnp.concatenate([kv_prev_ref[part, :, cols], qkv_ref[:blk, cur_cols]], axis=0)
        return qkv_ref[(j - 1) * blk:(j + 1) * blk, cur_cols]

    def emit(j, o, lse):
        o1_ref[j * blk:(j + 1) * blk, :] = o
        l1_ref[j * blk:(j + 1) * blk, :] = lse

    band = band_ref[...]
    attention = _attention_units(lambda j, cols: qkv_ref[j * blk:(j + 1) * blk, cols],
                                 lambda j, cols: with_prev(j, 0, k_at, cols),
                                 lambda j, cols: with_prev(j, 1, v_at, cols),
                                 band, _first_block_bias(band, has_prev), tm // blk, emit)

    chunks = lambda width, w: [pl.ds(c * w, w) for c in range(width // w)]
    conv = chunks(SSD_CONV_CH, COL_CHUNK)
    for fn, args in ((qkv_part, (2,)), (rope_tables, ()), (qkv_part, (0,)), (qkv_part, (1,))):
        fn(*args)
        conv_chunk(conv.pop(0))
    parts = ([(xbc_part, (c,)) for c in chunks(SSD_CONV_CH, PROJ_CHUNK)]
             + [(z_part, (c,)) for c in chunks(SSD_INNER, PROJ_CHUNK)] + [(dt_part, ())])
    per_part = -(-len(attention) // len(parts))
    for fn, args in parts:
        fn(*args)
        if conv:
            conv_chunk(conv.pop(0))
        for unit in attention[:per_part]:
            unit()
        attention = attention[per_part:]
    assert not conv and not attention


def _inproj(x1, mod, nw, posf, invf, sgn, perm, weave, band, w_in, wdtT,
            conv_w, conv_b, dtb_row, dtb_col):
    tm = TOKEN_TILE
    tiles = SEQ // tm
    n_tiles = TOKENS // tm
    cur = lambda i: jnp.minimum(i, n_tiles - 1)
    done = lambda i: jnp.maximum(i - 1, 0)
    tok = lambda w, d, at=cur: (jax.ShapeDtypeStruct((TOKENS, w), d),
                                pl.BlockSpec((tm, w), lambda i: (at(i), 0)))
    grouped = (jax.ShapeDtypeStruct((BATCH, tiles, REGROUP, tm // REGROUP, 3 * ATT_WIDTH), BF16),
               pl.BlockSpec((None, None, REGROUP, tm // REGROUP, 3 * ATT_WIDTH),
                            lambda i: (cur(i) // tiles, cur(i) % tiles, 0, 0, 0)))
    outs = [tok(ATT_WIDTH, BF16), tok(LANES, F32),
        grouped,
        tok(SSD_INNER, BF16),
        tok(SSD_CONV_CH, BF16, done),
        tok(SSD_HEADS, F32),
        (jax.ShapeDtypeStruct((SSD_HEADS, TOKENS), F32),
         pl.BlockSpec((SSD_HEADS, tm), lambda i: (0, cur(i)))),
    ]
    return pl.pallas_call(
        _inproj_kernel,
        grid=(n_tiles + 1,),
        in_specs=[tok(D_MODEL, F32)[1],
                  pl.BlockSpec((1, N_MOD, D_MODEL), lambda i: (cur(i) // tiles, 0, 0)),
                  _resident((1, D_MODEL)),
                  tok(LANES, F32)[1], _resident((1, LANES)), _resident((1, LANES)),
                  _resident(perm.shape), _resident(weave.shape), _resident(band.shape),
                  _resident(w_in.shape), _resident(wdtT.shape),
                  _resident(conv_w.shape), _resident(conv_b.shape),
                  _resident(dtb_row.shape), _resident(dtb_col.shape)],
        out_specs=[s for _, s in outs],
        out_shape=[s for s, _ in outs],
        scratch_shapes=[pltpu.VMEM((tm, D_MODEL), BF16), pltpu.VMEM((tm, D_MODEL), BF16),
                        pltpu.VMEM(((SSD_CONV - 1) * SUBLANES, SSD_CONV_CH), F32),
                        pltpu.VMEM((tm, LANES), F32), pltpu.VMEM((tm, LANES), F32),
                        pltpu.VMEM((tm, 3 * ATT_WIDTH), BF16),
                        pltpu.VMEM((2, ATT_BLOCK, ATT_WIDTH), BF16),
                        pltpu.VMEM((tm, SSD_CONV_CH), F32), pltpu.VMEM((tm, SSD_CONV_CH), F32)],
        compiler_params=_params("arbitrary"),
        name="in_proj",
    )(x1, mod, nw, posf, invf, sgn, perm, weave, band, w_in, wdtT,
      conv_w, conv_b, dtb_row, dtb_col)


def _band_bias(row_pos):
    blk = ATT_BLOCK
    pos = row_pos(np.arange(blk))
    k_pos = np.concatenate([pos, pos + blk])
    dist = pos[:, None] + blk - k_pos[None, :]
    return jnp.asarray(np.where((dist >= 0) & (dist <= blk), 0.0, NEG_BIG), F32)


def _first_block_bias(band, has_prev):
    ki = lax.broadcasted_iota(jnp.int32, band.shape, 1)
    return jnp.where((ki >= ATT_BLOCK) | has_prev, band, NEG_BIG)


def _attention_units(get_q, get_k, get_v, band, first_bias, n_blocks, emit):
    blk = ATT_BLOCK
    lane_row = lax.broadcasted_iota(jnp.int32, (1, LANES), 1)
    lane = lax.broadcasted_iota(jnp.int32, (blk, LANES), 1)
    low = lane < ATT_HEAD_DIM
    head_mask = [(lane_row < ATT_HEAD_DIM).astype(BF16), (lane_row >= ATT_HEAD_DIM).astype(BF16)]
    units = [(j, pair, half) for j in range(n_blocks)
             for pair in range(ATT_HEADS // 2) for half in range(2)]
    carry = {"scores": None, "outs": [], "o_pairs": [], "lse": jnp.zeros((blk, LANES), F32)}

    def scores(unit):
        j, pair, half = unit
        cols = slice(pair * LANES, (pair + 1) * LANES)
        s = lax.dot_general(get_q(j, cols) * head_mask[half], get_k(j, cols), _NT,
                            preferred_element_type=F32)
        return s + (first_bias if j == 0 else band)

    def run(n):
        j, pair, half = units[n]
        s = scores(units[0]) if n == 0 else carry["scores"]
        if n + 1 < len(units):
            carry["scores"] = scores(units[n + 1])
        m = jnp.max(s, axis=1, keepdims=True)
        p = jnp.exp(s - m)
        den = jnp.sum(p, axis=1, keepdims=True)
        o = jnp.dot(p.astype(BF16), get_v(j, slice(pair * LANES, (pair + 1) * LANES)),
                    preferred_element_type=F32)
        carry["outs"].append(o * (1.0 / den))
        carry["lse"] = jnp.where(lane == 2 * pair + half, m + jnp.log(den), carry["lse"])
        if half == 1:
            outs = carry["outs"]
            carry["o_pairs"].append(jnp.where(low, outs[0], outs[1]).astype(BF16))
            carry["outs"] = []
            if pair == ATT_HEADS // 2 - 1:
                emit(j, jnp.concatenate(carry["o_pairs"], axis=1), carry["lse"])
                carry["o_pairs"], carry["lse"] = [], jnp.zeros((blk, LANES), F32)

    return [lambda n=n: run(n) for n in range(len(units))]


def _attn_kernel(band_ref, q_ref, kp_ref, kc_ref, vp_ref, vc_ref, o_ref, lse_ref):
    blk = ATT_BLOCK
    flat = lambda ref: ref[...].reshape(-1, ref.shape[-1])
    q = flat(q_ref)
    k = jnp.concatenate([flat(kp_ref), flat(kc_ref)], axis=0)
    v = jnp.concatenate([flat(vp_ref), flat(vc_ref)], axis=0)
    band = band_ref[...]
    o_blocks, lse_blocks = [], []

    def emit(j, o, lse):
        o_blocks.append(o)
        lse_blocks.append(lse)

    for unit in _attention_units(lambda j, cols: q[j * blk:(j + 1) * blk, cols],
                                 lambda j, cols: k[j * blk:(j + 2) * blk, cols],
                                 lambda j, cols: v[j * blk:(j + 2) * blk, cols],
                                 band, _first_block_bias(band, pl.program_id(2) > 0),
                                 q.shape[0] // blk, emit):
        unit()
    o_ref[...] = jnp.concatenate(o_blocks, axis=0).reshape(o_ref.shape)
    lse_ref[...] = jnp.concatenate(lse_blocks, axis=0).reshape(lse_ref.shape)


def _attention_regrouped(qkvg, r):
    tiles, rows = qkvg.shape[1], qkvg.shape[3]
    fold = REGROUP // r
    tiles_per_blk = ATT_BLOCK // (fold * rows)
    step_blocks = min(ATT_STEP_BLOCKS, tiles // tiles_per_blk)
    n_blk = tiles // (step_blocks * tiles_per_blk)
    shape6 = lambda w: (BATCH, tiles, fold, r, rows, w)
    qkvg = qkvg.reshape(shape6(3 * ATT_WIDTH))
    blk6 = lambda n_tiles, w: (None, n_tiles, fold, None, rows, w)
    cur = lambda w, part=0: pl.BlockSpec(blk6(step_blocks * tiles_per_blk, w),
                                         lambda b, j, i: (b, i, 0, j, 0, part))
    prev = lambda part: pl.BlockSpec(
        blk6(tiles_per_blk, ATT_WIDTH),
        lambda b, j, i: (b, jnp.maximum(step_blocks * i - 1, 0), 0, j, 0, part))
    per_tile = fold * rows

    def row_pos(rho):
        t = rho // per_tile
        a = (rho // rows) % fold
        return t * per_tile + (rho % rows) * fold + a

    o, lse = pl.pallas_call(
        _attn_kernel,
        grid=(BATCH, r, n_blk),
        in_specs=[_resident((ATT_BLOCK, 2 * ATT_BLOCK)), cur(ATT_WIDTH, 0),
                  prev(1), cur(ATT_WIDTH, 1), prev(2), cur(ATT_WIDTH, 2)],
        out_specs=[cur(ATT_WIDTH), cur(LANES)],
        out_shape=[jax.ShapeDtypeStruct(shape6(ATT_WIDTH), BF16),
                   jax.ShapeDtypeStruct(shape6(LANES), F32)],
        compiler_params=_params("arbitrary", "arbitrary", "arbitrary"),
        name=f"attn_dil{r}",
    )(_band_bias(row_pos), qkvg, qkvg, qkvg, qkvg, qkvg)
    grouped5 = (BATCH, tiles, REGROUP, rows)
    return o.reshape(grouped5 + (ATT_WIDTH,)), lse.reshape(grouped5 + (LANES,))


def _woven_time(i):
    return ((i & (SUBLANES - 1)) << 4) | (i >> 3)


def _ssd_kernel(*refs):
    *io_refs, y_a_ref, sq_a_ref, y_b_ref, sq_b_ref = refs
    state_ref = io_refs[-1]
    first = (pl.program_id(0) == 0) & (pl.program_id(1) == 0)

    @pl.when(first)
    def _():
        y_b_ref[...] = jnp.zeros_like(y_b_ref)
        sq_b_ref[...] = jnp.zeros_like(sq_b_ref)

    @pl.when(pl.program_id(1) == 0)
    def _():
        state_ref[...] = jnp.zeros_like(state_ref)

    @pl.when(pl.program_id(1) % 2 == 0)
    def _():
        _ssd_step(*io_refs, y_a_ref, sq_a_ref, y_b_ref, sq_b_ref)

    @pl.when(pl.program_id(1) % 2 == 1)
    def _():
        _ssd_step(*io_refs, y_b_ref, sq_b_ref, y_a_ref, sq_a_ref)


def _ssd_step(xbc_ref, dt_ref, dtT_ref, zs_ref,
              alog_row_ref, alog_col_ref, dskip_ref, nw_ref, unweave_ref,
              o_ref, state_ref, y_ref, sq_ref, y_done_ref, sq_done_ref):
    q = SSD_CHUNK
    heads = SSD_HEADS // SSD_GROUPS
    gw = heads * SSD_HEAD_DIM

    dtT = dtT_ref[...]
    a_dt_col = dt_ref[...] * (-jnp.exp(alog_row_ref[...]))
    a_dt_row = dtT * (-jnp.exp(alog_col_ref[...]))
    t_row = _woven_time(lax.broadcasted_iota(jnp.int32, (q, q), 0))
    t_col = _woven_time(lax.broadcasted_iota(jnp.int32, (q, q), 1))
    causal = t_row >= t_col
    hi = lax.Precision.HIGHEST
    cs_col = jnp.dot(causal.astype(F32), a_dt_col, precision=hi, preferred_element_type=F32)
    cs_row = jnp.dot(a_dt_row, (t_row <= t_col).astype(F32), precision=hi,
                     preferred_element_type=F32)
    total = cs_row[:, q - 1:q]
    dt_decay_row = cs_row - jnp.log(dtT)
    to_end = dtT * jnp.exp(total - cs_row)
    chunk_decay = jnp.exp(total)
    r2 = lax.broadcasted_iota(jnp.int32, (gw, gw), 0)
    c2 = lax.broadcasted_iota(jnp.int32, (gw, gw), 1)
    eye = (r2 == c2).astype(BF16)
    no_rows = jnp.zeros((SSD_HEAD_DIM, q + SSD_STATE), BF16)
    sq_sum = jnp.zeros((q, LANES), F32)

    def load(g):
        b_at = SSD_INNER + g * SSD_STATE
        c_at = b_at + SSD_GROUPS * SSD_STATE
        b_g = xbc_ref[:, b_at:b_at + SSD_STATE]
        c_g = xbc_ref[:, c_at:c_at + SSD_STATE]
        cb = lax.dot_general(c_g, b_g, _NT, preferred_element_type=F32)
        x_g = xbc_ref[:, g * gw:(g + 1) * gw]
        x_t = lax.dot_general(eye, x_g, _NT, preferred_element_type=F32)
        return b_g, c_g, cb, x_g, x_t

    def weights(g, loaded):
        _, c_g, cb, _, _ = loaded
        c_f = c_g.astype(F32)
        w_parts = []
        for j in range(heads):
            h = g * heads + j
            cs_l = jnp.broadcast_to(cs_col[:, h:h + 1], (q, q))
            w_parts.append(jnp.where(causal, cb * jnp.exp(cs_l - dt_decay_row[h:h + 1]), 0.0)
                           .astype(BF16))
            w_parts.append((c_f * jnp.exp(cs_l)).astype(BF16))
        return jnp.concatenate(w_parts, axis=1)

    def finish(g, loaded, w_all, sq_sum):
        b_g, _, _, x_g, x_t = loaded
        state = state_ref[g]
        rhs = jnp.concatenate([x_t.astype(BF16), state.astype(BF16)], axis=1)
        rhs_rows, end_rows, decay_rows = [], [], []
        for j in range(heads):
            h = g * heads + j
            mine = rhs[j * SSD_HEAD_DIM:(j + 1) * SSD_HEAD_DIM]
            rhs_rows.append(jnp.concatenate([mine if jj == j else no_rows for jj in range(heads)],
                                            axis=1))
            end_rows.append(jnp.broadcast_to(to_end[h:h + 1], (SSD_HEAD_DIM, q)))
            decay_rows.append(jnp.broadcast_to(chunk_decay[h:h + 1], (SSD_HEAD_DIM, SSD_STATE)))
        y = lax.dot_general(w_all, jnp.concatenate(rhs_rows, axis=0), _NT,
                            preferred_element_type=F32)
        xcols = slice(g * gw, (g + 1) * gw)
        y = (y + dskip_ref[:, xcols] * x_g.astype(F32)) * zs_ref[:, xcols].astype(F32)
        y_ref[:, xcols] = y
        sq = y * y
        upd = jnp.dot((x_t * jnp.concatenate(end_rows, axis=0)).astype(BF16), b_g,
                      preferred_element_type=F32)
        state_ref[g] = state * jnp.concatenate(decay_rows, axis=0) + upd
        return sq_sum + sq[:, :LANES] + sq[:, LANES:]

    ms = jnp.sum(sq_done_ref[...], axis=-1, keepdims=True) * (1.0 / SSD_INNER)
    inv_rms = lax.rsqrt(ms + NORM_EPS)

    def emit(g):
        cols = slice(g * gw, (g + 1) * gw)
        y = (y_done_ref[:, cols] * inv_rms * nw_ref[:, cols]).astype(BF16)
        o_ref[:, cols] = jnp.dot(unweave_ref[...], y, preferred_element_type=F32).astype(BF16)

    loaded = {0: load(0)}
    w_all = {0: weights(0, loaded[0])}
    for g in range(SSD_GROUPS):
        if g + 1 < SSD_GROUPS:
            loaded[g + 1] = load(g + 1)
            w_all[g + 1] = weights(g + 1, loaded[g + 1])
        emit(g)
        sq_sum = finish(g, loaded.pop(g), w_all.pop(g), sq_sum)
    sq_ref[...] = sq_sum


def _ssd(xbc, dt, dtT, zs, alog_row, alog_col, dskip, nw, unweave):
    n_chunks = SEQ // SSD_CHUNK
    cur = lambda b, c: b * n_chunks + jnp.minimum(c, n_chunks - 1)
    done = lambda b, c: b * n_chunks + jnp.maximum(c - 1, 0)
    tok = lambda w, at=cur: pl.BlockSpec((SSD_CHUNK, w), lambda b, c: (at(b, c), 0))
    gw = SSD_INNER // SSD_GROUPS
    return pl.pallas_call(
        _ssd_kernel,
        grid=(BATCH, n_chunks + 1),
        in_specs=[tok(SSD_CONV_CH), tok(SSD_HEADS),
                  pl.BlockSpec((SSD_HEADS, SSD_CHUNK), lambda b, c: (0, cur(b, c))),
                  tok(SSD_INNER),
                  _resident((1, SSD_HEADS)), _resident((SSD_HEADS, 1)),
                  _resident((1, SSD_INNER)), _resident((1, SSD_INNER)),
                  _resident(unweave.shape)],
        out_specs=tok(SSD_INNER, done),
        out_shape=jax.ShapeDtypeStruct((TOKENS, SSD_INNER), BF16),
        scratch_shapes=[pltpu.VMEM((SSD_GROUPS, gw, SSD_STATE), F32)]
                       + [pltpu.VMEM((SSD_CHUNK, SSD_INNER), F32),
                          pltpu.VMEM((SSD_CHUNK, LANES), F32)] * 2,
        compiler_params=_params("arbitrary", "arbitrary"),
        name="ssd_scan",
    )(xbc, dt, dtT, zs, alog_row, alog_col, dskip, nw, unweave)


def _out_kernel(x_ref, mod_ref, o1_ref, l1_ref, o4_ref, l4_ref, o16_ref, l16_ref,
                yn_ref, expand_ref, restore_ref, n2w_ref, wg_ref, watt_ref, wssd_ref, wmix_ref,
                n3w_ref, wgu_ref, wd_ref, fnw_ref, out_ref):
    tm = x_ref.shape[0]
    mod = mod_ref[0]
    expand = lambda a: jnp.dot(a.astype(BF16), expand_ref[...], preferred_element_type=F32)
    rows = lambda ref: ref[...].reshape(tm, ref.shape[-1])

    def mix(lse_a, o_a, lse_b, o_b):
        m = jnp.maximum(lse_a, lse_b)
        e_a, e_b = jnp.exp(lse_a - m), jnp.exp(lse_b - m)
        tot = e_a + e_b
        return m + jnp.log(tot), o_b + expand(e_a * (1.0 / tot)) * (o_a - o_b)

    lse_g, o_g = mix(rows(l4_ref), rows(o4_ref).astype(F32), rows(l16_ref), rows(o16_ref).astype(F32))
    restore = restore_ref[...]
    o_g = jnp.dot(restore, o_g.astype(BF16), preferred_element_type=F32)
    lse_rest, lse_g_tok = lse_g, None
    for _ in range(3):
        piece = lse_rest.astype(BF16)
        moved = jnp.dot(restore, piece, preferred_element_type=F32)
        lse_g_tok = moved if lse_g_tok is None else lse_g_tok + moved
        lse_rest = lse_rest - piece.astype(F32)
    _, o_att = mix(l1_ref[...], o1_ref[...].astype(F32), lse_g_tok, o_g)

    y_att = jnp.dot(o_att.astype(BF16), watt_ref[...], preferred_element_type=F32)
    y_ssd = jnp.dot(yn_ref[...], wssd_ref[...], preferred_element_type=F32)
    x1 = x_ref[...]
    h_mix = _rms_mod(x1, n2w_ref[...], mod[3:4], mod[4:5]).astype(BF16)
    gates = _sigmoid(jnp.dot(h_mix, wg_ref[...], preferred_element_type=F32))
    merged = gates[:, :D_MODEL] * y_att + gates[:, D_MODEL:] * y_ssd
    y = jnp.dot(merged.astype(BF16), wmix_ref[...], preferred_element_type=F32)
    x2 = x1 + mod[5:6] * y
    h = _rms_mod(x2, n3w_ref[...], mod[6:7], mod[7:8]).astype(BF16)
    x3 = x2 + (0.5 * mod[8:9]) * _swiglu(h, wgu_ref, wd_ref)
    out_ref[...] = _rms(x3, fnw_ref[...])


def _out(x1, mod, o1, l1, o4, l4, o16, l16, yn, expand, restore, n2w, wg,
         watt, wssd, wmix, n3w, wgu, wd, fnw):
    tm = TOKEN_TILE
    tiles = SEQ // tm
    grouped = lambda w: pl.BlockSpec((None, None, REGROUP, tm // REGROUP, w),
                                     lambda i: (i // tiles, i % tiles, 0, 0, 0))
    return pl.pallas_call(
        _out_kernel,
        grid=(TOKENS // tm,),
        in_specs=[_tile_spec(D_MODEL), _mod_spec(), _tile_spec(ATT_WIDTH), _tile_spec(LANES),
                  grouped(ATT_WIDTH), grouped(LANES), grouped(ATT_WIDTH), grouped(LANES),
                  _tile_spec(SSD_INNER)]
                 + [_resident(a.shape) for a in (expand, restore, n2w, wg, watt, wssd, wmix, n3w,
                                                 wgu, wd, fnw)],
        out_specs=_tile_spec(D_MODEL),
        out_shape=jax.ShapeDtypeStruct((TOKENS, D_MODEL), F32),
        compiler_params=_params("arbitrary"),
        name="mix_out_ffn2",
    )(x1, mod, o1, l1, o4, l4, o16, l16, yn, expand, restore, n2w, wg,
      watt, wssd, wmix, n3w, wgu, wd, fnw)


def _rope_tables():
    inv_freq = ROPE_THETA ** (-jnp.arange(0, ROPE_DIM, 2, dtype=F32) / ROPE_DIM)
    d = np.arange(LANES) % ATT_HEAD_DIM
    half = ROPE_DIM // 2
    invf = jnp.where(d < ROPE_DIM, inv_freq[d % half], 0.0).astype(F32).reshape(1, LANES)
    sgn = np.where(d < half, -1.0, np.where(d < ROPE_DIM, 1.0, 0.0)).astype(np.float32)
    return invf, jnp.asarray(sgn).reshape(1, LANES)


def _regroup_matrix(tm):
    rho = np.arange(tm)
    src = REGROUP * (rho % (tm // REGROUP)) + rho // (tm // REGROUP)
    return (np.arange(tm)[None, :] == src[:, None]).astype(np.float32)


def _weave_matrix(tm):
    rho = np.arange(tm)
    src = (rho // SSD_CHUNK) * SSD_CHUNK + _woven_time(rho % SSD_CHUNK)
    return (np.arange(tm)[None, :] == src[:, None]).astype(np.float32)


def kernel(x, c, positions, w_ada, b_ada, norm1_w, ffn1_w_gu, ffn1_w_down, norm2_w, w_in,
           conv_w, conv_b, dt_bias, a_log, d_skip, ssd_norm_w, w_att_out, w_ssd_out,
           w_mix_out, norm3_w, ffn2_w_gu, ffn2_w_down, final_norm_w):
    l = 0
    row = lambda t: t.reshape(1, -1).astype(F32)
    xf = x.reshape(TOKENS, D_MODEL)
    mod = _ada(c, w_ada, b_ada, l).reshape(BATCH, N_MOD, D_MODEL)

    x1 = _ffn1(xf, mod, row(norm1_w[l]), ffn1_w_gu[l].astype(BF16), ffn1_w_down[l].astype(BF16))

    w = w_in[l].astype(BF16)
    invf, sgn = _rope_tables()
    perm = _regroup_matrix(TOKEN_TILE)
    posf = jnp.broadcast_to(positions.reshape(TOKENS, 1).astype(F32), (TOKENS, LANES))
    o1, l1, qkvg, zs, xbc, dt, dtT = _inproj(
        x1, mod, row(norm2_w[l]), posf, invf, sgn, jnp.asarray(perm, BF16),
        jnp.asarray(_weave_matrix(TOKEN_TILE), BF16), _band_bias(lambda r: r),
        w, w[:, IN_DT:IN_GATES].T,
        conv_w[l], row(conv_b[l]), row(dt_bias[l]), dt_bias[l].reshape(SSD_HEADS, 1))

    o4, l4 = _attention_regrouped(qkvg, 4)
    o16, l16 = _attention_regrouped(qkvg, 16)

    yn = _ssd(xbc, dt, dtT, zs, row(a_log[l]), a_log[l].reshape(SSD_HEADS, 1),
              row(jnp.repeat(d_skip[l], SSD_HEAD_DIM)), row(ssd_norm_w[l]),
              jnp.asarray(_weave_matrix(SSD_CHUNK).T, BF16))

    head_of_col = np.arange(ATT_WIDTH) // ATT_HEAD_DIM
    expand = jnp.asarray(np.arange(LANES)[:, None] == head_of_col[None, :], BF16)
    out = _out(x1, mod, o1, l1, o4, l4, o16, l16, yn, expand, jnp.asarray(perm.T, BF16),
               row(norm2_w[l]), w[:, IN_GATES:], w_att_out[l].astype(BF16),
               w_ssd_out[l].astype(BF16), w_mix_out[l].astype(BF16),
               row(norm3_w[l]), ffn2_w_gu[l].astype(BF16), ffn2_w_down[l].astype(BF16),
               row(final_norm_w))
    return out.reshape(BATCH, SEQ, D_MODEL)
```

```python
import jax
import jax.numpy as jnp
import numpy as np
from jax import lax
from jax.experimental import pallas as pl
from jax.experimental.pallas import tpu as pltpu

F32 = jnp.float32
BF16 = jnp.bfloat16

D_MODEL = 1024
BATCH = 2
SEQ = 8192
TOKENS = BATCH * SEQ
ATT_HEADS = 12
ATT_HEAD_DIM = 64
ATT_WIDTH = ATT_HEADS * ATT_HEAD_DIM
ROPE_DIM = ATT_HEAD_DIM // 4
ROPE_THETA = 500000.0
ATT_BLOCK = 128
SSD_INNER = 2 * D_MODEL
SSD_HEAD_DIM = 64
SSD_HEADS = SSD_INNER // SSD_HEAD_DIM
SSD_GROUPS = 8
SSD_STATE = 128
SSD_CONV = 4
SSD_CHUNK = 128
SSD_CONV_CH = SSD_INNER + 2 * SSD_GROUPS * SSD_STATE
D_FF = 2816
N_MOD = 9
NORM_EPS = 1e-6
IN_Z = 3 * ATT_WIDTH
IN_XBC = IN_Z + SSD_INNER
IN_DT = IN_XBC + SSD_CONV_CH
IN_GATES = IN_DT + SSD_HEADS

LANES = 128
SUBLANES = 8
VMEM_LIMIT = 56 * 1024 * 1024
TOKEN_TILE = 256
FFN1_TILE = 512
COL_CHUNK = 256
PROJ_CHUNK = 1024
REGROUP = 16
ATT_STEP_BLOCKS = 8
ADA_COL_TILE = 1536
NEG_BIG = -1e30

_NT = (((1,), (1,)), ((), ()))


def _params(*sem):
    return pltpu.CompilerParams(dimension_semantics=sem, vmem_limit_bytes=VMEM_LIMIT)


def _resident(shape):
    zeros = (0,) * len(shape)
    return pl.BlockSpec(shape, lambda *_: zeros, pipeline_mode=pl.Buffered(1))


def _sigmoid(x):
    return 0.5 * jnp.tanh(0.5 * x) + 0.5


def _silu(x):
    return x * _sigmoid(x)


def _rms(x, w):
    ms = jnp.sum(x * x, axis=-1, keepdims=True) * (1.0 / x.shape[-1])
    return x * lax.rsqrt(ms + NORM_EPS) * w


def _rms_mod(x, w, shift, scale):
    return _rms(x, w) * (1.0 + scale) + shift


def _swiglu(h, wgu_ref, wd_ref):
    gu = jnp.dot(h, wgu_ref[...], preferred_element_type=F32)
    a = (_silu(gu[:, :D_FF]) * gu[:, D_FF:]).astype(BF16)
    return jnp.dot(a, wd_ref[...], preferred_element_type=F32)


def _ada_kernel(ct_ref, w_ref, b_ref, o_ref):
    act = _silu(ct_ref[...])
    w = w_ref[...]
    rows = [jnp.sum(w * act[:, b:b + 1], axis=0, keepdims=True) for b in range(BATCH)]
    o_ref[...] = jnp.concatenate(rows, axis=0) + b_ref[...]


def _ada(c, w_ada, b_ada, layer):
    n = N_MOD * D_MODEL
    return pl.pallas_call(
        _ada_kernel,
        grid=(n // ADA_COL_TILE,),
        in_specs=[
            pl.BlockSpec((D_MODEL, BATCH), lambda j: (0, 0)),
            pl.BlockSpec((None, D_MODEL, ADA_COL_TILE), lambda j: (layer, 0, j)),
            pl.BlockSpec((None, 1, ADA_COL_TILE), lambda j: (layer, 0, j)),
        ],
        out_specs=pl.BlockSpec((BATCH, ADA_COL_TILE), lambda j: (0, j)),
        out_shape=jax.ShapeDtypeStruct((BATCH, n), F32),
        compiler_params=_params("arbitrary"),
        name="ada_mod",
    )(c.T, w_ada, b_ada.reshape(-1, 1, n))


def _ffn1_kernel(x_ref, mod_ref, nw_ref, wgu_ref, wd_ref, o_ref):
    x = x_ref[...]
    mod = mod_ref[0]
    h = _rms_mod(x, nw_ref[...], mod[0:1], mod[1:2]).astype(BF16)
    o_ref[...] = x + (0.5 * mod[2:3]) * _swiglu(h, wgu_ref, wd_ref)


def _tile_spec(width, tm=TOKEN_TILE):
    return pl.BlockSpec((tm, width), lambda i: (i, 0))


def _mod_spec(tm=TOKEN_TILE):
    per_batch = SEQ // tm
    return pl.BlockSpec((1, N_MOD, D_MODEL), lambda i: (i // per_batch, 0, 0))


def _ffn1(x, mod, nw, wgu, wd):
    return pl.pallas_call(
        _ffn1_kernel,
        grid=(TOKENS // FFN1_TILE,),
        in_specs=[_tile_spec(D_MODEL, FFN1_TILE), _mod_spec(FFN1_TILE), _resident((1, D_MODEL)),
                  _resident(wgu.shape), _resident(wd.shape)],
        out_specs=_tile_spec(D_MODEL, FFN1_TILE),
        out_shape=jax.ShapeDtypeStruct((TOKENS, D_MODEL), F32),
        compiler_params=_params("arbitrary"),
        name="ffn1",
    )(x, mod, nw, wgu, wd)


def _softplus(x):
    return jnp.maximum(x, 0.0) + jnp.log1p(jnp.exp(-jnp.abs(x)))


def _inproj_kernel(*refs):
    *io_refs, raw_a_ref, raw_b_ref = refs
    x_ref, mod_ref, nw_ref = io_refs[:3]
    weave_ref = io_refs[7]
    h_ref, hw_ref, hist_ref, _, _, qkv_ref, kv_prev_ref = io_refs[-7:]
    step = pl.program_id(0)
    tm = x_ref.shape[0]
    tiles_per_seq = SEQ // tm
    n_tiles = pl.num_programs(0) - 1

    @pl.when(step == 0)
    def _():
        raw_b_ref[...] = jnp.zeros_like(raw_b_ref)
        kv_prev_ref[...] = jnp.zeros_like(kv_prev_ref)

    @pl.when((step == 0) | (step % tiles_per_seq == 1))
    def _():
        hist_ref[...] = jnp.zeros_like(hist_ref)

    mod = mod_ref[0]
    h_ref[...] = _rms_mod(x_ref[...], nw_ref[...], mod[3:4], mod[4:5]).astype(BF16)
    hw_ref[...] = jnp.dot(weave_ref[...], h_ref[...], preferred_element_type=F32).astype(BF16)

    @pl.when(step % 2 == 0)
    def _():
        _inproj_step(*io_refs, raw_a_ref, raw_b_ref)

    @pl.when(step % 2 == 1)
    def _():
        _inproj_step(*io_refs, raw_b_ref, raw_a_ref)

    @pl.when(step < n_tiles - 1)
    def _():
        for part in range(2):
            cols = slice((part + 1) * ATT_WIDTH, (part + 2) * ATT_WIDTH)
            kv_prev_ref[part] = qkv_ref[tm - ATT_BLOCK:, cols]


def _inproj_step(x_ref, mod_ref, nw_ref, pos_ref, invf_ref, sgn_ref, perm_ref, weave_ref,
                 band_ref, win_ref, wdtT_ref,
                 convw_ref, convb_ref, dtb_row_ref, dtb_col_ref,
                 o1_ref, l1_ref, qkvg_ref, zs_ref, xbc_ref, dt_ref, dtT_ref,
                 h_ref, hw_ref, hist_ref, cos_ref, sin_ref, qkv_ref, kv_prev_ref,
                 raw_out_ref, raw_in_ref):
    tm = x_ref.shape[0]
    tile = jnp.minimum(pl.program_id(0), pl.num_programs(0) - 2)
    has_prev = tile % (SEQ // tm) != 0

    def xbc_part(cols):
        raw_out_ref[:, cols] = jnp.dot(hw_ref[...], win_ref[:, pl.ds(IN_XBC + cols.start, cols.size)],
                                       preferred_element_type=F32)

    def rope_tables():
        ang = pos_ref[...] * invf_ref[...]
        cos_ref[...] = jnp.cos(ang)
        sin_ref[...] = jnp.sin(ang) * sgn_ref[...]

    def rope(t):
        reps = ATT_WIDTH // LANES
        cos = jnp.concatenate([cos_ref[...]] * reps, axis=1)
        sin = jnp.concatenate([sin_ref[...]] * reps, axis=1)
        lane = lax.broadcasted_iota(jnp.int32, (tm, ATT_WIDTH), 1)
        first_half = (lane & (ATT_HEAD_DIM - 1)) < (ROPE_DIM // 2)
        partner = jnp.where(first_half,
                            pltpu.roll(t, ATT_WIDTH - ROPE_DIM // 2, 1),
                            pltpu.roll(t, ROPE_DIM // 2, 1))
        return t * cos + partner * sin

    def qkv_part(idx):
        cols = slice(idx * ATT_WIDTH, (idx + 1) * ATT_WIDTH)
        t = jnp.dot(h_ref[...], win_ref[:, cols], preferred_element_type=F32)
        if idx == 0:
            t = rope(t) * (ATT_HEAD_DIM ** -0.5)
        elif idx == 1:
            t = rope(t)
        t = t.astype(BF16)
        qkv_ref[:, cols] = t
        regrouped = jnp.dot(perm_ref[...], t, preferred_element_type=F32).astype(BF16)
        qkvg_ref[:, :, cols] = regrouped.reshape(qkvg_ref.shape[:2] + (ATT_WIDTH,))

    def z_part(cols):
        z = jnp.dot(hw_ref[...], win_ref[:, pl.ds(IN_Z + cols.start, cols.size)],
                    preferred_element_type=F32)
        zs_ref[:, cols] = _silu(z).astype(BF16)

    def dt_part():
        hw = hw_ref[...]
        dt_raw = jnp.dot(hw, win_ref[:, IN_DT:IN_DT + LANES],
                         preferred_element_type=F32)[:, :SSD_HEADS]
        dt_ref[...] = _softplus(dt_raw + dtb_row_ref[...])
        dtT_raw = lax.dot_general(wdtT_ref[...], hw, _NT, preferred_element_type=F32)
        dtT_ref[...] = _softplus(dtT_raw + dtb_col_ref[...])

    tail_rows = (SSD_CONV - 1) * SUBLANES
    tail_start = SSD_CHUNK - tail_rows
    first_sublane = lax.broadcasted_iota(jnp.int32, (SUBLANES, COL_CHUNK), 0) == 0

    def conv_chunk(src):
        raw = raw_in_ref[:, src]
        taps = [convw_ref[k:k + 1, src] for k in range(SSD_CONV)]
        prev_tail = hist_ref[:, src]
        pieces = []
        for ck in range(tm // SSD_CHUNK):
            cur = raw[ck * SSD_CHUNK:(ck + 1) * SSD_CHUNK]
            cur_tail = cur[tail_start:]
            wrapped = []
            for k in range(SSD_CONV - 1):
                rows = slice(k * SUBLANES, (k + 1) * SUBLANES)
                wrapped.append(jnp.where(first_sublane, pltpu.roll(prev_tail[rows], 1, 0),
                                         pltpu.roll(cur_tail[rows], 1, 0)))
            ext = jnp.concatenate(wrapped + [cur], axis=0)
            acc = cur * taps[SSD_CONV - 1] + convb_ref[:, src]
            for s in range(1, SSD_CONV):
                lo = (SSD_CONV - 1 - s) * SUBLANES
                acc = acc + ext[lo:lo + SSD_CHUNK] * taps[SSD_CONV - 1 - s]
            pieces.append(acc)
            prev_tail = cur_tail
        hist_ref[:, src] = prev_tail
        xbc_ref[:, src] = _silu(jnp.concatenate(pieces, axis=0)).astype(BF16)

    blk = ATT_BLOCK
    k_at, v_at = ATT_WIDTH, 2 * ATT_WIDTH

    def with_prev(j, part, at, cols):
        cur_cols = slice(at + cols.start, at + cols.stop)
        if j == 0:
            return jnp.concatenate([kv_prev_ref[part, :, cols], qkv_ref[:blk, cur_cols]], axis=0)
        return qkv_ref[(j - 1) * blk:(j + 1) * blk, cur_cols]

    def emit(j, o, lse):
        o1_ref[j * blk:(j + 1) * blk, :] = o
        l1_ref[j * blk:(j + 1) * blk, :] = lse

    band = band_ref[...]
    attention = _attention_units(lambda j, cols: qkv_ref[j * blk:(j + 1) * blk, cols],
                                 lambda j, cols: with_prev(j, 0, k_at, cols),
                                 lambda j, cols: with_prev(j, 1, v_at, cols),
                                 band, _first_block_bias(band, has_prev), tm // blk, emit)

    chunks = lambda width, w: [pl.ds(c * w, w) for c in range(width // w)]
    conv = chunks(SSD_CONV_CH, COL_CHUNK)
    for fn, args in ((qkv_part, (2,)), (rope_tables, ()), (qkv_part, (0,)), (qkv_part, (1,))):
        fn(*args)
        conv_chunk(conv.pop(0))
    parts = ([(xbc_part, (c,)) for c in chunks(SSD_CONV_CH, PROJ_CHUNK)]
             + [(z_part, (c,)) for c in chunks(SSD_INNER, PROJ_CHUNK)] + [(dt_part, ())])
    for n, (fn, args) in enumerate(parts):
        fn(*args)
        left = len(parts) - n
        for _ in range(-(-len(conv) // left)):
            conv_chunk(conv.pop(0))
        for _ in range(-(-len(attention) // left)):
            attention.pop(0)()
    assert not conv and not attention


def _inproj(x1, mod, nw, posf, invf, sgn, perm, weave, band, w_in, wdtT,
            conv_w, conv_b, dtb_row, dtb_col):
    tm = TOKEN_TILE
    tiles = SEQ // tm
    n_tiles = TOKENS // tm
    cur = lambda i: jnp.minimum(i, n_tiles - 1)
    done = lambda i: jnp.maximum(i - 1, 0)
    tok = lambda w, d, at=cur: (jax.ShapeDtypeStruct((TOKENS, w), d),
                                pl.BlockSpec((tm, w), lambda i: (at(i), 0)))
    grouped = (jax.ShapeDtypeStruct((BATCH, tiles, REGROUP, tm // REGROUP, 3 * ATT_WIDTH), BF16),
               pl.BlockSpec((None, None, REGROUP, tm // REGROUP, 3 * ATT_WIDTH),
                            lambda i: (cur(i) // tiles, cur(i) % tiles, 0, 0, 0)))
    outs = [tok(ATT_WIDTH, BF16), tok(LANES, F32),
        grouped,
        tok(SSD_INNER, BF16),
        tok(SSD_CONV_CH, BF16, done),
        tok(SSD_HEADS, F32),
        (jax.ShapeDtypeStruct((SSD_HEADS, TOKENS), F32),
         pl.BlockSpec((SSD_HEADS, tm), lambda i: (0, cur(i)))),
    ]
    return pl.pallas_call(
        _inproj_kernel,
        grid=(n_tiles + 1,),
        in_specs=[tok(D_MODEL, F32)[1],
                  pl.BlockSpec((1, N_MOD, D_MODEL), lambda i: (cur(i) // tiles, 0, 0)),
                  _resident((1, D_MODEL)),
                  tok(LANES, F32)[1], _resident((1, LANES)), _resident((1, LANES)),
                  _resident(perm.shape), _resident(weave.shape), _resident(band.shape),
                  _resident(w_in.shape), _resident(wdtT.shape),
                  _resident(conv_w.shape), _resident(conv_b.shape),
                  _resident(dtb_row.shape), _resident(dtb_col.shape)],
        out_specs=[s for _, s in outs],
        out_shape=[s for s, _ in outs],
        scratch_shapes=[pltpu.VMEM((tm, D_MODEL), BF16), pltpu.VMEM((tm, D_MODEL), BF16),
                        pltpu.VMEM(((SSD_CONV - 1) * SUBLANES, SSD_CONV_CH), F32),
                        pltpu.VMEM((tm, LANES), F32), pltpu.VMEM((tm, LANES), F32),
                        pltpu.VMEM((tm, 3 * ATT_WIDTH), BF16),
                        pltpu.VMEM((2, ATT_BLOCK, ATT_WIDTH), BF16),
                        pltpu.VMEM((tm, SSD_CONV_CH), F32), pltpu.VMEM((tm, SSD_CONV_CH), F32)],
        compiler_params=_params("arbitrary"),
        name="in_proj",
    )(x1, mod, nw, posf, invf, sgn, perm, weave, band, w_in, wdtT,
      conv_w, conv_b, dtb_row, dtb_col)


def _band_bias(row_pos):
    blk = ATT_BLOCK
    pos = row_pos(np.arange(blk))
    k_pos = np.concatenate([pos, pos + blk])
    dist = pos[:, None] + blk - k_pos[None, :]
    return jnp.asarray(np.where((dist >= 0) & (dist <= blk), 0.0, NEG_BIG), F32)


def _first_block_bias(band, has_prev):
    ki = lax.broadcasted_iota(jnp.int32, band.shape, 1)
    return jnp.where((ki >= ATT_BLOCK) | has_prev, band, NEG_BIG)


def _attention_units(get_q, get_k, get_v, band, first_bias, n_blocks, emit):
    blk = ATT_BLOCK
    lane_row = lax.broadcasted_iota(jnp.int32, (1, LANES), 1)
    lane = lax.broadcasted_iota(jnp.int32, (blk, LANES), 1)
    low = lane < ATT_HEAD_DIM
    head_mask = [(lane_row < ATT_HEAD_DIM).astype(BF16), (lane_row >= ATT_HEAD_DIM).astype(BF16)]
    units = [(j, pair, half) for j in range(n_blocks)
             for pair in range(ATT_HEADS // 2) for half in range(2)]
    carry = {"scores": None, "outs": [], "o_pairs": [], "lse": jnp.zeros((blk, LANES), F32)}

    def scores(unit):
        j, pair, half = unit
        cols = slice(pair * LANES, (pair + 1) * LANES)
        s = lax.dot_general(get_q(j, cols) * head_mask[half], get_k(j, cols), _NT,
                            preferred_element_type=F32)
        return s + (first_bias if j == 0 else band)

    def run(n):
        j, pair, half = units[n]
        s = scores(units[0]) if n == 0 else carry["scores"]
        if n + 1 < len(units):
            carry["scores"] = scores(units[n + 1])
        m = jnp.max(s, axis=1, keepdims=True)
        p = jnp.exp(s - m)
        den = jnp.sum(p, axis=1, keepdims=True)
        o = jnp.dot(p.astype(BF16), get_v(j, slice(pair * LANES, (pair + 1) * LANES)),
                    preferred_element_type=F32)
        carry["outs"].append(o * (1.0 / den))
        carry["lse"] = jnp.where(lane == 2 * pair + half, m + jnp.log(den), carry["lse"])
        if half == 1:
            outs = carry["outs"]
            carry["o_pairs"].append(jnp.where(low, outs[0], outs[1]).astype(BF16))
            carry["outs"] = []
            if pair == ATT_HEADS // 2 - 1:
                emit(j, jnp.concatenate(carry["o_pairs"], axis=1), carry["lse"])
                carry["o_pairs"], carry["lse"] = [], jnp.zeros((blk, LANES), F32)

    return [lambda n=n: run(n) for n in range(len(units))]


def _attn_kernel(band_ref, q_ref, kp_ref, kc_ref, vp_ref, vc_ref, o_ref, lse_ref):
    blk = ATT_BLOCK
    flat = lambda ref: ref[...].reshape(-1, ref.shape[-1])
    q = flat(q_ref)
    k = jnp.concatenate([flat(kp_ref), flat(kc_ref)], axis=0)
    v = jnp.concatenate([flat(vp_ref), flat(vc_ref)], axis=0)
    band = band_ref[...]
    o_blocks, lse_blocks = [], []

    def emit(j, o, lse):
        o_blocks.append(o)
        lse_blocks.append(lse)

    for unit in _attention_units(lambda j, cols: q[j * blk:(j + 1) * blk, cols],
                                 lambda j, cols: k[j * blk:(j + 2) * blk, cols],
                                 lambda j, cols: v[j * blk:(j + 2) * blk, cols],
                                 band, _first_block_bias(band, pl.program_id(2) > 0),
                                 q.shape[0] // blk, emit):
        unit()
    o_ref[...] = jnp.concatenate(o_blocks, axis=0).reshape(o_ref.shape)
    lse_ref[...] = jnp.concatenate(lse_blocks, axis=0).reshape(lse_ref.shape)


def _attention_regrouped(qkvg, r):
    tiles, rows = qkvg.shape[1], qkvg.shape[3]
    fold = REGROUP // r
    tiles_per_blk = ATT_BLOCK // (fold * rows)
    step_blocks = min(ATT_STEP_BLOCKS, tiles // tiles_per_blk)
    n_blk = tiles // (step_blocks * tiles_per_blk)
    shape6 = lambda w: (BATCH, tiles, fold, r, rows, w)
    qkvg = qkvg.reshape(shape6(3 * ATT_WIDTH))
    blk6 = lambda n_tiles, w: (None, n_tiles, fold, None, rows, w)
    cur = lambda w, part=0: pl.BlockSpec(blk6(step_blocks * tiles_per_blk, w),
                                         lambda b, j, i: (b, i, 0, j, 0, part))
    prev = lambda part: pl.BlockSpec(
        blk6(tiles_per_blk, ATT_WIDTH),
        lambda b, j, i: (b, jnp.maximum(step_blocks * i - 1, 0), 0, j, 0, part))
    per_tile = fold * rows

    def row_pos(rho):
        t = rho // per_tile
        a = (rho // rows) % fold
        return t * per_tile + (rho % rows) * fold + a

    o, lse = pl.pallas_call(
        _attn_kernel,
        grid=(BATCH, r, n_blk),
        in_specs=[_resident((ATT_BLOCK, 2 * ATT_BLOCK)), cur(ATT_WIDTH, 0),
                  prev(1), cur(ATT_WIDTH, 1), prev(2), cur(ATT_WIDTH, 2)],
        out_specs=[cur(ATT_WIDTH), cur(LANES)],
        out_shape=[jax.ShapeDtypeStruct(shape6(ATT_WIDTH), BF16),
                   jax.ShapeDtypeStruct(shape6(LANES), F32)],
        compiler_params=_params("arbitrary", "arbitrary", "arbitrary"),
        name=f"attn_dil{r}",
    )(_band_bias(row_pos), qkvg, qkvg, qkvg, qkvg, qkvg)
    grouped5 = (BATCH, tiles, REGROUP, rows)
    return o.reshape(grouped5 + (ATT_WIDTH,)), lse.reshape(grouped5 + (LANES,))


def _woven_time(i):
    return ((i & (SUBLANES - 1)) << 4) | (i >> 3)


def _ssd_kernel(*refs):
    *io_refs, y_a_ref, sq_a_ref, y_b_ref, sq_b_ref = refs
    state_ref = io_refs[-1]
    first = (pl.program_id(0) == 0) & (pl.program_id(1) == 0)

    @pl.when(first)
    def _():
        y_b_ref[...] = jnp.zeros_like(y_b_ref)
        sq_b_ref[...] = jnp.zeros_like(sq_b_ref)

    @pl.when(pl.program_id(1) == 0)
    def _():
        state_ref[...] = jnp.zeros_like(state_ref)

    @pl.when(pl.program_id(1) % 2 == 0)
    def _():
        _ssd_step(*io_refs, y_a_ref, sq_a_ref, y_b_ref, sq_b_ref)

    @pl.when(pl.program_id(1) % 2 == 1)
    def _():
        _ssd_step(*io_refs, y_b_ref, sq_b_ref, y_a_ref, sq_a_ref)


def _ssd_step(xbc_ref, dt_ref, dtT_ref, zs_ref,
              alog_row_ref, alog_col_ref, dskip_ref, nw_ref, unweave_ref,
              o_ref, state_ref, y_ref, sq_ref, y_done_ref, sq_done_ref):
    q = SSD_CHUNK
    heads = SSD_HEADS // SSD_GROUPS
    gw = heads * SSD_HEAD_DIM

    dtT = dtT_ref[...]
    a_dt_col = dt_ref[...] * (-jnp.exp(alog_row_ref[...]))
    a_dt_row = dtT * (-jnp.exp(alog_col_ref[...]))
    t_row = _woven_time(lax.broadcasted_iota(jnp.int32, (q, q), 0))
    t_col = _woven_time(lax.broadcasted_iota(jnp.int32, (q, q), 1))
    causal = t_row >= t_col
    hi = lax.Precision.HIGHEST
    cs_col = jnp.dot(causal.astype(F32), a_dt_col, precision=hi, preferred_element_type=F32)
    cs_row = jnp.dot(a_dt_row, (t_row <= t_col).astype(F32), precision=hi,
                     preferred_element_type=F32)
    total = cs_row[:, q - 1:q]
    dt_decay_row = cs_row - jnp.log(dtT)
    to_end = dtT * jnp.exp(total - cs_row)
    chunk_decay = jnp.exp(total)
    r2 = lax.broadcasted_iota(jnp.int32, (gw, gw), 0)
    c2 = lax.broadcasted_iota(jnp.int32, (gw, gw), 1)
    eye = (r2 == c2).astype(BF16)
    no_rows = jnp.zeros((SSD_HEAD_DIM, q + SSD_STATE), BF16)
    sq_sum = jnp.zeros((q, LANES), F32)

    def load(g):
        b_at = SSD_INNER + g * SSD_STATE
        c_at = b_at + SSD_GROUPS * SSD_STATE
        b_g = xbc_ref[:, b_at:b_at + SSD_STATE]
        c_g = xbc_ref[:, c_at:c_at + SSD_STATE]
        cb = lax.dot_general(c_g, b_g, _NT, preferred_element_type=F32)
        x_g = xbc_ref[:, g * gw:(g + 1) * gw]
        x_t = lax.dot_general(eye, x_g, _NT, preferred_element_type=F32)
        return b_g, c_g, cb, x_g, x_t

    def weights(g, loaded):
        _, c_g, cb, _, _ = loaded
        c_f = c_g.astype(F32)
        w_parts = []
        for j in range(heads):
            h = g * heads + j
            cs_l = jnp.broadcast_to(cs_col[:, h:h + 1], (q, q))
            w_parts.append(jnp.where(causal, cb * jnp.exp(cs_l - dt_decay_row[h:h + 1]), 0.0)
                           .astype(BF16))
            w_parts.append((c_f * jnp.exp(cs_l)).astype(BF16))
        return jnp.concatenate(w_parts, axis=1)

    def finish(g, loaded, w_all, sq_sum):
        b_g, _, _, x_g, x_t = loaded
        state = state_ref[g]
        rhs = jnp.concatenate([x_t.astype(BF16), state.astype(BF16)], axis=1)
        rhs_rows, end_rows, decay_rows = [], [], []
        for j in range(heads):
            h = g * heads + j
            mine = rhs[j * SSD_HEAD_DIM:(j + 1) * SSD_HEAD_DIM]
            rhs_rows.append(jnp.concatenate([mine if jj == j else no_rows for jj in range(heads)],
                                            axis=1))
            end_rows.append(jnp.broadcast_to(to_end[h:h + 1], (SSD_HEAD_DIM, q)))
            decay_rows.append(jnp.broadcast_to(chunk_decay[h:h + 1], (SSD_HEAD_DIM, SSD_STATE)))
        y = lax.dot_general(w_all, jnp.concatenate(rhs_rows, axis=0), _NT,
                            preferred_element_type=F32)
        xcols = slice(g * gw, (g + 1) * gw)
        y = (y + dskip_ref[:, xcols] * x_g.astype(F32)) * zs_ref[:, xcols].astype(F32)
        y_ref[:, xcols] = y
        sq = y * y
        upd = jnp.dot((x_t * jnp.concatenate(end_rows, axis=0)).astype(BF16), b_g,
                      preferred_element_type=F32)
        state_ref[g] = state * jnp.concatenate(decay_rows, axis=0) + upd
        return sq_sum + sq[:, :LANES] + sq[:, LANES:]

    ms = jnp.sum(sq_done_ref[...], axis=-1, keepdims=True) * (1.0 / SSD_INNER)
    inv_rms = lax.rsqrt(ms + NORM_EPS)

    def emit(g):
        cols = slice(g * gw, (g + 1) * gw)
        y = (y_done_ref[:, cols] * inv_rms * nw_ref[:, cols]).astype(BF16)
        o_ref[:, cols] = jnp.dot(unweave_ref[...], y, preferred_element_type=F32).astype(BF16)

    loaded = {0: load(0)}
    w_all = {0: weights(0, loaded[0])}
    for g in range(SSD_GROUPS):
        if g + 1 < SSD_GROUPS:
            loaded[g + 1] = load(g + 1)
            w_all[g + 1] = weights(g + 1, loaded[g + 1])
        emit(g)
        sq_sum = finish(g, loaded.pop(g), w_all.pop(g), sq_sum)
    sq_ref[...] = sq_sum


def _ssd(xbc, dt, dtT, zs, alog_row, alog_col, dskip, nw, unweave):
    n_chunks = SEQ // SSD_CHUNK
    cur = lambda b, c: b * n_chunks + jnp.minimum(c, n_chunks - 1)
    done = lambda b, c: b * n_chunks + jnp.maximum(c - 1, 0)
    tok = lambda w, at=cur: pl.BlockSpec((SSD_CHUNK, w), lambda b, c: (at(b, c), 0))
    gw = SSD_INNER // SSD_GROUPS
    return pl.pallas_call(
        _ssd_kernel,
        grid=(BATCH, n_chunks + 1),
        in_specs=[tok(SSD_CONV_CH), tok(SSD_HEADS),
                  pl.BlockSpec((SSD_HEADS, SSD_CHUNK), lambda b, c: (0, cur(b, c))),
                  tok(SSD_INNER),
                  _resident((1, SSD_HEADS)), _resident((SSD_HEADS, 1)),
                  _resident((1, SSD_INNER)), _resident((1, SSD_INNER)),
                  _resident(unweave.shape)],
        out_specs=tok(SSD_INNER, done),
        out_shape=jax.ShapeDtypeStruct((TOKENS, SSD_INNER), BF16),
        scratch_shapes=[pltpu.VMEM((SSD_GROUPS, gw, SSD_STATE), F32)]
                       + [pltpu.VMEM((SSD_CHUNK, SSD_INNER), F32),
                          pltpu.VMEM((SSD_CHUNK, LANES), F32)] * 2,
        compiler_params=_params("arbitrary", "arbitrary"),
        name="ssd_scan",
    )(xbc, dt, dtT, zs, alog_row, alog_col, dskip, nw, unweave)


def _out_kernel(x_ref, mod_ref, o1_ref, l1_ref, o4_ref, l4_ref, o16_ref, l16_ref,
                yn_ref, expand_ref, restore_ref, n2w_ref, wg_ref, watt_ref, wssd_ref, wmix_ref,
                n3w_ref, wgu_ref, wd_ref, fnw_ref, out_ref):
    tm = x_ref.shape[0]
    mod = mod_ref[0]
    expand = lambda a: jnp.dot(a.astype(BF16), expand_ref[...], preferred_element_type=F32)
    rows = lambda ref: ref[...].reshape(tm, ref.shape[-1])

    def mix(lse_a, o_a, lse_b, o_b):
        m = jnp.maximum(lse_a, lse_b)
        e_a, e_b = jnp.exp(lse_a - m), jnp.exp(lse_b - m)
        tot = e_a + e_b
        return m + jnp.log(tot), o_b + expand(e_a * (1.0 / tot)) * (o_a - o_b)

    lse_g, o_g = mix(rows(l4_ref), rows(o4_ref).astype(F32), rows(l16_ref), rows(o16_ref).astype(F32))
    restore = restore_ref[...]
    o_g = jnp.dot(restore, o_g.astype(BF16), preferred_element_type=F32)
    lse_rest, lse_g_tok = lse_g, None
    for _ in range(3):
        piece = lse_rest.astype(BF16)
        moved = jnp.dot(restore, piece, preferred_element_type=F32)
        lse_g_tok = moved if lse_g_tok is None else lse_g_tok + moved
        lse_rest = lse_rest - piece.astype(F32)
    _, o_att = mix(l1_ref[...], o1_ref[...].astype(F32), lse_g_tok, o_g)

    y_att = jnp.dot(o_att.astype(BF16), watt_ref[...], preferred_element_type=F32)
    y_ssd = jnp.dot(yn_ref[...], wssd_ref[...], preferred_element_type=F32)
    x1 = x_ref[...]
    h_mix = _rms_mod(x1, n2w_ref[...], mod[3:4], mod[4:5]).astype(BF16)
    gates = _sigmoid(jnp.dot(h_mix, wg_ref[...], preferred_element_type=F32))
    merged = gates[:, :D_MODEL] * y_att + gates[:, D_MODEL:] * y_ssd
    y = jnp.dot(merged.astype(BF16), wmix_ref[...], preferred_element_type=F32)
    x2 = x1 + mod[5:6] * y
    h = _rms_mod(x2, n3w_ref[...], mod[6:7], mod[7:8]).astype(BF16)
    x3 = x2 + (0.5 * mod[8:9]) * _swiglu(h, wgu_ref, wd_ref)
    out_ref[...] = _rms(x3, fnw_ref[...])


def _out(x1, mod, o1, l1, o4, l4, o16, l16, yn, expand, restore, n2w, wg,
         watt, wssd, wmix, n3w, wgu, wd, fnw):
    tm = TOKEN_TILE
    tiles = SEQ // tm
    grouped = lambda w: pl.BlockSpec((None, None, REGROUP, tm // REGROUP, w),
                                     lambda i: (i // tiles, i % tiles, 0, 0, 0))
    return pl.pallas_call(
        _out_kernel,
        grid=(TOKENS // tm,),
        in_specs=[_tile_spec(D_MODEL), _mod_spec(), _tile_spec(ATT_WIDTH), _tile_spec(LANES),
                  grouped(ATT_WIDTH), grouped(LANES), grouped(ATT_WIDTH), grouped(LANES),
                  _tile_spec(SSD_INNER)]
                 + [_resident(a.shape) for a in (expand, restore, n2w, wg, watt, wssd, wmix, n3w,
                                                 wgu, wd, fnw)],
        out_specs=_tile_spec(D_MODEL),
        out_shape=jax.ShapeDtypeStruct((TOKENS, D_MODEL), F32),
        compiler_params=_params("arbitrary"),
        name="mix_out_ffn2",
    )(x1, mod, o1, l1, o4, l4, o16, l16, yn, expand, restore, n2w, wg,
      watt, wssd, wmix, n3w, wgu, wd, fnw)


def _rope_tables():
    inv_freq = ROPE_THETA ** (-jnp.arange(0, ROPE_DIM, 2, dtype=F32) / ROPE_DIM)
    d = np.arange(LANES) % ATT_HEAD_DIM
    half = ROPE_DIM // 2
    invf = jnp.where(d < ROPE_DIM, inv_freq[d % half], 0.0).astype(F32).reshape(1, LANES)
    sgn = np.where(d < half, -1.0, np.where(d < ROPE_DIM, 1.0, 0.0)).astype(np.float32)
    return invf, jnp.asarray(sgn).reshape(1, LANES)


def _regroup_matrix(tm):
    rho = np.arange(tm)
    src = REGROUP * (rho % (tm // REGROUP)) + rho // (tm // REGROUP)
    return (np.arange(tm)[None, :] == src[:, None]).astype(np.float32)


def _weave_matrix(tm):
    rho = np.arange(tm)
    src = (rho // SSD_CHUNK) * SSD_CHUNK + _woven_time(rho % SSD_CHUNK)
    return (np.arange(tm)[None, :] == src[:, None]).astype(np.float32)


def kernel(x, c, positions, w_ada, b_ada, norm1_w, ffn1_w_gu, ffn1_w_down, norm2_w, w_in,
           conv_w, conv_b, dt_bias, a_log, d_skip, ssd_norm_w, w_att_out, w_ssd_out,
           w_mix_out, norm3_w, ffn2_w_gu, ffn2_w_down, final_norm_w):
    l = 0
    row = lambda t: t.reshape(1, -1).astype(F32)
    xf = x.reshape(TOKENS, D_MODEL)
    mod = _ada(c, w_ada, b_ada, l).reshape(BATCH, N_MOD, D_MODEL)

    x1 = _ffn1(xf, mod, row(norm1_w[l]), ffn1_w_gu[l].astype(BF16), ffn1_w_down[l].astype(BF16))

    w = w_in[l].astype(BF16)
    invf, sgn = _rope_tables()
    perm = _regroup_matrix(TOKEN_TILE)
    posf = jnp.broadcast_to(positions.reshape(TOKENS, 1).astype(F32), (TOKENS, LANES))
    o1, l1, qkvg, zs, xbc, dt, dtT = _inproj(
        x1, mod, row(norm2_w[l]), posf, invf, sgn, jnp.asarray(perm, BF16),
        jnp.asarray(_weave_matrix(TOKEN_TILE), BF16), _band_bias(lambda r: r),
        w, w[:, IN_DT:IN_GATES].T,
        conv_w[l], row(conv_b[l]), row(dt_bias[l]), dt_bias[l].reshape(SSD_HEADS, 1))

    o4, l4 = _attention_regrouped(qkvg, 4)
    o16, l16 = _attention_regrouped(qkvg, 16)

    yn = _ssd(xbc, dt, dtT, zs, row(a_log[l]), a_log[l].reshape(SSD_HEADS, 1),
              row(jnp.repeat(d_skip[l], SSD_HEAD_DIM)), row(ssd_norm_w[l]),
              jnp.asarray(_weave_matrix(SSD_CHUNK).T, BF16))

    head_of_col = np.arange(ATT_WIDTH) // ATT_HEAD_DIM
    expand = jnp.asarray(np.arange(LANES)[:, None] == head_of_col[None, :], BF16)
    out = _out(x1, mod, o1, l1, o4, l4, o16, l16, yn, expand, jnp.asarray(perm.T, BF16),
               row(norm2_w[l]), w[:, IN_GATES:], w_att_out[l].astype(BF16),
               w_ssd_out[l].astype(BF16), w_mix_out[l].astype(BF16),
               row(norm3_w[l]), ffn2_w_gu[l].astype(BF16), ffn2_w_down[l].astype(BF16),
               row(final_norm_w))
    return out.reshape(BATCH, SEQ, D_MODEL)
```

```python
import jax
import jax.numpy as jnp
import numpy as np
from jax import lax
from jax.experimental import pallas as pl
from jax.experimental.pallas import tpu as pltpu

F32 = jnp.float32
BF16 = jnp.bfloat16

D_MODEL = 1024
BATCH = 2
SEQ = 8192
TOKENS = BATCH * SEQ
ATT_HEADS = 12
ATT_HEAD_DIM = 64
ATT_WIDTH = ATT_HEADS * ATT_HEAD_DIM
ROPE_DIM = ATT_HEAD_DIM // 4
ROPE_THETA = 500000.0
ATT_BLOCK = 128
SSD_INNER = 2 * D_MODEL
SSD_HEAD_DIM = 64
SSD_HEADS = SSD_INNER // SSD_HEAD_DIM
SSD_GROUPS = 8
SSD_STATE = 128
SSD_CONV = 4
SSD_CHUNK = 128
SSD_CONV_CH = SSD_INNER + 2 * SSD_GROUPS * SSD_STATE
D_FF = 2816
N_MOD = 9
NORM_EPS = 1e-6
IN_Z = 3 * ATT_WIDTH
IN_XBC = IN_Z + SSD_INNER
IN_DT = IN_XBC + SSD_CONV_CH
IN_GATES = IN_DT + SSD_HEADS

LANES = 128
SUBLANES = 8
VMEM_LIMIT = 56 * 1024 * 1024
TOKEN_TILE = 256
FFN1_TILE = 512
COL_CHUNK = 256
PROJ_CHUNK = 1024
REGROUP = 16
ATT_STEP_BLOCKS = 8
ADA_COL_TILE = 1536
NEG_BIG = -1e30

_NT = (((1,), (1,)), ((), ()))


def _params(*sem):
    return pltpu.CompilerParams(dimension_semantics=sem, vmem_limit_bytes=VMEM_LIMIT)


def _resident(shape):
    zeros = (0,) * len(shape)
    return pl.BlockSpec(shape, lambda *_: zeros, pipeline_mode=pl.Buffered(1))


def _sigmoid(x):
    return 0.5 * jnp.tanh(0.5 * x) + 0.5


def _silu(x):
    return x * _sigmoid(x)


def _rms(x, w):
    ms = jnp.sum(x * x, axis=-1, keepdims=True) * (1.0 / x.shape[-1])
    return x * lax.rsqrt(ms + NORM_EPS) * w


def _rms_mod(x, w, shift, scale):
    return _rms(x, w) * (1.0 + scale) + shift


def _swiglu(h, wgu_ref, wd_ref):
    gu = jnp.dot(h, wgu_ref[...], preferred_element_type=F32)
    a = (_silu(gu[:, :D_FF]) * gu[:, D_FF:]).astype(BF16)
    return jnp.dot(a, wd_ref[...], preferred_element_type=F32)


def _ada_kernel(ct_ref, w_ref, b_ref, o_ref):
    act = _silu(ct_ref[...])
    w = w_ref[...]
    rows = [jnp.sum(w * act[:, b:b + 1], axis=0, keepdims=True) for b in range(BATCH)]
    o_ref[...] = jnp.concatenate(rows, axis=0) + b_ref[...]


def _ada(c, w_ada, b_ada, layer):
    n = N_MOD * D_MODEL
    return pl.pallas_call(
        _ada_kernel,
        grid=(n // ADA_COL_TILE,),
        in_specs=[
            pl.BlockSpec((D_MODEL, BATCH), lambda j: (0, 0)),
            pl.BlockSpec((None, D_MODEL, ADA_COL_TILE), lambda j: (layer, 0, j)),
            pl.BlockSpec((None, 1, ADA_COL_TILE), lambda j: (layer, 0, j)),
        ],
        out_specs=pl.BlockSpec((BATCH, ADA_COL_TILE), lambda j: (0, j)),
        out_shape=jax.ShapeDtypeStruct((BATCH, n), F32),
        compiler_params=_params("arbitrary"),
        name="ada_mod",
    )(c.T, w_ada, b_ada.reshape(-1, 1, n))


def _ffn1_kernel(x_ref, mod_ref, nw_ref, wgu_ref, wd_ref, o_ref):
    x = x_ref[...]
    mod = mod_ref[0]
    h = _rms_mod(x, nw_ref[...], mod[0:1], mod[1:2]).astype(BF16)
    o_ref[...] = x + (0.5 * mod[2:3]) * _swiglu(h, wgu_ref, wd_ref)


def _tile_spec(width, tm=TOKEN_TILE):
    return pl.BlockSpec((tm, width), lambda i: (i, 0))


def _mod_spec(tm=TOKEN_TILE):
    per_batch = SEQ // tm
    return pl.BlockSpec((1, N_MOD, D_MODEL), lambda i: (i // per_batch, 0, 0))


def _ffn1(x, mod, nw, wgu, wd):
    return pl.pallas_call(
        _ffn1_kernel,
        grid=(TOKENS // FFN1_TILE,),
        in_specs=[_tile_spec(D_MODEL, FFN1_TILE), _mod_spec(FFN1_TILE), _resident((1, D_MODEL)),
                  _resident(wgu.shape), _resident(wd.shape)],
        out_specs=_tile_spec(D_MODEL, FFN1_TILE),
        out_shape=jax.ShapeDtypeStruct((TOKENS, D_MODEL), F32),
        compiler_params=_params("arbitrary"),
        name="ffn1",
    )(x, mod, nw, wgu, wd)


def _softplus(x):
    return jnp.maximum(x, 0.0) + jnp.log1p(jnp.exp(-jnp.abs(x)))


def _inproj_kernel(*refs):
    *io_refs, raw_a_ref, raw_b_ref = refs
    x_ref, mod_ref, nw_ref = io_refs[:3]
    weave_ref = io_refs[7]
    h_ref, hw_ref, hist_ref, _, _, qkv_ref, kv_prev_ref = io_refs[-7:]
    step = pl.program_id(0)
    tm = x_ref.shape[0]
    tiles_per_seq = SEQ // tm
    n_tiles = pl.num_programs(0) - 1

    @pl.when(step == 0)
    def _():
        raw_b_ref[...] = jnp.zeros_like(raw_b_ref)
        kv_prev_ref[...] = jnp.zeros_like(kv_prev_ref)

    @pl.when((step == 0) | (step % tiles_per_seq == 1))
    def _():
        hist_ref[...] = jnp.zeros_like(hist_ref)

    mod = mod_ref[0]
    h_ref[...] = _rms_mod(x_ref[...], nw_ref[...], mod[3:4], mod[4:5]).astype(BF16)
    hw_ref[...] = jnp.dot(weave_ref[...], h_ref[...], preferred_element_type=F32).astype(BF16)

    @pl.when(step % 2 == 0)
    def _():
        _inproj_step(*io_refs, raw_a_ref, raw_b_ref)

    @pl.when(step % 2 == 1)
    def _():
        _inproj_step(*io_refs, raw_b_ref, raw_a_ref)

    @pl.when(step < n_tiles - 1)
    def _():
        for part in range(2):
            cols = slice((part + 1) * ATT_WIDTH, (part + 2) * ATT_WIDTH)
            kv_prev_ref[part] = qkv_ref[tm - ATT_BLOCK:, cols]


def _inproj_step(x_ref, mod_ref, nw_ref, pos_ref, invf_ref, sgn_ref, perm_ref, weave_ref,
                 band_ref, win_ref, wdtT_ref,
                 convw_ref, convb_ref, dtb_row_ref, dtb_col_ref,
                 o1_ref, l1_ref, qkvg_ref, zs_ref, xbc_ref, dt_ref, dtT_ref,
                 h_ref, hw_ref, hist_ref, cos_ref, sin_ref, qkv_ref, kv_prev_ref,
                 raw_out_ref, raw_in_ref):
    tm = x_ref.shape[0]
    tile = jnp.minimum(pl.program_id(0), pl.num_programs(0) - 2)
    has_prev = tile % (SEQ // tm) != 0

    def xbc_part(cols):
        raw_out_ref[:, cols] = jnp.dot(hw_ref[...], win_ref[:, pl.ds(IN_XBC + cols.start, cols.size)],
                                       preferred_element_type=F32)

    def rope_tables():
        ang = pos_ref[...] * invf_ref[...]
        cos_ref[...] = jnp.cos(ang)
        sin_ref[...] = jnp.sin(ang) * sgn_ref[...]

    def rope(t):
        reps = ATT_WIDTH // LANES
        cos = jnp.concatenate([cos_ref[...]] * reps, axis=1)
        sin = jnp.concatenate([sin_ref[...]] * reps, axis=1)
        lane = lax.broadcasted_iota(jnp.int32, (tm, ATT_WIDTH), 1)
        first_half = (lane & (ATT_HEAD_DIM - 1)) < (ROPE_DIM // 2)
        partner = jnp.where(first_half,
                            pltpu.roll(t, ATT_WIDTH - ROPE_DIM // 2, 1),
                            pltpu.roll(t, ROPE_DIM // 2, 1))
        return t * cos + partner * sin

    def qkv_part():
        t = jnp.dot(h_ref[...], win_ref[:, :IN_Z], preferred_element_type=F32)
        q = rope(t[:, :ATT_WIDTH]) * (ATT_HEAD_DIM ** -0.5)
        k = rope(t[:, ATT_WIDTH:2 * ATT_WIDTH])
        qkv_ref[...] = jnp.concatenate([q, k, t[:, 2 * ATT_WIDTH:]], axis=1).astype(BF16)

    def regroup_part():
        regrouped = jnp.dot(perm_ref[...], qkv_ref[...], preferred_element_type=F32).astype(BF16)
        qkvg_ref[...] = regrouped.reshape(qkvg_ref.shape)

    def z_part(cols):
        z = jnp.dot(hw_ref[...], win_ref[:, pl.ds(IN_Z + cols.start, cols.size)],
                    preferred_element_type=F32)
        zs_ref[:, cols] = _silu(z).astype(BF16)

    def dt_part():
        hw = hw_ref[...]
        dt_raw = jnp.dot(hw, win_ref[:, IN_DT:IN_DT + LANES],
                         preferred_element_type=F32)[:, :SSD_HEADS]
        dt_ref[...] = _softplus(dt_raw + dtb_row_ref[...])
        dtT_raw = lax.dot_general(wdtT_ref[...], hw, _NT, preferred_element_type=F32)
        dtT_ref[...] = _softplus(dtT_raw + dtb_col_ref[...])

    tail_rows = (SSD_CONV - 1) * SUBLANES
    tail_start = SSD_CHUNK - tail_rows
    first_sublane = lax.broadcasted_iota(jnp.int32, (SUBLANES, COL_CHUNK), 0) == 0

    def conv_chunk(src):
        raw = raw_in_ref[:, src]
        taps = [convw_ref[k:k + 1, src] for k in range(SSD_CONV)]
        prev_tail = hist_ref[:, src]
        pieces = []
        for ck in range(tm // SSD_CHUNK):
            cur = raw[ck * SSD_CHUNK:(ck + 1) * SSD_CHUNK]
            cur_tail = cur[tail_start:]
            wrapped = []
            for k in range(SSD_CONV - 1):
                rows = slice(k * SUBLANES, (k + 1) * SUBLANES)
                wrapped.append(jnp.where(first_sublane, pltpu.roll(prev_tail[rows], 1, 0),
                                         pltpu.roll(cur_tail[rows], 1, 0)))
            ext = jnp.concatenate(wrapped + [cur], axis=0)
            acc = cur * taps[SSD_CONV - 1] + convb_ref[:, src]
            for s in range(1, SSD_CONV):
                lo = (SSD_CONV - 1 - s) * SUBLANES
                acc = acc + ext[lo:lo + SSD_CHUNK] * taps[SSD_CONV - 1 - s]
            pieces.append(acc)
            prev_tail = cur_tail
        hist_ref[:, src] = prev_tail
        xbc_ref[:, src] = _silu(jnp.concatenate(pieces, axis=0)).astype(BF16)

    blk = ATT_BLOCK
    k_at, v_at = ATT_WIDTH, 2 * ATT_WIDTH

    def with_prev(j, part, at, cols):
        cur_cols = slice(at + cols.start, at + cols.stop)
        if j == 0:
            return jnp.concatenate([kv_prev_ref[part, :, cols], qkv_ref[:blk, cur_cols]], axis=0)
        return qkv_ref[(j - 1) * blk:(j + 1) * blk, cur_cols]

    def emit(j, o, lse):
        o1_ref[j * blk:(j + 1) * blk, :] = o
        l1_ref[j * blk:(j + 1) * blk, :] = lse

    band = band_ref[...]
    attention = _attention_units(lambda j, cols: qkv_ref[j * blk:(j + 1) * blk, cols],
                                 lambda j, cols: with_prev(j, 0, k_at, cols),
                                 lambda j, cols: with_prev(j, 1, v_at, cols),
                                 band, _first_block_bias(band, has_prev), tm // blk, emit)

    chunks = lambda width, w: [pl.ds(c * w, w) for c in range(width // w)]
    conv = chunks(SSD_CONV_CH, COL_CHUNK)
    for fn in (rope_tables, qkv_part, regroup_part):
        fn()
        conv_chunk(conv.pop(0))
    parts = ([(xbc_part, (c,)) for c in chunks(SSD_CONV_CH, PROJ_CHUNK)]
             + [(z_part, (c,)) for c in chunks(SSD_INNER, PROJ_CHUNK)] + [(dt_part, ())])
    for n, (fn, args) in enumerate(parts):
        fn(*args)
        left = len(parts) - n
        for _ in range(-(-len(conv) // left)):
            conv_chunk(conv.pop(0))
        for _ in range(-(-len(attention) // left)):
            attention.pop(0)()
    assert not conv and not attention


def _inproj(x1, mod, nw, posf, invf, sgn, perm, weave, band, w_in, wdtT,
            conv_w, conv_b, dtb_row, dtb_col):
    tm = TOKEN_TILE
    tiles = SEQ // tm
    n_tiles = TOKENS // tm
    cur = lambda i: jnp.minimum(i, n_tiles - 1)
    done = lambda i: jnp.maximum(i - 1, 0)
    tok = lambda w, d, at=cur: (jax.ShapeDtypeStruct((TOKENS, w), d),
                                pl.BlockSpec((tm, w), lambda i: (at(i), 0)))
    grouped = (jax.ShapeDtypeStruct((BATCH, tiles, REGROUP, tm // REGROUP, 3 * ATT_WIDTH), BF16),
               pl.BlockSpec((None, None, REGROUP, tm // REGROUP, 3 * ATT_WIDTH),
                            lambda i: (cur(i) // tiles, cur(i) % tiles, 0, 0, 0)))
    outs = [tok(ATT_WIDTH, BF16), tok(LANES, F32),
        grouped,
        tok(SSD_INNER, BF16),
        tok(SSD_CONV_CH, BF16, done),
        tok(SSD_HEADS, F32),
        (jax.ShapeDtypeStruct((SSD_HEADS, TOKENS), F32),
         pl.BlockSpec((SSD_HEADS, tm), lambda i: (0, cur(i)))),
    ]
    return pl.pallas_call(
        _inproj_kernel,
        grid=(n_tiles + 1,),
        in_specs=[tok(D_MODEL, F32)[1],
                  pl.BlockSpec((1, N_MOD, D_MODEL), lambda i: (cur(i) // tiles, 0, 0)),
                  _resident((1, D_MODEL)),
                  tok(LANES, F32)[1], _resident((1, LANES)), _resident((1, LANES)),
                  _resident(perm.shape), _resident(weave.shape), _resident(band.shape),
                  _resident(w_in.shape), _resident(wdtT.shape),
                  _resident(conv_w.shape), _resident(conv_b.shape),
                  _resident(dtb_row.shape), _resident(dtb_col.shape)],
        out_specs=[s for _, s in outs],
        out_shape=[s for s, _ in outs],
        scratch_shapes=[pltpu.VMEM((tm, D_MODEL), BF16), pltpu.VMEM((tm, D_MODEL), BF16),
                        pltpu.VMEM(((SSD_CONV - 1) * SUBLANES, SSD_CONV_CH), F32),
                        pltpu.VMEM((tm, LANES), F32), pltpu.VMEM((tm, LANES), F32),
                        pltpu.VMEM((tm, 3 * ATT_WIDTH), BF16),
                        pltpu.VMEM((2, ATT_BLOCK, ATT_WIDTH), BF16),
                        pltpu.VMEM((tm, SSD_CONV_CH), F32), pltpu.VMEM((tm, SSD_CONV_CH), F32)],
        compiler_params=_params("arbitrary"),
        name="in_proj",
    )(x1, mod, nw, posf, invf, sgn, perm, weave, band, w_in, wdtT,
      conv_w, conv_b, dtb_row, dtb_col)


def _band_bias(row_pos):
    blk = ATT_BLOCK
    pos = row_pos(np.arange(blk))
    k_pos = np.concatenate([pos, pos + blk])
    dist = pos[:, None] + blk - k_pos[None, :]
    return jnp.asarray(np.where((dist >= 0) & (dist <= blk), 0.0, NEG_BIG), F32)


def _first_block_bias(band, has_prev):
    ki = lax.broadcasted_iota(jnp.int32, band.shape, 1)
    return jnp.where((ki >= ATT_BLOCK) | has_prev, band, NEG_BIG)


def _attention_units(get_q, get_k, get_v, band, first_bias, n_blocks, emit):
    blk = ATT_BLOCK
    lane_row = lax.broadcasted_iota(jnp.int32, (1, LANES), 1)
    lane = lax.broadcasted_iota(jnp.int32, (blk, LANES), 1)
    low = lane < ATT_HEAD_DIM
    head_mask = [(lane_row < ATT_HEAD_DIM).astype(BF16), (lane_row >= ATT_HEAD_DIM).astype(BF16)]
    units = [(j, pair, half) for j in range(n_blocks)
             for pair in range(ATT_HEADS // 2) for half in range(2)]
    carry = {"scores": None, "outs": [], "o_pairs": [], "lse": jnp.zeros((blk, LANES), F32)}

    def scores(unit):
        j, pair, half = unit
        cols = slice(pair * LANES, (pair + 1) * LANES)
        s = lax.dot_general(get_q(j, cols) * head_mask[half], get_k(j, cols), _NT,
                            preferred_element_type=F32)
        return s + (first_bias if j == 0 else band)

    def run(n):
        j, pair, half = units[n]
        s = scores(units[0]) if n == 0 else carry["scores"]
        if n + 1 < len(units):
            carry["scores"] = scores(units[n + 1])
        m = jnp.max(s, axis=1, keepdims=True)
        p = jnp.exp(s - m)
        den = jnp.sum(p, axis=1, keepdims=True)
        o = jnp.dot(p.astype(BF16), get_v(j, slice(pair * LANES, (pair + 1) * LANES)),
                    preferred_element_type=F32)
        carry["outs"].append(o * (1.0 / den))
        carry["lse"] = jnp.where(lane == 2 * pair + half, m + jnp.log(den), carry["lse"])
        if half == 1:
            outs = carry["outs"]
            carry["o_pairs"].append(jnp.where(low, outs[0], outs[1]).astype(BF16))
            carry["outs"] = []
            if pair == ATT_HEADS // 2 - 1:
                emit(j, jnp.concatenate(carry["o_pairs"], axis=1), carry["lse"])
                carry["o_pairs"], carry["lse"] = [], jnp.zeros((blk, LANES), F32)

    return [lambda n=n: run(n) for n in range(len(units))]


def _attn_kernel(band_ref, q_ref, kp_ref, kc_ref, vp_ref, vc_ref, o_ref, lse_ref):
    blk = ATT_BLOCK
    flat = lambda ref: ref[...].reshape(-1, ref.shape[-1])
    q = flat(q_ref)
    k = jnp.concatenate([flat(kp_ref), flat(kc_ref)], axis=0)
    v = jnp.concatenate([flat(vp_ref), flat(vc_ref)], axis=0)
    band = band_ref[...]
    o_blocks, lse_blocks = [], []

    def emit(j, o, lse):
        o_blocks.append(o)
        lse_blocks.append(lse)

    for unit in _attention_units(lambda j, cols: q[j * blk:(j + 1) * blk, cols],
                                 lambda j, cols: k[j * blk:(j + 2) * blk, cols],
                                 lambda j, cols: v[j * blk:(j + 2) * blk, cols],
                                 band, _first_block_bias(band, pl.program_id(2) > 0),
                                 q.shape[0] // blk, emit):
        unit()
    o_ref[...] = jnp.concatenate(o_blocks, axis=0).reshape(o_ref.shape)
    lse_ref[...] = jnp.concatenate(lse_blocks, axis=0).reshape(lse_ref.shape)


def _attention_regrouped(qkvg, r):
    tiles, rows = qkvg.shape[1], qkvg.shape[3]
    fold = REGROUP // r
    tiles_per_blk = ATT_BLOCK // (fold * rows)
    step_blocks = min(ATT_STEP_BLOCKS, tiles // tiles_per_blk)
    n_blk = tiles // (step_blocks * tiles_per_blk)
    shape6 = lambda w: (BATCH, tiles, fold, r, rows, w)
    qkvg = qkvg.reshape(shape6(3 * ATT_WIDTH))
    blk6 = lambda n_tiles, w: (None, n_tiles, fold, None, rows, w)
    cur = lambda w, part=0: pl.BlockSpec(blk6(step_blocks * tiles_per_blk, w),
                                         lambda b, j, i: (b, i, 0, j, 0, part))
    prev = lambda part: pl.BlockSpec(
        blk6(tiles_per_blk, ATT_WIDTH),
        lambda b, j, i: (b, jnp.maximum(step_blocks * i - 1, 0), 0, j, 0, part))
    per_tile = fold * rows

    def row_pos(rho):
        t = rho // per_tile
        a = (rho // rows) % fold
        return t * per_tile + (rho % rows) * fold + a

    o, lse = pl.pallas_call(
        _attn_kernel,
        grid=(BATCH, r, n_blk),
        in_specs=[_resident((ATT_BLOCK, 2 * ATT_BLOCK)), cur(ATT_WIDTH, 0),
                  prev(1), cur(ATT_WIDTH, 1), prev(2), cur(ATT_WIDTH, 2)],
        out_specs=[cur(ATT_WIDTH), cur(LANES)],
        out_shape=[jax.ShapeDtypeStruct(shape6(ATT_WIDTH), BF16),
                   jax.ShapeDtypeStruct(shape6(LANES), F32)],
        compiler_params=_params("arbitrary", "arbitrary", "arbitrary"),
        name=f"attn_dil{r}",
    )(_band_bias(row_pos), qkvg, qkvg, qkvg, qkvg, qkvg)
    grouped5 = (BATCH, tiles, REGROUP, rows)
    return o.reshape(grouped5 + (ATT_WIDTH,)), lse.reshape(grouped5 + (LANES,))


def _woven_time(i):
    return ((i & (SUBLANES - 1)) << 4) | (i >> 3)


def _ssd_kernel(*refs):
    *io_refs, y_a_ref, sq_a_ref, y_b_ref, sq_b_ref = refs
    state_ref = io_refs[-1]
    first = (pl.program_id(0) == 0) & (pl.program_id(1) == 0)

    @pl.when(first)
    def _():
        y_b_ref[...] = jnp.zeros_like(y_b_ref)
        sq_b_ref[...] = jnp.zeros_like(sq_b_ref)

    @pl.when(pl.program_id(1) == 0)
    def _():
        state_ref[...] = jnp.zeros_like(state_ref)

    @pl.when(pl.program_id(1) % 2 == 0)
    def _():
        _ssd_step(*io_refs, y_a_ref, sq_a_ref, y_b_ref, sq_b_ref)

    @pl.when(pl.program_id(1) % 2 == 1)
    def _():
        _ssd_step(*io_refs, y_b_ref, sq_b_ref, y_a_ref, sq_a_ref)


def _ssd_step(xbc_ref, dt_ref, dtT_ref, zs_ref,
              alog_row_ref, alog_col_ref, dskip_ref, nw_ref, unweave_ref,
              o_ref, state_ref, y_ref, sq_ref, y_done_ref, sq_done_ref):
    q = SSD_CHUNK
    heads = SSD_HEADS // SSD_GROUPS
    gw = heads * SSD_HEAD_DIM

    dtT = dtT_ref[...]
    a_dt_col = dt_ref[...] * (-jnp.exp(alog_row_ref[...]))
    a_dt_row = dtT * (-jnp.exp(alog_col_ref[...]))
    t_row = _woven_time(lax.broadcasted_iota(jnp.int32, (q, q), 0))
    t_col = _woven_time(lax.broadcasted_iota(jnp.int32, (q, q), 1))
    causal = t_row >= t_col
    hi = lax.Precision.HIGHEST
    cs_col = jnp.dot(causal.astype(F32), a_dt_col, precision=hi, preferred_element_type=F32)
    cs_row = jnp.dot(a_dt_row, (t_row <= t_col).astype(F32), precision=hi,
                     preferred_element_type=F32)
    total = cs_row[:, q - 1:q]
    dt_decay_row = cs_row - jnp.log(dtT)
    to_end = dtT * jnp.exp(total - cs_row)
    chunk_decay = jnp.exp(total)
    r2 = lax.broadcasted_iota(jnp.int32, (gw, gw), 0)
    c2 = lax.broadcasted_iota(jnp.int32, (gw, gw), 1)
    eye = (r2 == c2).astype(BF16)
    no_rows = jnp.zeros((SSD_HEAD_DIM, q + SSD_STATE), BF16)
    sq_sum = jnp.zeros((q, LANES), F32)

    def load(g):
        b_at = SSD_INNER + g * SSD_STATE
        c_at = b_at + SSD_GROUPS * SSD_STATE
        b_g = xbc_ref[:, b_at:b_at + SSD_STATE]
        c_g = xbc_ref[:, c_at:c_at + SSD_STATE]
        cb = lax.dot_general(c_g, b_g, _NT, preferred_element_type=F32)
        x_g = xbc_ref[:, g * gw:(g + 1) * gw]
        x_t = lax.dot_general(eye, x_g, _NT, preferred_element_type=F32)
        return b_g, c_g, cb, x_g, x_t

    def weights(g, loaded):
        _, c_g, cb, _, _ = loaded
        c_f = c_g.astype(F32)
        w_parts = []
        for j in range(heads):
            h = g * heads + j
            cs_l = jnp.broadcast_to(cs_col[:, h:h + 1], (q, q))
            w_parts.append(jnp.where(causal, cb * jnp.exp(cs_l - dt_decay_row[h:h + 1]), 0.0)
                           .astype(BF16))
            w_parts.append((c_f * jnp.exp(cs_l)).astype(BF16))
        return jnp.concatenate(w_parts, axis=1)

    def finish(g, loaded, w_all, sq_sum):
        b_g, _, _, x_g, x_t = loaded
        state = state_ref[g]
        rhs = jnp.concatenate([x_t.astype(BF16), state.astype(BF16)], axis=1)
        rhs_rows, end_rows, decay_rows = [], [], []
        for j in range(heads):
            h = g * heads + j
            mine = rhs[j * SSD_HEAD_DIM:(j + 1) * SSD_HEAD_DIM]
            rhs_rows.append(jnp.concatenate([mine if jj == j else no_rows for jj in range(heads)],
                                            axis=1))
            end_rows.append(jnp.broadcast_to(to_end[h:h + 1], (SSD_HEAD_DIM, q)))
            decay_rows.append(jnp.broadcast_to(chunk_decay[h:h + 1], (SSD_HEAD_DIM, SSD_STATE)))
        y = lax.dot_general(w_all, jnp.concatenate(rhs_rows, axis=0), _NT,
                            preferred_element_type=F32)
        xcols = slice(g * gw, (g + 1) * gw)
        y = (y + dskip_ref[:, xcols] * x_g.astype(F32)) * zs_ref[:, xcols].astype(F32)
        y_ref[:, xcols] = y
        sq = y * y
        upd = jnp.dot((x_t * jnp.concatenate(end_rows, axis=0)).astype(BF16), b_g,
                      preferred_element_type=F32)
        state_ref[g] = state * jnp.concatenate(decay_rows, axis=0) + upd
        return sq_sum + sq[:, :LANES] + sq[:, LANES:]

    ms = jnp.sum(sq_done_ref[...], axis=-1, keepdims=True) * (1.0 / SSD_INNER)
    inv_rms = lax.rsqrt(ms + NORM_EPS)

    def emit(g):
        cols = slice(g * gw, (g + 1) * gw)
        y = (y_done_ref[:, cols] * inv_rms * nw_ref[:, cols]).astype(BF16)
        o_ref[:, cols] = jnp.dot(unweave_ref[...], y, preferred_element_type=F32).astype(BF16)

    loaded = {0: load(0)}
    w_all = {0: weights(0, loaded[0])}
    for g in range(SSD_GROUPS):
        if g + 1 < SSD_GROUPS:
            loaded[g + 1] = load(g + 1)
            w_all[g + 1] = weights(g + 1, loaded[g + 1])
        emit(g)
        sq_sum = finish(g, loaded.pop(g), w_all.pop(g), sq_sum)
    sq_ref[...] = sq_sum


def _ssd(xbc, dt, dtT, zs, alog_row, alog_col, dskip, nw, unweave):
    n_chunks = SEQ // SSD_CHUNK
    cur = lambda b, c: b * n_chunks + jnp.minimum(c, n_chunks - 1)
    done = lambda b, c: b * n_chunks + jnp.maximum(c - 1, 0)
    tok = lambda w, at=cur: pl.BlockSpec((SSD_CHUNK, w), lambda b, c: (at(b, c), 0))
    gw = SSD_INNER // SSD_GROUPS
    return pl.pallas_call(
        _ssd_kernel,
        grid=(BATCH, n_chunks + 1),
        in_specs=[tok(SSD_CONV_CH), tok(SSD_HEADS),
                  pl.BlockSpec((SSD_HEADS, SSD_CHUNK), lambda b, c: (0, cur(b, c))),
                  tok(SSD_INNER),
                  _resident((1, SSD_HEADS)), _resident((SSD_HEADS, 1)),
                  _resident((1, SSD_INNER)), _resident((1, SSD_INNER)),
                  _resident(unweave.shape)],
        out_specs=tok(SSD_INNER, done),
        out_shape=jax.ShapeDtypeStruct((TOKENS, SSD_INNER), BF16),
        scratch_shapes=[pltpu.VMEM((SSD_GROUPS, gw, SSD_STATE), F32)]
                       + [pltpu.VMEM((SSD_CHUNK, SSD_INNER), F32),
                          pltpu.VMEM((SSD_CHUNK, LANES), F32)] * 2,
        compiler_params=_params("arbitrary", "arbitrary"),
        name="ssd_scan",
    )(xbc, dt, dtT, zs, alog_row, alog_col, dskip, nw, unweave)


def _out_kernel(x_ref, mod_ref, o1_ref, l1_ref, o4_ref, l4_ref, o16_ref, l16_ref,
                yn_ref, expand_ref, restore_ref, n2w_ref, wg_ref, watt_ref, wssd_ref, wmix_ref,
                n3w_ref, wgu_ref, wd_ref, fnw_ref, out_ref):
    tm = x_ref.shape[0]
    mod = mod_ref[0]
    expand = lambda a: jnp.dot(a.astype(BF16), expand_ref[...], preferred_element_type=F32)
    rows = lambda ref: ref[...].reshape(tm, ref.shape[-1])

    def mix(lse_a, o_a, lse_b, o_b):
        m = jnp.maximum(lse_a, lse_b)
        e_a, e_b = jnp.exp(lse_a - m), jnp.exp(lse_b - m)
        tot = e_a + e_b
        return m + jnp.log(tot), o_b + expand(e_a * (1.0 / tot)) * (o_a - o_b)

    lse_g, o_g = mix(rows(l4_ref), rows(o4_ref).astype(F32), rows(l16_ref), rows(o16_ref).astype(F32))
    restore = restore_ref[...]
    o_g = jnp.dot(restore, o_g.astype(BF16), preferred_element_type=F32)
    lse_rest, lse_g_tok = lse_g, None
    for _ in range(3):
        piece = lse_rest.astype(BF16)
        moved = jnp.dot(restore, piece, preferred_element_type=F32)
        lse_g_tok = moved if lse_g_tok is None else lse_g_tok + moved
        lse_rest = lse_rest - piece.astype(F32)
    _, o_att = mix(l1_ref[...], o1_ref[...].astype(F32), lse_g_tok, o_g)

    y_att = jnp.dot(o_att.astype(BF16), watt_ref[...], preferred_element_type=F32)
    y_ssd = jnp.dot(yn_ref[...], wssd_ref[...], preferred_element_type=F32)
    x1 = x_ref[...]
    h_mix = _rms_mod(x1, n2w_ref[...], mod[3:4], mod[4:5]).astype(BF16)
    gates = _sigmoid(jnp.dot(h_mix, wg_ref[...], preferred_element_type=F32))
    merged = gates[:, :D_MODEL] * y_att + gates[:, D_MODEL:] * y_ssd
    y = jnp.dot(merged.astype(BF16), wmix_ref[...], preferred_element_type=F32)
    x2 = x1 + mod[5:6] * y
    h = _rms_mod(x2, n3w_ref[...], mod[6:7], mod[7:8]).astype(BF16)
    x3 = x2 + (0.5 * mod[8:9]) * _swiglu(h, wgu_ref, wd_ref)
    out_ref[...] = _rms(x3, fnw_ref[...])


def _out(x1, mod, o1, l1, o4, l4, o16, l16, yn, expand, restore, n2w, wg,
         watt, wssd, wmix, n3w, wgu, wd, fnw):
    tm = TOKEN_TILE
    tiles = SEQ // tm
    grouped = lambda w: pl.BlockSpec((None, None, REGROUP, tm // REGROUP, w),
                                     lambda i: (i // tiles, i % tiles, 0, 0, 0))
    return pl.pallas_call(
        _out_kernel,
        grid=(TOKENS // tm,),
        in_specs=[_tile_spec(D_MODEL), _mod_spec(), _tile_spec(ATT_WIDTH), _tile_spec(LANES),
                  grouped(ATT_WIDTH), grouped(LANES), grouped(ATT_WIDTH), grouped(LANES),
                  _tile_spec(SSD_INNER)]
                 + [_resident(a.shape) for a in (expand, restore, n2w, wg, watt, wssd, wmix, n3w,
                                                 wgu, wd, fnw)],
        out_specs=_tile_spec(D_MODEL),
        out_shape=jax.ShapeDtypeStruct((TOKENS, D_MODEL), F32),
        compiler_params=_params("arbitrary"),
        name="mix_out_ffn2",
    )(x1, mod, o1, l1, o4, l4, o16, l16, yn, expand, restore, n2w, wg,
      watt, wssd, wmix, n3w, wgu, wd, fnw)


def _rope_tables():
    inv_freq = ROPE_THETA ** (-jnp.arange(0, ROPE_DIM, 2, dtype=F32) / ROPE_DIM)
    d = np.arange(LANES) % ATT_HEAD_DIM
    half = ROPE_DIM // 2
    invf = jnp.where(d < ROPE_DIM, inv_freq[d % half], 0.0).astype(F32).reshape(1, LANES)
    sgn = np.where(d < half, -1.0, np.where(d < ROPE_DIM, 1.0, 0.0)).astype(np.float32)
    return invf, jnp.asarray(sgn).reshape(1, LANES)


def _regroup_matrix(tm):
    rho = np.arange(tm)
    src = REGROUP * (rho % (tm // REGROUP)) + rho // (tm // REGROUP)
    return (np.arange(tm)[None, :] == src[:, None]).astype(np.float32)


def _weave_matrix(tm):
    rho = np.arange(tm)
    src = (rho // SSD_CHUNK) * SSD_CHUNK + _woven_time(rho % SSD_CHUNK)
    return (np.arange(tm)[None, :] == src[:, None]).astype(np.float32)


def kernel(x, c, positions, w_ada, b_ada, norm1_w, ffn1_w_gu, ffn1_w_down, norm2_w, w_in,
           conv_w, conv_b, dt_bias, a_log, d_skip, ssd_norm_w, w_att_out, w_ssd_out,
           w_mix_out, norm3_w, ffn2_w_gu, ffn2_w_down, final_norm_w):
    l = 0
    row = lambda t: t.reshape(1, -1).astype(F32)
    xf = x.reshape(TOKENS, D_MODEL)
    mod = _ada(c, w_ada, b_ada, l).reshape(BATCH, N_MOD, D_MODEL)

    x1 = _ffn1(xf, mod, row(norm1_w[l]), ffn1_w_gu[l].astype(BF16), ffn1_w_down[l].astype(BF16))

    w = w_in[l].astype(BF16)
    invf, sgn = _rope_tables()
    perm = _regroup_matrix(TOKEN_TILE)
    posf = jnp.broadcast_to(positions.reshape(TOKENS, 1).astype(F32), (TOKENS, LANES))
    o1, l1, qkvg, zs, xbc, dt, dtT = _inproj(
        x1, mod, row(norm2_w[l]), posf, invf, sgn, jnp.asarray(perm, BF16),
        jnp.asarray(_weave_matrix(TOKEN_TILE), BF16), _band_bias(lambda r: r),
        w, w[:, IN_DT:IN_GATES].T,
        conv_w[l], row(conv_b[l]), row(dt_bias[l]), dt_bias[l].reshape(SSD_HEADS, 1))

    o4, l4 = _attention_regrouped(qkvg, 4)
    o16, l16 = _attention_regrouped(qkvg, 16)

    yn = _ssd(xbc, dt, dtT, zs, row(a_log[l]), a_log[l].reshape(SSD_HEADS, 1),
              row(jnp.repeat(d_skip[l], SSD_HEAD_DIM)), row(ssd_norm_w[l]),
              jnp.asarray(_weave_matrix(SSD_CHUNK).T, BF16))

    head_of_col = np.arange(ATT_WIDTH) // ATT_HEAD_DIM
    expand = jnp.asarray(np.arange(LANES)[:, None] == head_of_col[None, :], BF16)
    out = _out(x1, mod, o1, l1, o4, l4, o16, l16, yn, expand, jnp.asarray(perm.T, BF16),
               row(norm2_w[l]), w[:, IN_GATES:], w_att_out[l].astype(BF16),
               w_ssd_out[l].astype(BF16), w_mix_out[l].astype(BF16),
               row(norm3_w[l]), ffn2_w_gu[l].astype(BF16), ffn2_w_down[l].astype(BF16),
               row(final_norm_w))
    return out.reshape(BATCH, SEQ, D_MODEL)
```

```python
import jax
import jax.numpy as jnp
import numpy as np
from jax import lax
from jax.experimental import pallas as pl
from jax.experimental.pallas import tpu as pltpu

F32 = jnp.float32
BF16 = jnp.bfloat16

D_MODEL = 1024
BATCH = 2
SEQ = 8192
TOKENS = BATCH * SEQ
ATT_HEADS = 12
ATT_HEAD_DIM = 64
ATT_WIDTH = ATT_HEADS * ATT_HEAD_DIM
ROPE_DIM = ATT_HEAD_DIM // 4
ROPE_THETA = 500000.0
ATT_BLOCK = 128
SSD_INNER = 2 * D_MODEL
SSD_HEAD_DIM = 64
SSD_HEADS = SSD_INNER // SSD_HEAD_DIM
SSD_GROUPS = 8
SSD_STATE = 128
SSD_CONV = 4
SSD_CHUNK = 128
SSD_CONV_CH = SSD_INNER + 2 * SSD_GROUPS * SSD_STATE
D_FF = 2816
N_MOD = 9
NORM_EPS = 1e-6
IN_Z = 3 * ATT_WIDTH
IN_XBC = IN_Z + SSD_INNER
IN_DT = IN_XBC + SSD_CONV_CH
IN_GATES = IN_DT + SSD_HEADS

LANES = 128
SUBLANES = 8
VMEM_LIMIT = 56 * 1024 * 1024
TOKEN_TILE = 256
FFN1_TILE = 512
COL_CHUNK = 256
PROJ_CHUNK = 1024
REGROUP = 16
ATT_STEP_BLOCKS = 8
ADA_COL_TILE = 1536
NEG_BIG = -1e30

_NT = (((1,), (1,)), ((), ()))


def _params(*sem):
    return pltpu.CompilerParams(dimension_semantics=sem, vmem_limit_bytes=VMEM_LIMIT)


def _resident(shape):
    zeros = (0,) * len(shape)
    return pl.BlockSpec(shape, lambda *_: zeros, pipeline_mode=pl.Buffered(1))


def _sigmoid(x):
    return 0.5 * jnp.tanh(0.5 * x) + 0.5


def _silu(x):
    return x * _sigmoid(x)


def _rms(x, w):
    ms = jnp.sum(x * x, axis=-1, keepdims=True) * (1.0 / x.shape[-1])
    return x * lax.rsqrt(ms + NORM_EPS) * w


def _rms_mod(x, w, shift, scale):
    return _rms(x, w) * (1.0 + scale) + shift


def _swiglu(h, wgu_ref, wd_ref):
    gu = jnp.dot(h, wgu_ref[...], preferred_element_type=F32)
    a = (_silu(gu[:, :D_FF]) * gu[:, D_FF:]).astype(BF16)
    return jnp.dot(a, wd_ref[...], preferred_element_type=F32)


def _ada_kernel(ct_ref, w_ref, b_ref, o_ref):
    act = _silu(ct_ref[...])
    w = w_ref[...]
    rows = [jnp.sum(w * act[:, b:b + 1], axis=0, keepdims=True) for b in range(BATCH)]
    o_ref[...] = jnp.concatenate(rows, axis=0) + b_ref[...]


def _ada(c, w_ada, b_ada, layer):
    n = N_MOD * D_MODEL
    return pl.pallas_call(
        _ada_kernel,
        grid=(n // ADA_COL_TILE,),
        in_specs=[
            pl.BlockSpec((D_MODEL, BATCH), lambda j: (0, 0)),
            pl.BlockSpec((None, D_MODEL, ADA_COL_TILE), lambda j: (layer, 0, j)),
            pl.BlockSpec((None, 1, ADA_COL_TILE), lambda j: (layer, 0, j)),
        ],
        out_specs=pl.BlockSpec((BATCH, ADA_COL_TILE), lambda j: (0, j)),
        out_shape=jax.ShapeDtypeStruct((BATCH, n), F32),
        compiler_params=_params("arbitrary"),
        name="ada_mod",
    )(c.T, w_ada, b_ada.reshape(-1, 1, n))


def _ffn1_kernel(x_ref, mod_ref, nw_ref, wgu_ref, wd_ref, o_ref):
    x = x_ref[...]
    mod = mod_ref[0]
    h = _rms_mod(x, nw_ref[...], mod[0:1], mod[1:2]).astype(BF16)
    o_ref[...] = x + (0.5 * mod[2:3]) * _swiglu(h, wgu_ref, wd_ref)


def _tile_spec(width, tm=TOKEN_TILE):
    return pl.BlockSpec((tm, width), lambda i: (i, 0))


def _mod_spec(tm=TOKEN_TILE):
    per_batch = SEQ // tm
    return pl.BlockSpec((1, N_MOD, D_MODEL), lambda i: (i // per_batch, 0, 0))


def _ffn1(x, mod, nw, wgu, wd):
    return pl.pallas_call(
        _ffn1_kernel,
        grid=(TOKENS // FFN1_TILE,),
        in_specs=[_tile_spec(D_MODEL, FFN1_TILE), _mod_spec(FFN1_TILE), _resident((1, D_MODEL)),
                  _resident(wgu.shape), _resident(wd.shape)],
        out_specs=_tile_spec(D_MODEL, FFN1_TILE),
        out_shape=jax.ShapeDtypeStruct((TOKENS, D_MODEL), F32),
        compiler_params=_params("arbitrary"),
        name="ffn1",
    )(x, mod, nw, wgu, wd)


def _softplus(x):
    return jnp.maximum(x, 0.0) + jnp.log1p(jnp.exp(-jnp.abs(x)))


def _inproj_kernel(*refs):
    *io_refs, raw_a_ref, raw_b_ref = refs
    x_ref, mod_ref, nw_ref = io_refs[:3]
    weave_ref = io_refs[7]
    h_ref, hw_ref, hist_ref, _, _, qkv_ref, kv_prev_ref = io_refs[-7:]
    step = pl.program_id(0)
    tm = x_ref.shape[0]
    tiles_per_seq = SEQ // tm
    n_tiles = pl.num_programs(0) - 1

    @pl.when(step == 0)
    def _():
        raw_b_ref[...] = jnp.zeros_like(raw_b_ref)
        kv_prev_ref[...] = jnp.zeros_like(kv_prev_ref)

    @pl.when((step == 0) | (step % tiles_per_seq == 1))
    def _():
        hist_ref[...] = jnp.zeros_like(hist_ref)

    mod = mod_ref[0]
    h_ref[...] = _rms_mod(x_ref[...], nw_ref[...], mod[3:4], mod[4:5]).astype(BF16)
    hw_ref[...] = jnp.dot(weave_ref[...], h_ref[...], preferred_element_type=F32).astype(BF16)

    @pl.when(step % 2 == 0)
    def _():
        _inproj_step(*io_refs, raw_a_ref, raw_b_ref)

    @pl.when(step % 2 == 1)
    def _():
        _inproj_step(*io_refs, raw_b_ref, raw_a_ref)

    @pl.when(step < n_tiles - 1)
    def _():
        for part in range(2):
            cols = slice((part + 1) * ATT_WIDTH, (part + 2) * ATT_WIDTH)
            kv_prev_ref[part] = qkv_ref[tm - ATT_BLOCK:, cols]


def _inproj_step(x_ref, mod_ref, nw_ref, pos_ref, invf_ref, sgn_ref, perm_ref, weave_ref,
                 band_ref, win_ref, wdtT_ref,
                 convw_ref, convb_ref, dtb_row_ref, dtb_col_ref,
                 o1_ref, l1_ref, qkvg_ref, zs_ref, xbc_ref, dt_ref, dtT_ref,
                 h_ref, hw_ref, hist_ref, cos_ref, sin_ref, qkv_ref, kv_prev_ref,
                 raw_out_ref, raw_in_ref):
    tm = x_ref.shape[0]
    tile = jnp.minimum(pl.program_id(0), pl.num_programs(0) - 2)
    has_prev = tile % (SEQ // tm) != 0

    def xbc_part(cols):
        raw_out_ref[:, cols] = jnp.dot(hw_ref[...], win_ref[:, pl.ds(IN_XBC + cols.start, cols.size)],
                                       preferred_element_type=F32)

    def rope_tables():
        ang = pos_ref[...] * invf_ref[...]
        cos_ref[...] = jnp.cos(ang)
        sin_ref[...] = jnp.sin(ang) * sgn_ref[...]

    def rope(t):
        reps = ATT_WIDTH // LANES
        cos = jnp.concatenate([cos_ref[...]] * reps, axis=1)
        sin = jnp.concatenate([sin_ref[...]] * reps, axis=1)
        lane = lax.broadcasted_iota(jnp.int32, (tm, ATT_WIDTH), 1)
        first_half = (lane & (ATT_HEAD_DIM - 1)) < (ROPE_DIM // 2)
        partner = jnp.where(first_half,
                            pltpu.roll(t, ATT_WIDTH - ROPE_DIM // 2, 1),
                            pltpu.roll(t, ROPE_DIM // 2, 1))
        return t * cos + partner * sin

    def qkv_part():
        t = jnp.dot(h_ref[...], win_ref[:, :IN_Z], preferred_element_type=F32)
        q = rope(t[:, :ATT_WIDTH]) * (ATT_HEAD_DIM ** -0.5)
        k = rope(t[:, ATT_WIDTH:2 * ATT_WIDTH])
        qkv_ref[...] = jnp.concatenate([q, k, t[:, 2 * ATT_WIDTH:]], axis=1).astype(BF16)

    def regroup_part():
        regrouped = jnp.dot(perm_ref[...], qkv_ref[...], preferred_element_type=F32).astype(BF16)
        qkvg_ref[...] = regrouped.reshape(qkvg_ref.shape)

    def z_part(cols):
        z = jnp.dot(hw_ref[...], win_ref[:, pl.ds(IN_Z + cols.start, cols.size)],
                    preferred_element_type=F32)
        zs_ref[:, cols] = _silu(z).astype(BF16)

    def dt_part():
        hw = hw_ref[...]
        dt_raw = jnp.dot(hw, win_ref[:, IN_DT:IN_DT + LANES],
                         preferred_element_type=F32)[:, :SSD_HEADS]
        dt_ref[...] = _softplus(dt_raw + dtb_row_ref[...])
        dtT_raw = lax.dot_general(wdtT_ref[...], hw, _NT, preferred_element_type=F32)
        dtT_ref[...] = _softplus(dtT_raw + dtb_col_ref[...])

    tail_rows = (SSD_CONV - 1) * SUBLANES
    tail_start = SSD_CHUNK - tail_rows
    first_sublane = lax.broadcasted_iota(jnp.int32, (SUBLANES, COL_CHUNK), 0) == 0

    def conv_chunk(src):
        raw = raw_in_ref[:, src]
        taps = [convw_ref[k:k + 1, src] for k in range(SSD_CONV)]
        prev_tail = hist_ref[:, src]
        pieces = []
        for ck in range(tm // SSD_CHUNK):
            cur = raw[ck * SSD_CHUNK:(ck + 1) * SSD_CHUNK]
            cur_tail = cur[tail_start:]
            wrapped = []
            for k in range(SSD_CONV - 1):
                rows = slice(k * SUBLANES, (k + 1) * SUBLANES)
                wrapped.append(jnp.where(first_sublane, pltpu.roll(prev_tail[rows], 1, 0),
                                         pltpu.roll(cur_tail[rows], 1, 0)))
            ext = jnp.concatenate(wrapped + [cur], axis=0)
            acc = cur * taps[SSD_CONV - 1] + convb_ref[:, src]
            for s in range(1, SSD_CONV):
                lo = (SSD_CONV - 1 - s) * SUBLANES
                acc = acc + ext[lo:lo + SSD_CHUNK] * taps[SSD_CONV - 1 - s]
            pieces.append(acc)
            prev_tail = cur_tail
        hist_ref[:, src] = prev_tail
        xbc_ref[:, src] = _silu(jnp.concatenate(pieces, axis=0)).astype(BF16)

    blk = ATT_BLOCK
    k_at, v_at = ATT_WIDTH, 2 * ATT_WIDTH

    def with_prev(j, part, at, cols):
        cur_cols = slice(at + cols.start, at + cols.stop)
        if j == 0:
            return jnp.concatenate([kv_prev_ref[part, :, cols], qkv_ref[:blk, cur_cols]], axis=0)
        return qkv_ref[(j - 1) * blk:(j + 1) * blk, cur_cols]

    def emit(j, o, lse):
        o1_ref[j * blk:(j + 1) * blk, :] = o
        l1_ref[j * blk:(j + 1) * blk, :] = lse

    band = band_ref[...]
    attention = _attention_units(lambda j, cols: qkv_ref[j * blk:(j + 1) * blk, cols],
                                 lambda j, cols: with_prev(j, 0, k_at, cols),
                                 lambda j, cols: with_prev(j, 1, v_at, cols),
                                 band, _first_block_bias(band, has_prev), tm // blk, emit)

    chunks = lambda width, w: [pl.ds(c * w, w) for c in range(width // w)]
    conv = chunks(SSD_CONV_CH, COL_CHUNK)
    for fn in (rope_tables, qkv_part, regroup_part):
        fn()
        conv_chunk(conv.pop(0))
    parts = ([(xbc_part, (c,)) for c in chunks(SSD_CONV_CH, PROJ_CHUNK)]
             + [(z_part, (c,)) for c in chunks(SSD_INNER, PROJ_CHUNK)] + [(dt_part, ())])
    for n, (fn, args) in enumerate(parts):
        fn(*args)
        left = len(parts) - n
        for _ in range(-(-len(conv) // left)):
            conv_chunk(conv.pop(0))
        for _ in range(-(-len(attention) // left)):
            attention.pop(0)()
    assert not conv and not attention


def _inproj(x1, mod, nw, posf, invf, sgn, perm, weave, band, w_in, wdtT,
            conv_w, conv_b, dtb_row, dtb_col):
    tm = TOKEN_TILE
    tiles = SEQ // tm
    n_tiles = TOKENS // tm
    cur = lambda i: jnp.minimum(i, n_tiles - 1)
    done = lambda i: jnp.maximum(i - 1, 0)
    tok = lambda w, d, at=cur: (jax.ShapeDtypeStruct((TOKENS, w), d),
                                pl.BlockSpec((tm, w), lambda i: (at(i), 0)))
    grouped = (jax.ShapeDtypeStruct((BATCH, tiles, REGROUP, tm // REGROUP, 3 * ATT_WIDTH), BF16),
               pl.BlockSpec((None, None, REGROUP, tm // REGROUP, 3 * ATT_WIDTH),
                            lambda i: (cur(i) // tiles, cur(i) % tiles, 0, 0, 0)))
    outs = [tok(ATT_WIDTH, BF16), tok(LANES, F32),
        grouped,
        tok(SSD_INNER, BF16),
        tok(SSD_CONV_CH, BF16, done),
        tok(SSD_HEADS, F32),
        (jax.ShapeDtypeStruct((SSD_HEADS, TOKENS), F32),
         pl.BlockSpec((SSD_HEADS, tm), lambda i: (0, cur(i)))),
    ]
    return pl.pallas_call(
        _inproj_kernel,
        grid=(n_tiles + 1,),
        in_specs=[tok(D_MODEL, F32)[1],
                  pl.BlockSpec((1, N_MOD, D_MODEL), lambda i: (cur(i) // tiles, 0, 0)),
                  _resident((1, D_MODEL)),
                  tok(LANES, F32)[1], _resident((1, LANES)), _resident((1, LANES)),
                  _resident(perm.shape), _resident(weave.shape), _resident(band.shape),
                  _resident(w_in.shape), _resident(wdtT.shape),
                  _resident(conv_w.shape), _resident(conv_b.shape),
                  _resident(dtb_row.shape), _resident(dtb_col.shape)],
        out_specs=[s for _, s in outs],
        out_shape=[s for s, _ in outs],
        scratch_shapes=[pltpu.VMEM((tm, D_MODEL), BF16), pltpu.VMEM((tm, D_MODEL), BF16),
                        pltpu.VMEM(((SSD_CONV - 1) * SUBLANES, SSD_CONV_CH), F32),
                        pltpu.VMEM((tm, LANES), F32), pltpu.VMEM((tm, LANES), F32),
                        pltpu.VMEM((tm, 3 * ATT_WIDTH), BF16),
                        pltpu.VMEM((2, ATT_BLOCK, ATT_WIDTH), BF16),
                        pltpu.VMEM((tm, SSD_CONV_CH), F32), pltpu.VMEM((tm, SSD_CONV_CH), F32)],
        compiler_params=_params("arbitrary"),
        name="in_proj",
    )(x1, mod, nw, posf, invf, sgn, perm, weave, band, w_in, wdtT,
      conv_w, conv_b, dtb_row, dtb_col)


def _band_bias(row_pos):
    blk = ATT_BLOCK
    pos = row_pos(np.arange(blk))
    k_pos = np.concatenate([pos, pos + blk])
    dist = pos[:, None] + blk - k_pos[None, :]
    return jnp.asarray(np.where((dist >= 0) & (dist <= blk), 0.0, NEG_BIG), F32)


def _first_block_bias(band, has_prev):
    ki = lax.broadcasted_iota(jnp.int32, band.shape, 1)
    return jnp.where((ki >= ATT_BLOCK) | has_prev, band, NEG_BIG)


def _attention_units(get_q, get_k, get_v, band, first_bias, n_blocks, emit):
    blk = ATT_BLOCK
    lane_row = lax.broadcasted_iota(jnp.int32, (1, LANES), 1)
    lane = lax.broadcasted_iota(jnp.int32, (blk, LANES), 1)
    low = lane < ATT_HEAD_DIM
    head_mask = [(lane_row < ATT_HEAD_DIM).astype(BF16), (lane_row >= ATT_HEAD_DIM).astype(BF16)]
    units = [(j, pair, half) for j in range(n_blocks)
             for pair in range(ATT_HEADS // 2) for half in range(2)]
    carry = {"scores": None, "outs": [], "o_pairs": [], "lse": jnp.zeros((blk, LANES), F32)}

    def scores(unit):
        j, pair, half = unit
        cols = slice(pair * LANES, (pair + 1) * LANES)
        s = lax.dot_general(get_q(j, cols) * head_mask[half], get_k(j, cols), _NT,
                            preferred_element_type=F32)
        return s + (first_bias if j == 0 else band)

    def run(n):
        j, pair, half = units[n]
        s = scores(units[0]) if n == 0 else carry["scores"]
        if n + 1 < len(units):
            carry["scores"] = scores(units[n + 1])
        m = jnp.max(s, axis=1, keepdims=True)
        p = jnp.exp(s - m)
        den = jnp.sum(p, axis=1, keepdims=True)
        o = jnp.dot(p.astype(BF16), get_v(j, slice(pair * LANES, (pair + 1) * LANES)),
                    preferred_element_type=F32)
        carry["outs"].append(o * (1.0 / den))
        carry["lse"] = jnp.where(lane == 2 * pair + half, m + jnp.log(den), carry["lse"])
        if half == 1:
            outs = carry["outs"]
            carry["o_pairs"].append(jnp.where(low, outs[0], outs[1]).astype(BF16))
            carry["outs"] = []
            if pair == ATT_HEADS // 2 - 1:
                emit(j, jnp.concatenate(carry["o_pairs"], axis=1), carry["lse"])
                carry["o_pairs"], carry["lse"] = [], jnp.zeros((blk, LANES), F32)

    return [lambda n=n: run(n) for n in range(len(units))]


def _attn_kernel(band_ref, q_ref, kp_ref, kc_ref, vp_ref, vc_ref, o_ref, lse_ref):
    blk = ATT_BLOCK
    flat = lambda ref: ref[...].reshape(-1, ref.shape[-1])
    q = flat(q_ref)
    k = jnp.concatenate([flat(kp_ref), flat(kc_ref)], axis=0)
    v = jnp.concatenate([flat(vp_ref), flat(vc_ref)], axis=0)
    band = band_ref[...]
    o_blocks, lse_blocks = [], []

    def emit(j, o, lse):
        o_blocks.append(o)
        lse_blocks.append(lse)

    for unit in _attention_units(lambda j, cols: q[j * blk:(j + 1) * blk, cols],
                                 lambda j, cols: k[j * blk:(j + 2) * blk, cols],
                                 lambda j, cols: v[j * blk:(j + 2) * blk, cols],
                                 band, _first_block_bias(band, pl.program_id(2) > 0),
                                 q.shape[0] // blk, emit):
        unit()
    o_ref[...] = jnp.concatenate(o_blocks, axis=0).reshape(o_ref.shape)
    lse_ref[...] = jnp.concatenate(lse_blocks, axis=0).reshape(lse_ref.shape)


def _attention_regrouped(qkvg, r):
    tiles, rows = qkvg.shape[1], qkvg.shape[3]
    fold = REGROUP // r
    tiles_per_blk = ATT_BLOCK // (fold * rows)
    step_blocks = min(ATT_STEP_BLOCKS, tiles // tiles_per_blk)
    n_blk = tiles // (step_blocks * tiles_per_blk)
    shape6 = lambda w: (BATCH, tiles, fold, r, rows, w)
    qkvg = qkvg.reshape(shape6(3 * ATT_WIDTH))
    blk6 = lambda n_tiles, w: (None, n_tiles, fold, None, rows, w)
    cur = lambda w, part=0: pl.BlockSpec(blk6(step_blocks * tiles_per_blk, w),
                                         lambda b, j, i: (b, i, 0, j, 0, part))
    prev = lambda part: pl.BlockSpec(
        blk6(tiles_per_blk, ATT_WIDTH),
        lambda b, j, i: (b, jnp.maximum(step_blocks * i - 1, 0), 0, j, 0, part))
    per_tile = fold * rows

    def row_pos(rho):
        t = rho // per_tile
        a = (rho // rows) % fold
        return t * per_tile + (rho % rows) * fold + a

    o, lse = pl.pallas_call(
        _attn_kernel,
        grid=(BATCH, r, n_blk),
        in_specs=[_resident((ATT_BLOCK, 2 * ATT_BLOCK)), cur(ATT_WIDTH, 0),
                  prev(1), cur(ATT_WIDTH, 1), prev(2), cur(ATT_WIDTH, 2)],
        out_specs=[cur(ATT_WIDTH), cur(LANES)],
        out_shape=[jax.ShapeDtypeStruct(shape6(ATT_WIDTH), BF16),
                   jax.ShapeDtypeStruct(shape6(LANES), F32)],
        compiler_params=_params("arbitrary", "arbitrary", "arbitrary"),
        name=f"attn_dil{r}",
    )(_band_bias(row_pos), qkvg, qkvg, qkvg, qkvg, qkvg)
    grouped5 = (BATCH, tiles, REGROUP, rows)
    return o.reshape(grouped5 + (ATT_WIDTH,)), lse.reshape(grouped5 + (LANES,))


def _woven_time(i):
    return ((i & (SUBLANES - 1)) << 4) | (i >> 3)


def _ssd_kernel(*refs):
    *io_refs, y_a_ref, sq_a_ref, y_b_ref, sq_b_ref = refs
    state_ref = io_refs[-1]
    first = (pl.program_id(0) == 0) & (pl.program_id(1) == 0)

    @pl.when(first)
    def _():
        y_b_ref[...] = jnp.zeros_like(y_b_ref)
        sq_b_ref[...] = jnp.zeros_like(sq_b_ref)

    @pl.when(pl.program_id(1) == 0)
    def _():
        state_ref[...] = jnp.zeros_like(state_ref)

    @pl.when(pl.program_id(1) % 2 == 0)
    def _():
        _ssd_step(*io_refs, y_a_ref, sq_a_ref, y_b_ref, sq_b_ref)

    @pl.when(pl.program_id(1) % 2 == 1)
    def _():
        _ssd_step(*io_refs, y_b_ref, sq_b_ref, y_a_ref, sq_a_ref)


def _ssd_step(xbc_ref, dt_ref, dtT_ref, zs_ref,
              alog_row_ref, alog_col_ref, dskip_ref, nw_ref, unweave_ref,
              o_ref, state_ref, y_ref, sq_ref, y_done_ref, sq_done_ref):
    q = SSD_CHUNK
    heads = SSD_HEADS // SSD_GROUPS
    gw = heads * SSD_HEAD_DIM

    dtT = dtT_ref[...]
    a_dt_col = dt_ref[...] * (-jnp.exp(alog_row_ref[...]))
    a_dt_row = dtT * (-jnp.exp(alog_col_ref[...]))
    t_row = _woven_time(lax.broadcasted_iota(jnp.int32, (q, q), 0))
    t_col = _woven_time(lax.broadcasted_iota(jnp.int32, (q, q), 1))
    causal = t_row >= t_col
    hi = lax.Precision.HIGHEST
    cs_col = jnp.dot(causal.astype(F32), a_dt_col, precision=hi, preferred_element_type=F32)
    cs_row = jnp.dot(a_dt_row, (t_row <= t_col).astype(F32), precision=hi,
                     preferred_element_type=F32)
    total = cs_row[:, q - 1:q]
    dt_decay_row = cs_row - jnp.log(dtT)
    to_end = dtT * jnp.exp(total - cs_row)
    chunk_decay = jnp.exp(total)
    r2 = lax.broadcasted_iota(jnp.int32, (gw, gw), 0)
    c2 = lax.broadcasted_iota(jnp.int32, (gw, gw), 1)
    eye = (r2 == c2).astype(BF16)
    no_rows = jnp.zeros((SSD_HEAD_DIM, q + SSD_STATE), BF16)
    sq_sum = jnp.zeros((q, LANES), F32)

    def load(g):
        b_at = SSD_INNER + g * SSD_STATE
        c_at = b_at + SSD_GROUPS * SSD_STATE
        b_g = xbc_ref[:, b_at:b_at + SSD_STATE]
        c_g = xbc_ref[:, c_at:c_at + SSD_STATE]
        cb = lax.dot_general(c_g, b_g, _NT, preferred_element_type=F32)
        x_g = xbc_ref[:, g * gw:(g + 1) * gw]
        x_t = lax.dot_general(eye, x_g, _NT, preferred_element_type=F32)
        return b_g, c_g, cb, x_g, x_t

    def weights(g, loaded):
        _, c_g, cb, _, _ = loaded
        c_f = c_g.astype(F32)
        w_parts = []
        for j in range(heads):
            h = g * heads + j
            cs_l = jnp.broadcast_to(cs_col[:, h:h + 1], (q, q))
            w_parts.append(jnp.where(causal, cb * jnp.exp(cs_l - dt_decay_row[h:h + 1]), 0.0)
                           .astype(BF16))
            w_parts.append((c_f * jnp.exp(cs_l)).astype(BF16))
        return jnp.concatenate(w_parts, axis=1)

    def finish(g, loaded, w_all, sq_sum):
        b_g, _, _, x_g, x_t = loaded
        state = state_ref[g]
        rhs = jnp.concatenate([x_t.astype(BF16), state.astype(BF16)], axis=1)
        rhs_rows, end_rows, decay_rows = [], [], []
        for j in range(heads):
            h = g * heads + j
            mine = rhs[j * SSD_HEAD_DIM:(j + 1) * SSD_HEAD_DIM]
            rhs_rows.append(jnp.concatenate([mine if jj == j else no_rows for jj in range(heads)],
                                            axis=1))
            end_rows.append(jnp.broadcast_to(to_end[h:h + 1], (SSD_HEAD_DIM, q)))
            decay_rows.append(jnp.broadcast_to(chunk_decay[h:h + 1], (SSD_HEAD_DIM, SSD_STATE)))
        y = lax.dot_general(w_all, jnp.concatenate(rhs_rows, axis=0), _NT,
                            preferred_element_type=F32)
        xcols = slice(g * gw, (g + 1) * gw)
        y = (y + dskip_ref[:, xcols] * x_g.astype(F32)) * zs_ref[:, xcols].astype(F32)
        y_ref[:, xcols] = y
        sq = y * y
        upd = jnp.dot((x_t * jnp.concatenate(end_rows, axis=0)).astype(BF16), b_g,
                      preferred_element_type=F32)
        state_ref[g] = state * jnp.concatenate(decay_rows, axis=0) + upd
        return sq_sum + sq[:, :LANES] + sq[:, LANES:]

    ms = jnp.sum(sq_done_ref[...], axis=-1, keepdims=True) * (1.0 / SSD_INNER)
    inv_rms = lax.rsqrt(ms + NORM_EPS)

    def emit(g):
        if g % 2:
            return
        cols = slice(g * gw, (g + 2) * gw)
        y = (y_done_ref[:, cols] * inv_rms * nw_ref[:, cols]).astype(BF16)
        o_ref[:, cols] = jnp.dot(unweave_ref[...], y, preferred_element_type=F32).astype(BF16)

    loaded = {0: load(0)}
    w_all = {0: weights(0, loaded[0])}
    for g in range(SSD_GROUPS):
        if g + 1 < SSD_GROUPS:
            loaded[g + 1] = load(g + 1)
            w_all[g + 1] = weights(g + 1, loaded[g + 1])
        emit(g)
        sq_sum = finish(g, loaded.pop(g), w_all.pop(g), sq_sum)
    sq_ref[...] = sq_sum


def _ssd(xbc, dt, dtT, zs, alog_row, alog_col, dskip, nw, unweave):
    n_chunks = SEQ // SSD_CHUNK
    cur = lambda b, c: b * n_chunks + jnp.minimum(c, n_chunks - 1)
    done = lambda b, c: b * n_chunks + jnp.maximum(c - 1, 0)
    tok = lambda w, at=cur: pl.BlockSpec((SSD_CHUNK, w), lambda b, c: (at(b, c), 0))
    gw = SSD_INNER // SSD_GROUPS
    return pl.pallas_call(
        _ssd_kernel,
        grid=(BATCH, n_chunks + 1),
        in_specs=[tok(SSD_CONV_CH), tok(SSD_HEADS),
                  pl.BlockSpec((SSD_HEADS, SSD_CHUNK), lambda b, c: (0, cur(b, c))),
                  tok(SSD_INNER),
                  _resident((1, SSD_HEADS)), _resident((SSD_HEADS, 1)),
                  _resident((1, SSD_INNER)), _resident((1, SSD_INNER)),
                  _resident(unweave.shape)],
        out_specs=tok(SSD_INNER, done),
        out_shape=jax.ShapeDtypeStruct((TOKENS, SSD_INNER), BF16),
        scratch_shapes=[pltpu.VMEM((SSD_GROUPS, gw, SSD_STATE), F32)]
                       + [pltpu.VMEM((SSD_CHUNK, SSD_INNER), F32),
                          pltpu.VMEM((SSD_CHUNK, LANES), F32)] * 2,
        compiler_params=_params("arbitrary", "arbitrary"),
        name="ssd_scan",
    )(xbc, dt, dtT, zs, alog_row, alog_col, dskip, nw, unweave)


def _out_kernel(x_ref, mod_ref, o1_ref, l1_ref, o4_ref, l4_ref, o16_ref, l16_ref,
                yn_ref, expand_ref, restore_ref, n2w_ref, wg_ref, watt_ref, wssd_ref, wmix_ref,
                n3w_ref, wgu_ref, wd_ref, fnw_ref, out_ref):
    tm = x_ref.shape[0]
    mod = mod_ref[0]
    expand = lambda a: jnp.dot(a.astype(BF16), expand_ref[...], preferred_element_type=F32)
    rows = lambda ref: ref[...].reshape(tm, ref.shape[-1])

    def mix(lse_a, o_a, lse_b, o_b):
        m = jnp.maximum(lse_a, lse_b)
        e_a, e_b = jnp.exp(lse_a - m), jnp.exp(lse_b - m)
        tot = e_a + e_b
        return m + jnp.log(tot), o_b + expand(e_a * (1.0 / tot)) * (o_a - o_b)

    lse_g, o_g = mix(rows(l4_ref), rows(o4_ref).astype(F32), rows(l16_ref), rows(o16_ref).astype(F32))
    pieces, lse_rest = [], lse_g
    for _ in range(3):
        pieces.append(lse_rest.astype(BF16))
        lse_rest = lse_rest - pieces[-1].astype(F32)
    moved = jnp.dot(restore_ref[...], jnp.concatenate([o_g.astype(BF16)] + pieces, axis=1),
                    preferred_element_type=F32)
    lse_g_tok = sum(moved[:, ATT_WIDTH + n * LANES:ATT_WIDTH + (n + 1) * LANES] for n in range(3))
    _, o_att = mix(l1_ref[...], o1_ref[...].astype(F32), lse_g_tok, moved[:, :ATT_WIDTH])

    y_att = jnp.dot(o_att.astype(BF16), watt_ref[...], preferred_element_type=F32)
    y_ssd = jnp.dot(yn_ref[...], wssd_ref[...], preferred_element_type=F32)
    x1 = x_ref[...]
    h_mix = _rms_mod(x1, n2w_ref[...], mod[3:4], mod[4:5]).astype(BF16)
    gates = _sigmoid(jnp.dot(h_mix, wg_ref[...], preferred_element_type=F32))
    merged = gates[:, :D_MODEL] * y_att + gates[:, D_MODEL:] * y_ssd
    y = jnp.dot(merged.astype(BF16), wmix_ref[...], preferred_element_type=F32)
    x2 = x1 + mod[5:6] * y
    h = _rms_mod(x2, n3w_ref[...], mod[6:7], mod[7:8]).astype(BF16)
    x3 = x2 + (0.5 * mod[8:9]) * _swiglu(h, wgu_ref, wd_ref)
    out_ref[...] = _rms(x3, fnw_ref[...])


def _out(x1, mod, o1, l1, o4, l4, o16, l16, yn, expand, restore, n2w, wg,
         watt, wssd, wmix, n3w, wgu, wd, fnw):
    tm = TOKEN_TILE
    tiles = SEQ // tm
    grouped = lambda w: pl.BlockSpec((None, None, REGROUP, tm // REGROUP, w),
                                     lambda i: (i // tiles, i % tiles, 0, 0, 0))
    return pl.pallas_call(
        _out_kernel,
        grid=(TOKENS // tm,),
        in_specs=[_tile_spec(D_MODEL), _mod_spec(), _tile_spec(ATT_WIDTH), _tile_spec(LANES),
                  grouped(ATT_WIDTH), grouped(LANES), grouped(ATT_WIDTH), grouped(LANES),
                  _tile_spec(SSD_INNER)]
                 + [_resident(a.shape) for a in (expand, restore, n2w, wg, watt, wssd, wmix, n3w,
                                                 wgu, wd, fnw)],
        out_specs=_tile_spec(D_MODEL),
        out_shape=jax.ShapeDtypeStruct((TOKENS, D_MODEL), F32),
        compiler_params=_params("arbitrary"),
        name="mix_out_ffn2",
    )(x1, mod, o1, l1, o4, l4, o16, l16, yn, expand, restore, n2w, wg,
      watt, wssd, wmix, n3w, wgu, wd, fnw)


def _rope_tables():
    inv_freq = ROPE_THETA ** (-jnp.arange(0, ROPE_DIM, 2, dtype=F32) / ROPE_DIM)
    d = np.arange(LANES) % ATT_HEAD_DIM
    half = ROPE_DIM // 2
    invf = jnp.where(d < ROPE_DIM, inv_freq[d % half], 0.0).astype(F32).reshape(1, LANES)
    sgn = np.where(d < half, -1.0, np.where(d < ROPE_DIM, 1.0, 0.0)).astype(np.float32)
    return invf, jnp.asarray(sgn).reshape(1, LANES)


def _regroup_matrix(tm):
    rho = np.arange(tm)
    src = REGROUP * (rho % (tm // REGROUP)) + rho // (tm // REGROUP)
    return (np.arange(tm)[None, :] == src[:, None]).astype(np.float32)


def _weave_matrix(tm):
    rho = np.arange(tm)
    src = (rho // SSD_CHUNK) * SSD_CHUNK + _woven_time(rho % SSD_CHUNK)
    return (np.arange(tm)[None, :] == src[:, None]).astype(np.float32)


def kernel(x, c, positions, w_ada, b_ada, norm1_w, ffn1_w_gu, ffn1_w_down, norm2_w, w_in,
           conv_w, conv_b, dt_bias, a_log, d_skip, ssd_norm_w, w_att_out, w_ssd_out,
           w_mix_out, norm3_w, ffn2_w_gu, ffn2_w_down, final_norm_w):
    l = 0
    row = lambda t: t.reshape(1, -1).astype(F32)
    xf = x.reshape(TOKENS, D_MODEL)
    mod = _ada(c, w_ada, b_ada, l).reshape(BATCH, N_MOD, D_MODEL)

    x1 = _ffn1(xf, mod, row(norm1_w[l]), ffn1_w_gu[l].astype(BF16), ffn1_w_down[l].astype(BF16))

    w = w_in[l].astype(BF16)
    invf, sgn = _rope_tables()
    perm = _regroup_matrix(TOKEN_TILE)
    posf = jnp.broadcast_to(positions.reshape(TOKENS, 1).astype(F32), (TOKENS, LANES))
    o1, l1, qkvg, zs, xbc, dt, dtT = _inproj(
        x1, mod, row(norm2_w[l]), posf, invf, sgn, jnp.asarray(perm, BF16),
        jnp.asarray(_weave_matrix(TOKEN_TILE), BF16), _band_bias(lambda r: r),
        w, w[:, IN_DT:IN_GATES].T,
        conv_w[l], row(conv_b[l]), row(dt_bias[l]), dt_bias[l].reshape(SSD_HEADS, 1))

    o4, l4 = _attention_regrouped(qkvg, 4)
    o16, l16 = _attention_regrouped(qkvg, 16)

    yn = _ssd(xbc, dt, dtT, zs, row(a_log[l]), a_log[l].reshape(SSD_HEADS, 1),
              row(jnp.repeat(d_skip[l], SSD_HEAD_DIM)), row(ssd_norm_w[l]),
              jnp.asarray(_weave_matrix(SSD_CHUNK).T, BF16))

    head_of_col = np.arange(ATT_WIDTH) // ATT_HEAD_DIM
    expand = jnp.asarray(np.arange(LANES)[:, None] == head_of_col[None, :], BF16)
    out = _out(x1, mod, o1, l1, o4, l4, o16, l16, yn, expand, jnp.asarray(perm.T, BF16),
               row(norm2_w[l]), w[:, IN_GATES:], w_att_out[l].astype(BF16),
               w_ssd_out[l].astype(BF16), w_mix_out[l].astype(BF16),
               row(norm3_w[l]), ffn2_w_gu[l].astype(BF16), ffn2_w_down[l].astype(BF16),
               row(final_norm_w))
    return out.reshape(BATCH, SEQ, D_MODEL)
```

```python
import jax
import jax.numpy as jnp
import numpy as np
from jax import lax
from jax.experimental import pallas as pl
from jax.experimental.pallas import tpu as pltpu

F32 = jnp.float32
BF16 = jnp.bfloat16

D_MODEL = 1024
BATCH = 2
SEQ = 8192
TOKENS = BATCH * SEQ
ATT_HEADS = 12
ATT_HEAD_DIM = 64
ATT_WIDTH = ATT_HEADS * ATT_HEAD_DIM
ROPE_DIM = ATT_HEAD_DIM // 4
ROPE_THETA = 500000.0
ATT_BLOCK = 128
SSD_INNER = 2 * D_MODEL
SSD_HEAD_DIM = 64
SSD_HEADS = SSD_INNER // SSD_HEAD_DIM
SSD_GROUPS = 8
SSD_STATE = 128
SSD_CONV = 4
SSD_CHUNK = 128
SSD_CONV_CH = SSD_INNER + 2 * SSD_GROUPS * SSD_STATE
D_FF = 2816
N_MOD = 9
NORM_EPS = 1e-6
IN_Z = 3 * ATT_WIDTH
IN_XBC = IN_Z + SSD_INNER
IN_DT = IN_XBC + SSD_CONV_CH
IN_GATES = IN_DT + SSD_HEADS

LANES = 128
SUBLANES = 8
VMEM_LIMIT = 56 * 1024 * 1024
TOKEN_TILE = 256
FFN1_TILE = 512
COL_CHUNK = 256
PROJ_CHUNK = 1024
REGROUP = 16
ATT_STEP_BLOCKS = 8
ADA_COL_TILE = 1536
NEG_BIG = -1e30

_NT = (((1,), (1,)), ((), ()))


def _params(*sem):
    return pltpu.CompilerParams(dimension_semantics=sem, vmem_limit_bytes=VMEM_LIMIT)


def _resident(shape):
    zeros = (0,) * len(shape)
    return pl.BlockSpec(shape, lambda *_: zeros, pipeline_mode=pl.Buffered(1))


def _sigmoid(x):
    return 0.5 * jnp.tanh(0.5 * x) + 0.5


def _silu(x):
    return x * _sigmoid(x)


def _rms(x, w):
    ms = jnp.sum(x * x, axis=-1, keepdims=True) * (1.0 / x.shape[-1])
    return x * lax.rsqrt(ms + NORM_EPS) * w


def _rms_mod(x, w, shift, scale):
    return _rms(x, w) * (1.0 + scale) + shift


def _swiglu(h, wgu_ref, wd_ref):
    gu = jnp.dot(h, wgu_ref[...], preferred_element_type=F32)
    a = (_silu(gu[:, :D_FF]) * gu[:, D_FF:]).astype(BF16)
    return jnp.dot(a, wd_ref[...], preferred_element_type=F32)


def _ada_kernel(ct_ref, w_ref, b_ref, o_ref):
    act = _silu(ct_ref[...])
    w = w_ref[...]
    rows = [jnp.sum(w * act[:, b:b + 1], axis=0, keepdims=True) for b in range(BATCH)]
    o_ref[...] = jnp.concatenate(rows, axis=0) + b_ref[...]


def _ada(c, w_ada, b_ada, layer):
    n = N_MOD * D_MODEL
    return pl.pallas_call(
        _ada_kernel,
        grid=(n // ADA_COL_TILE,),
        in_specs=[
            pl.BlockSpec((D_MODEL, BATCH), lambda j: (0, 0)),
            pl.BlockSpec((None, D_MODEL, ADA_COL_TILE), lambda j: (layer, 0, j)),
            pl.BlockSpec((None, 1, ADA_COL_TILE), lambda j: (layer, 0, j)),
        ],
        out_specs=pl.BlockSpec((BATCH, ADA_COL_TILE), lambda j: (0, j)),
        out_shape=jax.ShapeDtypeStruct((BATCH, n), F32),
        compiler_params=_params("arbitrary"),
        name="ada_mod",
    )(c.T, w_ada, b_ada.reshape(-1, 1, n))


def _ffn1_kernel(x_ref, mod_ref, nw_ref, wgu_ref, wd_ref, o_ref):
    x = x_ref[...]
    mod = mod_ref[0]
    h = _rms_mod(x, nw_ref[...], mod[0:1], mod[1:2]).astype(BF16)
    o_ref[...] = x + (0.5 * mod[2:3]) * _swiglu(h, wgu_ref, wd_ref)


def _tile_spec(width, tm=TOKEN_TILE):
    return pl.BlockSpec((tm, width), lambda i: (i, 0))


def _mod_spec(tm=TOKEN_TILE):
    per_batch = SEQ // tm
    return pl.BlockSpec((1, N_MOD, D_MODEL), lambda i: (i // per_batch, 0, 0))


def _ffn1(x, mod, nw, wgu, wd):
    return pl.pallas_call(
        _ffn1_kernel,
        grid=(TOKENS // FFN1_TILE,),
        in_specs=[_tile_spec(D_MODEL, FFN1_TILE), _mod_spec(FFN1_TILE), _resident((1, D_MODEL)),
                  _resident(wgu.shape), _resident(wd.shape)],
        out_specs=_tile_spec(D_MODEL, FFN1_TILE),
        out_shape=jax.ShapeDtypeStruct((TOKENS, D_MODEL), F32),
        compiler_params=_params("arbitrary"),
        name="ffn1",
    )(x, mod, nw, wgu, wd)


def _softplus(x):
    return jnp.maximum(x, 0.0) + jnp.log1p(jnp.exp(-jnp.abs(x)))


def _inproj_kernel(*refs):
    *io_refs, raw_a_ref, raw_b_ref = refs
    x_ref, mod_ref, nw_ref = io_refs[:3]
    weave_ref = io_refs[7]
    h_ref, hw_ref, hist_ref, _, _, qkv_ref, kv_prev_ref = io_refs[-7:]
    step = pl.program_id(0)
    tm = x_ref.shape[0]
    tiles_per_seq = SEQ // tm
    n_tiles = pl.num_programs(0) - 1

    @pl.when(step == 0)
    def _():
        raw_b_ref[...] = jnp.zeros_like(raw_b_ref)
        kv_prev_ref[...] = jnp.zeros_like(kv_prev_ref)

    @pl.when((step == 0) | (step % tiles_per_seq == 1))
    def _():
        hist_ref[...] = jnp.zeros_like(hist_ref)

    mod = mod_ref[0]
    h_ref[...] = _rms_mod(x_ref[...], nw_ref[...], mod[3:4], mod[4:5]).astype(BF16)
    hw_ref[...] = jnp.dot(weave_ref[...], h_ref[...], preferred_element_type=F32).astype(BF16)

    @pl.when(step % 2 == 0)
    def _():
        _inproj_step(*io_refs, raw_a_ref, raw_b_ref)

    @pl.when(step % 2 == 1)
    def _():
        _inproj_step(*io_refs, raw_b_ref, raw_a_ref)

    @pl.when(step < n_tiles - 1)
    def _():
        for part in range(2):
            cols = slice((part + 1) * ATT_WIDTH, (part + 2) * ATT_WIDTH)
            kv_prev_ref[part] = qkv_ref[tm - ATT_BLOCK:, cols]


def _inproj_step(x_ref, mod_ref, nw_ref, pos_ref, invf_ref, sgn_ref, perm_ref, weave_ref,
                 band_ref, win_ref, wdtT_ref,
                 convw_ref, convb_ref, dtb_row_ref, dtb_col_ref,
                 o1_ref, l1_ref, qkvg_ref, zs_ref, xbc_ref, dt_ref, dtT_ref,
                 h_ref, hw_ref, hist_ref, cos_ref, sin_ref, qkv_ref, kv_prev_ref,
                 raw_out_ref, raw_in_ref):
    tm = x_ref.shape[0]
    tile = jnp.minimum(pl.program_id(0), pl.num_programs(0) - 2)
    has_prev = tile % (SEQ // tm) != 0

    def xbc_part(cols):
        raw_out_ref[:, cols] = jnp.dot(hw_ref[...], win_ref[:, pl.ds(IN_XBC + cols.start, cols.size)],
                                       preferred_element_type=F32)

    def rope_tables():
        ang = pos_ref[...] * invf_ref[...]
        cos_ref[...] = jnp.cos(ang)
        sin_ref[...] = jnp.sin(ang) * sgn_ref[...]

    def rope(t):
        reps = ATT_WIDTH // LANES
        cos = jnp.concatenate([cos_ref[...]] * reps, axis=1)
        sin = jnp.concatenate([sin_ref[...]] * reps, axis=1)
        lane = lax.broadcasted_iota(jnp.int32, (tm, ATT_WIDTH), 1)
        first_half = (lane & (ATT_HEAD_DIM - 1)) < (ROPE_DIM // 2)
        partner = jnp.where(first_half,
                            pltpu.roll(t, ATT_WIDTH - ROPE_DIM // 2, 1),
                            pltpu.roll(t, ROPE_DIM // 2, 1))
        return t * cos + partner * sin

    def qkv_part():
        t = jnp.dot(h_ref[...], win_ref[:, :IN_Z], preferred_element_type=F32)
        q = rope(t[:, :ATT_WIDTH]) * (ATT_HEAD_DIM ** -0.5)
        k = rope(t[:, ATT_WIDTH:2 * ATT_WIDTH])
        qkv_ref[...] = jnp.concatenate([q, k, t[:, 2 * ATT_WIDTH:]], axis=1).astype(BF16)

    def regroup_part():
        regrouped = jnp.dot(perm_ref[...], qkv_ref[...], preferred_element_type=F32).astype(BF16)
        qkvg_ref[...] = regrouped.reshape(qkvg_ref.shape)

    def z_part(cols):
        z = jnp.dot(hw_ref[...], win_ref[:, pl.ds(IN_Z + cols.start, cols.size)],
                    preferred_element_type=F32)
        zs_ref[:, cols] = _silu(z).astype(BF16)

    def dt_part():
        hw = hw_ref[...]
        dt_raw = jnp.dot(hw, win_ref[:, IN_DT:IN_DT + LANES],
                         preferred_element_type=F32)[:, :SSD_HEADS]
        dt_ref[...] = _softplus(dt_raw + dtb_row_ref[...])
        dtT_raw = lax.dot_general(wdtT_ref[...], hw, _NT, preferred_element_type=F32)
        dtT_ref[...] = _softplus(dtT_raw + dtb_col_ref[...])

    tail_rows = (SSD_CONV - 1) * SUBLANES
    tail_start = SSD_CHUNK - tail_rows
    first_sublane = lax.broadcasted_iota(jnp.int32, (SUBLANES, COL_CHUNK), 0) == 0

    def conv_chunk(src):
        raw = raw_in_ref[:, src]
        taps = [convw_ref[k:k + 1, src] for k in range(SSD_CONV)]
        prev_tail = hist_ref[:, src]
        pieces = []
        for ck in range(tm // SSD_CHUNK):
            cur = raw[ck * SSD_CHUNK:(ck + 1) * SSD_CHUNK]
            cur_tail = cur[tail_start:]
            wrapped = []
            for k in range(SSD_CONV - 1):
                rows = slice(k * SUBLANES, (k + 1) * SUBLANES)
                wrapped.append(jnp.where(first_sublane, pltpu.roll(prev_tail[rows], 1, 0),
                                         pltpu.roll(cur_tail[rows], 1, 0)))
            ext = jnp.concatenate(wrapped + [cur], axis=0)
            acc = cur * taps[SSD_CONV - 1] + convb_ref[:, src]
            for s in range(1, SSD_CONV):
                lo = (SSD_CONV - 1 - s) * SUBLANES
                acc = acc + ext[lo:lo + SSD_CHUNK] * taps[SSD_CONV - 1 - s]
            pieces.append(acc)
            prev_tail = cur_tail
        hist_ref[:, src] = prev_tail
        xbc_ref[:, src] = _silu(jnp.concatenate(pieces, axis=0)).astype(BF16)

    blk = ATT_BLOCK
    k_at, v_at = ATT_WIDTH, 2 * ATT_WIDTH

    def with_prev(j, part, at, cols):
        cur_cols = slice(at + cols.start, at + cols.stop)
        if j == 0:
            return jnp.concatenate([kv_prev_ref[part, :, cols], qkv_ref[:blk, cur_cols]], axis=0)
        return qkv_ref[(j - 1) * blk:(j + 1) * blk, cur_cols]

    def emit(j, o, lse):
        o1_ref[j * blk:(j + 1) * blk, :] = o
        l1_ref[j * blk:(j + 1) * blk, :] = lse

    band = band_ref[...]
    attention = _attention_units(lambda j, cols: qkv_ref[j * blk:(j + 1) * blk, cols],
                                 lambda j, cols: with_prev(j, 0, k_at, cols),
                                 lambda j, cols: with_prev(j, 1, v_at, cols),
                                 band, _first_block_bias(band, has_prev), tm // blk, emit)

    chunks = lambda width, w: [pl.ds(c * w, w) for c in range(width // w)]
    conv = chunks(SSD_CONV_CH, COL_CHUNK)
    for fn in (rope_tables, qkv_part, regroup_part):
        fn()
        conv_chunk(conv.pop(0))
    parts = ([(xbc_part, (c,)) for c in chunks(SSD_CONV_CH, PROJ_CHUNK)]
             + [(z_part, (c,)) for c in chunks(SSD_INNER, PROJ_CHUNK)] + [(dt_part, ())])
    for n, (fn, args) in enumerate(parts):
        fn(*args)
        left = len(parts) - n
        for _ in range(-(-len(conv) // left)):
            conv_chunk(conv.pop(0))
        for _ in range(-(-len(attention) // left)):
            attention.pop(0)()
    assert not conv and not attention


def _inproj(x1, mod, nw, posf, invf, sgn, perm, weave, band, w_in, wdtT,
            conv_w, conv_b, dtb_row, dtb_col):
    tm = TOKEN_TILE
    tiles = SEQ // tm
    n_tiles = TOKENS // tm
    cur = lambda i: jnp.minimum(i, n_tiles - 1)
    done = lambda i: jnp.maximum(i - 1, 0)
    tok = lambda w, d, at=cur: (jax.ShapeDtypeStruct((TOKENS, w), d),
                                pl.BlockSpec((tm, w), lambda i: (at(i), 0)))
    grouped = (jax.ShapeDtypeStruct((BATCH, tiles, REGROUP, tm // REGROUP, 3 * ATT_WIDTH), BF16),
               pl.BlockSpec((None, None, REGROUP, tm // REGROUP, 3 * ATT_WIDTH),
                            lambda i: (cur(i) // tiles, cur(i) % tiles, 0, 0, 0)))
    outs = [tok(ATT_WIDTH, BF16), tok(LANES, F32),
        grouped,
        tok(SSD_INNER, BF16),
        tok(SSD_CONV_CH, BF16, done),
        tok(SSD_HEADS, F32),
        (jax.ShapeDtypeStruct((SSD_HEADS, TOKENS), F32),
         pl.BlockSpec((SSD_HEADS, tm), lambda i: (0, cur(i)))),
    ]
    return pl.pallas_call(
        _inproj_kernel,
        grid=(n_tiles + 1,),
        in_specs=[tok(D_MODEL, F32)[1],
                  pl.BlockSpec((1, N_MOD, D_MODEL), lambda i: (cur(i) // tiles, 0, 0)),
                  _resident((1, D_MODEL)),
                  tok(LANES, F32)[1], _resident((1, LANES)), _resident((1, LANES)),
                  _resident(perm.shape), _resident(weave.shape), _resident(band.shape),
                  _resident(w_in.shape), _resident(wdtT.shape),
                  _resident(conv_w.shape), _resident(conv_b.shape),
                  _resident(dtb_row.shape), _resident(dtb_col.shape)],
        out_specs=[s for _, s in outs],
        out_shape=[s for s, _ in outs],
        scratch_shapes=[pltpu.VMEM((tm, D_MODEL), BF16), pltpu.VMEM((tm, D_MODEL), BF16),
                        pltpu.VMEM(((SSD_CONV - 1) * SUBLANES, SSD_CONV_CH), F32),
                        pltpu.VMEM((tm, LANES), F32), pltpu.VMEM((tm, LANES), F32),
                        pltpu.VMEM((tm, 3 * ATT_WIDTH), BF16),
                        pltpu.VMEM((2, ATT_BLOCK, ATT_WIDTH), BF16),
                        pltpu.VMEM((tm, SSD_CONV_CH), F32), pltpu.VMEM((tm, SSD_CONV_CH), F32)],
        compiler_params=_params("arbitrary"),
        name="in_proj",
    )(x1, mod, nw, posf, invf, sgn, perm, weave, band, w_in, wdtT,
      conv_w, conv_b, dtb_row, dtb_col)


def _band_bias(row_pos):
    blk = ATT_BLOCK
    pos = row_pos(np.arange(blk))
    k_pos = np.concatenate([pos, pos + blk])
    dist = pos[:, None] + blk - k_pos[None, :]
    return jnp.asarray(np.where((dist >= 0) & (dist <= blk), 0.0, NEG_BIG), F32)


def _first_block_bias(band, has_prev):
    ki = lax.broadcasted_iota(jnp.int32, band.shape, 1)
    return jnp.where((ki >= ATT_BLOCK) | has_prev, band, NEG_BIG)


def _attention_units(get_q, get_k, get_v, band, first_bias, n_blocks, emit):
    blk = ATT_BLOCK
    lane_row = lax.broadcasted_iota(jnp.int32, (1, LANES), 1)
    lane = lax.broadcasted_iota(jnp.int32, (blk, LANES), 1)
    low = lane < ATT_HEAD_DIM
    head_mask = [(lane_row < ATT_HEAD_DIM).astype(BF16), (lane_row >= ATT_HEAD_DIM).astype(BF16)]
    units = [(j, pair, half) for j in range(n_blocks)
             for pair in range(ATT_HEADS // 2) for half in range(2)]
    carry = {"scores": None, "outs": [], "o_pairs": [], "lse": jnp.zeros((blk, LANES), F32)}

    def scores(unit):
        j, pair, half = unit
        cols = slice(pair * LANES, (pair + 1) * LANES)
        s = lax.dot_general(get_q(j, cols) * head_mask[half], get_k(j, cols), _NT,
                            preferred_element_type=F32)
        return s + (first_bias if j == 0 else band)

    def run(n):
        j, pair, half = units[n]
        s = scores(units[0]) if n == 0 else carry["scores"]
        if n + 1 < len(units):
            carry["scores"] = scores(units[n + 1])
        m = jnp.max(s, axis=1, keepdims=True)
        p = jnp.exp(s - m)
        den = jnp.sum(p, axis=1, keepdims=True)
        o = jnp.dot(p.astype(BF16), get_v(j, slice(pair * LANES, (pair + 1) * LANES)),
                    preferred_element_type=F32)
        carry["outs"].append(o * (1.0 / den))
        carry["lse"] = jnp.where(lane == 2 * pair + half, m + jnp.log(den), carry["lse"])
        if half == 1:
            outs = carry["outs"]
            carry["o_pairs"].append(jnp.where(low, outs[0], outs[1]).astype(BF16))
            carry["outs"] = []
            if pair == ATT_HEADS // 2 - 1:
                emit(j, jnp.concatenate(carry["o_pairs"], axis=1), carry["lse"])
                carry["o_pairs"], carry["lse"] = [], jnp.zeros((blk, LANES), F32)

    return [lambda n=n: run(n) for n in range(len(units))]


def _attn_kernel(band_ref, q_ref, kp_ref, kc_ref, vp_ref, vc_ref, o_ref, lse_ref):
    blk = ATT_BLOCK
    flat = lambda ref: ref[...].reshape(-1, ref.shape[-1])
    q = flat(q_ref)
    k = jnp.concatenate([flat(kp_ref), flat(kc_ref)], axis=0)
    v = jnp.concatenate([flat(vp_ref), flat(vc_ref)], axis=0)
    band = band_ref[...]
    o_blocks, lse_blocks = [], []

    def emit(j, o, lse):
        o_blocks.append(o)
        lse_blocks.append(lse)

    for unit in _attention_units(lambda j, cols: q[j * blk:(j + 1) * blk, cols],
                                 lambda j, cols: k[j * blk:(j + 2) * blk, cols],
                                 lambda j, cols: v[j * blk:(j + 2) * blk, cols],
                                 band, _first_block_bias(band, pl.program_id(2) > 0),
                                 q.shape[0] // blk, emit):
        unit()
    o_ref[...] = jnp.concatenate(o_blocks, axis=0).reshape(o_ref.shape)
    lse_ref[...] = jnp.concatenate(lse_blocks, axis=0).reshape(lse_ref.shape)


def _attention_regrouped(qkvg, r):
    tiles, rows = qkvg.shape[1], qkvg.shape[3]
    fold = REGROUP // r
    tiles_per_blk = ATT_BLOCK // (fold * rows)
    step_blocks = min(ATT_STEP_BLOCKS, tiles // tiles_per_blk)
    n_blk = tiles // (step_blocks * tiles_per_blk)
    shape6 = lambda w: (BATCH, tiles, fold, r, rows, w)
    qkvg = qkvg.reshape(shape6(3 * ATT_WIDTH))
    blk6 = lambda n_tiles, w: (None, n_tiles, fold, None, rows, w)
    cur = lambda w, part=0: pl.BlockSpec(blk6(step_blocks * tiles_per_blk, w),
                                         lambda b, j, i: (b, i, 0, j, 0, part))
    prev = lambda part: pl.BlockSpec(
        blk6(tiles_per_blk, ATT_WIDTH),
        lambda b, j, i: (b, jnp.maximum(step_blocks * i - 1, 0), 0, j, 0, part))
    per_tile = fold * rows

    def row_pos(rho):
        t = rho // per_tile
        a = (rho // rows) % fold
        return t * per_tile + (rho % rows) * fold + a

    o, lse = pl.pallas_call(
        _attn_kernel,
        grid=(BATCH, r, n_blk),
        in_specs=[_resident((ATT_BLOCK, 2 * ATT_BLOCK)), cur(ATT_WIDTH, 0),
                  prev(1), cur(ATT_WIDTH, 1), prev(2), cur(ATT_WIDTH, 2)],
        out_specs=[cur(ATT_WIDTH), cur(LANES)],
        out_shape=[jax.ShapeDtypeStruct(shape6(ATT_WIDTH), BF16),
                   jax.ShapeDtypeStruct(shape6(LANES), F32)],
        compiler_params=_params("arbitrary", "arbitrary", "arbitrary"),
        name=f"attn_dil{r}",
    )(_band_bias(row_pos), qkvg, qkvg, qkvg, qkvg, qkvg)
    grouped5 = (BATCH, tiles, REGROUP, rows)
    return o.reshape(grouped5 + (ATT_WIDTH,)), lse.reshape(grouped5 + (LANES,))


def _woven_time(i):
    return ((i & (SUBLANES - 1)) << 4) | (i >> 3)


def _ssd_kernel(*refs):
    *io_refs, y_a_ref, sq_a_ref, y_b_ref, sq_b_ref = refs
    state_ref = io_refs[-1]
    first = (pl.program_id(0) == 0) & (pl.program_id(1) == 0)

    @pl.when(first)
    def _():
        y_b_ref[...] = jnp.zeros_like(y_b_ref)
        sq_b_ref[...] = jnp.zeros_like(sq_b_ref)

    @pl.when(pl.program_id(1) == 0)
    def _():
        state_ref[...] = jnp.zeros_like(state_ref)

    @pl.when(pl.program_id(1) % 2 == 0)
    def _():
        _ssd_step(*io_refs, y_a_ref, sq_a_ref, y_b_ref, sq_b_ref)

    @pl.when(pl.program_id(1) % 2 == 1)
    def _():
        _ssd_step(*io_refs, y_b_ref, sq_b_ref, y_a_ref, sq_a_ref)


def _ssd_step(xbc_ref, dt_ref, dtT_ref, zs_ref,
              alog_row_ref, alog_col_ref, dskip_ref, nw_ref, unweave_ref,
              o_ref, state_ref, y_ref, sq_ref, y_done_ref, sq_done_ref):
    q = SSD_CHUNK
    heads = SSD_HEADS // SSD_GROUPS
    gw = heads * SSD_HEAD_DIM

    dtT = dtT_ref[...]
    a_dt_col = dt_ref[...] * (-jnp.exp(alog_row_ref[...]))
    a_dt_row = dtT * (-jnp.exp(alog_col_ref[...]))
    t_row = _woven_time(lax.broadcasted_iota(jnp.int32, (q, q), 0))
    t_col = _woven_time(lax.broadcasted_iota(jnp.int32, (q, q), 1))
    causal = t_row >= t_col
    hi = lax.Precision.HIGHEST
    cs_col = jnp.dot(causal.astype(F32), a_dt_col, precision=hi, preferred_element_type=F32)
    cs_row = jnp.dot(a_dt_row, (t_row <= t_col).astype(F32), precision=hi,
                     preferred_element_type=F32)
    total = cs_row[:, q - 1:q]
    dt_decay_row = cs_row - jnp.log(dtT)
    to_end = dtT * jnp.exp(total - cs_row)
    chunk_decay = jnp.exp(total)
    r2 = lax.broadcasted_iota(jnp.int32, (gw, gw), 0)
    c2 = lax.broadcasted_iota(jnp.int32, (gw, gw), 1)
    eye = (r2 == c2).astype(BF16)
    no_rows = jnp.zeros((SSD_HEAD_DIM, q + SSD_STATE), BF16)
    sq_sum = jnp.zeros((q, LANES), F32)

    def load(g):
        b_at = SSD_INNER + g * SSD_STATE
        c_at = b_at + SSD_GROUPS * SSD_STATE
        b_g = xbc_ref[:, b_at:b_at + SSD_STATE]
        c_g = xbc_ref[:, c_at:c_at + SSD_STATE]
        cb = lax.dot_general(c_g, b_g, _NT, preferred_element_type=F32)
        x_g = xbc_ref[:, g * gw:(g + 1) * gw]
        x_t = lax.dot_general(eye, x_g, _NT, preferred_element_type=F32)
        return b_g, c_g, cb, x_g, x_t

    def weights(g, loaded):
        _, c_g, cb, _, _ = loaded
        c_f = c_g.astype(F32)
        w_parts = []
        for j in range(heads):
            h = g * heads + j
            cs_l = jnp.broadcast_to(cs_col[:, h:h + 1], (q, q))
            w_parts.append(jnp.where(causal, cb * jnp.exp(cs_l - dt_decay_row[h:h + 1]), 0.0)
                           .astype(BF16))
            w_parts.append((c_f * jnp.exp(cs_l)).astype(BF16))
        return jnp.concatenate(w_parts, axis=1)

    def finish(g, loaded, w_all, sq_sum):
        b_g, _, _, x_g, x_t = loaded
        state = state_ref[g]
        rhs = jnp.concatenate([x_t.astype(BF16), state.astype(BF16)], axis=1)
        rhs_rows, end_rows, decay_rows = [], [], []
        for j in range(heads):
            h = g * heads + j
            mine = rhs[j * SSD_HEAD_DIM:(j + 1) * SSD_HEAD_DIM]
            rhs_rows.append(jnp.concatenate([mine if jj == j else no_rows for jj in range(heads)],
                                            axis=1))
            end_rows.append(jnp.broadcast_to(to_end[h:h + 1], (SSD_HEAD_DIM, q)))
            decay_rows.append(jnp.broadcast_to(chunk_decay[h:h + 1], (SSD_HEAD_DIM, SSD_STATE)))
        y = lax.dot_general(w_all, jnp.concatenate(rhs_rows, axis=0), _NT,
                            preferred_element_type=F32)
        xcols = slice(g * gw, (g + 1) * gw)
        y = (y + dskip_ref[:, xcols] * x_g.astype(F32)) * zs_ref[:, xcols].astype(F32)
        y_ref[:, xcols] = y
        sq = y * y
        upd = jnp.dot((x_t * jnp.concatenate(end_rows, axis=0)).astype(BF16), b_g,
                      preferred_element_type=F32)
        state_ref[g] = state * jnp.concatenate(decay_rows, axis=0) + upd
        return sq_sum + sq[:, :LANES] + sq[:, LANES:]

    ms = jnp.sum(sq_done_ref[...], axis=-1, keepdims=True) * (1.0 / SSD_INNER)
    inv_rms = lax.rsqrt(ms + NORM_EPS)

    def emit(g):
        if g % 4:
            return
        cols = slice(g * gw, (g + 4) * gw)
        y = (y_done_ref[:, cols] * inv_rms * nw_ref[:, cols]).astype(BF16)
        o_ref[:, cols] = jnp.dot(unweave_ref[...], y, preferred_element_type=F32).astype(BF16)

    loaded = {0: load(0)}
    w_all = {0: weights(0, loaded[0])}
    for g in range(SSD_GROUPS):
        if g + 1 < SSD_GROUPS:
            loaded[g + 1] = load(g + 1)
            w_all[g + 1] = weights(g + 1, loaded[g + 1])
        emit(g)
        sq_sum = finish(g, loaded.pop(g), w_all.pop(g), sq_sum)
    sq_ref[...] = sq_sum


def _ssd(xbc, dt, dtT, zs, alog_row, alog_col, dskip, nw, unweave):
    n_chunks = SEQ // SSD_CHUNK
    cur = lambda b, c: b * n_chunks + jnp.minimum(c, n_chunks - 1)
    done = lambda b, c: b * n_chunks + jnp.maximum(c - 1, 0)
    tok = lambda w, at=cur: pl.BlockSpec((SSD_CHUNK, w), lambda b, c: (at(b, c), 0))
    gw = SSD_INNER // SSD_GROUPS
    return pl.pallas_call(
        _ssd_kernel,
        grid=(BATCH, n_chunks + 1),
        in_specs=[tok(SSD_CONV_CH), tok(SSD_HEADS),
                  pl.BlockSpec((SSD_HEADS, SSD_CHUNK), lambda b, c: (0, cur(b, c))),
                  tok(SSD_INNER),
                  _resident((1, SSD_HEADS)), _resident((SSD_HEADS, 1)),
                  _resident((1, SSD_INNER)), _resident((1, SSD_INNER)),
                  _resident(unweave.shape)],
        out_specs=tok(SSD_INNER, done),
        out_shape=jax.ShapeDtypeStruct((TOKENS, SSD_INNER), BF16),
        scratch_shapes=[pltpu.VMEM((SSD_GROUPS, gw, SSD_STATE), F32)]
                       + [pltpu.VMEM((SSD_CHUNK, SSD_INNER), F32),
                          pltpu.VMEM((SSD_CHUNK, LANES), F32)] * 2,
        compiler_params=_params("arbitrary", "arbitrary"),
        name="ssd_scan",
    )(xbc, dt, dtT, zs, alog_row, alog_col, dskip, nw, unweave)


def _out_kernel(x_ref, mod_ref, o1_ref, l1_ref, o4_ref, l4_ref, o16_ref, l16_ref,
                yn_ref, expand_ref, restore_ref, n2w_ref, wg_ref, watt_ref, wssd_ref, wmix_ref,
                n3w_ref, wgu_ref, wd_ref, fnw_ref, out_ref):
    tm = x_ref.shape[0]
    mod = mod_ref[0]
    expand = lambda a: jnp.dot(a.astype(BF16), expand_ref[...], preferred_element_type=F32)
    rows = lambda ref: ref[...].reshape(tm, ref.shape[-1])

    def mix(lse_a, o_a, lse_b, o_b):
        m = jnp.maximum(lse_a, lse_b)
        e_a, e_b = jnp.exp(lse_a - m), jnp.exp(lse_b - m)
        tot = e_a + e_b
        return m + jnp.log(tot), o_b + expand(e_a * (1.0 / tot)) * (o_a - o_b)

    lse_g, o_g = mix(rows(l4_ref), rows(o4_ref).astype(F32), rows(l16_ref), rows(o16_ref).astype(F32))
    pieces, lse_rest = [], lse_g
    for _ in range(3):
        pieces.append(lse_rest.astype(BF16))
        lse_rest = lse_rest - pieces[-1].astype(F32)
    moved = jnp.dot(restore_ref[...], jnp.concatenate([o_g.astype(BF16)] + pieces, axis=1),
                    preferred_element_type=F32)
    lse_g_tok = sum(moved[:, ATT_WIDTH + n * LANES:ATT_WIDTH + (n + 1) * LANES] for n in range(3))
    _, o_att = mix(l1_ref[...], o1_ref[...].astype(F32), lse_g_tok, moved[:, :ATT_WIDTH])

    y_att = jnp.dot(o_att.astype(BF16), watt_ref[...], preferred_element_type=F32)
    y_ssd = jnp.dot(yn_ref[...], wssd_ref[...], preferred_element_type=F32)
    x1 = x_ref[...]
    h_mix = _rms_mod(x1, n2w_ref[...], mod[3:4], mod[4:5]).astype(BF16)
    gates = _sigmoid(jnp.dot(h_mix, wg_ref[...], preferred_element_type=F32))
    merged = gates[:, :D_MODEL] * y_att + gates[:, D_MODEL:] * y_ssd
    y = jnp.dot(merged.astype(BF16), wmix_ref[...], preferred_element_type=F32)
    x2 = x1 + mod[5:6] * y
    h = _rms_mod(x2, n3w_ref[...], mod[6:7], mod[7:8]).astype(BF16)
    x3 = x2 + (0.5 * mod[8:9]) * _swiglu(h, wgu_ref, wd_ref)
    out_ref[...] = _rms(x3, fnw_ref[...])


def _out(x1, mod, o1, l1, o4, l4, o16, l16, yn, expand, restore, n2w, wg,
         watt, wssd, wmix, n3w, wgu, wd, fnw):
    tm = TOKEN_TILE
    tiles = SEQ // tm
    grouped = lambda w: pl.BlockSpec((None, None, REGROUP, tm // REGROUP, w),
                                     lambda i: (i // tiles, i % tiles, 0, 0, 0))
    return pl.pallas_call(
        _out_kernel,
        grid=(TOKENS // tm,),
        in_specs=[_tile_spec(D_MODEL), _mod_spec(), _tile_spec(ATT_WIDTH), _tile_spec(LANES),
                  grouped(ATT_WIDTH), grouped(LANES), grouped(ATT_WIDTH), grouped(LANES),
                  _tile_spec(SSD_INNER)]
                 + [_resident(a.shape) for a in (expand, restore, n2w, wg, watt, wssd, wmix, n3w,
                                                 wgu, wd, fnw)],
        out_specs=_tile_spec(D_MODEL),
        out_shape=jax.ShapeDtypeStruct((TOKENS, D_MODEL), F32),
        compiler_params=_params("arbitrary"),
        name="mix_out_ffn2",
    )(x1, mod, o1, l1, o4, l4, o16, l16, yn, expand, restore, n2w, wg,
      watt, wssd, wmix, n3w, wgu, wd, fnw)


def _rope_tables():
    inv_freq = ROPE_THETA ** (-jnp.arange(0, ROPE_DIM, 2, dtype=F32) / ROPE_DIM)
    d = np.arange(LANES) % ATT_HEAD_DIM
    half = ROPE_DIM // 2
    invf = jnp.where(d < ROPE_DIM, inv_freq[d % half], 0.0).astype(F32).reshape(1, LANES)
    sgn = np.where(d < half, -1.0, np.where(d < ROPE_DIM, 1.0, 0.0)).astype(np.float32)
    return invf, jnp.asarray(sgn).reshape(1, LANES)


def _regroup_matrix(tm):
    rho = np.arange(tm)
    src = REGROUP * (rho % (tm // REGROUP)) + rho // (tm // REGROUP)
    return (np.arange(tm)[None, :] == src[:, None]).astype(np.float32)


def _weave_matrix(tm):
    rho = np.arange(tm)
    src = (rho // SSD_CHUNK) * SSD_CHUNK + _woven_time(rho % SSD_CHUNK)
    return (np.arange(tm)[None, :] == src[:, None]).astype(np.float32)


def kernel(x, c, positions, w_ada, b_ada, norm1_w, ffn1_w_gu, ffn1_w_down, norm2_w, w_in,
           conv_w, conv_b, dt_bias, a_log, d_skip, ssd_norm_w, w_att_out, w_ssd_out,
           w_mix_out, norm3_w, ffn2_w_gu, ffn2_w_down, final_norm_w):
    l = 0
    row = lambda t: t.reshape(1, -1).astype(F32)
    xf = x.reshape(TOKENS, D_MODEL)
    mod = _ada(c, w_ada, b_ada, l).reshape(BATCH, N_MOD, D_MODEL)

    x1 = _ffn1(xf, mod, row(norm1_w[l]), ffn1_w_gu[l].astype(BF16), ffn1_w_down[l].astype(BF16))

    w = w_in[l].astype(BF16)
    invf, sgn = _rope_tables()
    perm = _regroup_matrix(TOKEN_TILE)
    posf = jnp.broadcast_to(positions.reshape(TOKENS, 1).astype(F32), (TOKENS, LANES))
    o1, l1, qkvg, zs, xbc, dt, dtT = _inproj(
        x1, mod, row(norm2_w[l]), posf, invf, sgn, jnp.asarray(perm, BF16),
        jnp.asarray(_weave_matrix(TOKEN_TILE), BF16), _band_bias(lambda r: r),
        w, w[:, IN_DT:IN_GATES].T,
        conv_w[l], row(conv_b[l]), row(dt_bias[l]), dt_bias[l].reshape(SSD_HEADS, 1))

    o4, l4 = _attention_regrouped(qkvg, 4)
    o16, l16 = _attention_regrouped(qkvg, 16)

    yn = _ssd(xbc, dt, dtT, zs, row(a_log[l]), a_log[l].reshape(SSD_HEADS, 1),
              row(jnp.repeat(d_skip[l], SSD_HEAD_DIM)), row(ssd_norm_w[l]),
              jnp.asarray(_weave_matrix(SSD_CHUNK).T, BF16))

    head_of_col = np.arange(ATT_WIDTH) // ATT_HEAD_DIM
    expand = jnp.asarray(np.arange(LANES)[:, None] == head_of_col[None, :], BF16)
    out = _out(x1, mod, o1, l1, o4, l4, o16, l16, yn, expand, jnp.asarray(perm.T, BF16),
               row(norm2_w[l]), w[:, IN_GATES:], w_att_out[l].astype(BF16),
               w_ssd_out[l].astype(BF16), w_mix_out[l].astype(BF16),
               row(norm3_w[l]), ffn2_w_gu[l].astype(BF16), ffn2_w_down[l].astype(BF16),
               row(final_norm_w))
    return out.reshape(BATCH, SEQ, D_MODEL)
```
